```python
import jax, jax.numpy as jnp
from jax import lax
import numpy as np

D_MODEL = 1024
BATCH = 32
SEQ = 2048
DEPTH = 4

N_A_LAYERS = DEPTH // 2
N_B_LAYERS = DEPTH - N_A_LAYERS
LRU_WIDTH = D_MODEL
LRU_HEADS = 8
LRU_BLOCK = LRU_WIDTH // LRU_HEADS
CONV_WIDTH = 4
LRU_C = 8.0
HEAD_DIM = 64
N_HEADS = D_MODEL // HEAD_DIM
N_KV_GROUPS = 4
HEADS_PER_GROUP = N_HEADS // N_KV_GROUPS
CMP_BLOCK = 32
CMP_STRIDE = 16
CMP_HIDDEN = 256
SLC_BLOCK = 64
SLC_TOPK = 16
N_LOCAL_BLOCKS = 2
WINDOW = 512
Q_BLOCK = 128
ROPE_THETA = 10000.0
FFN_HIDDEN = ((8 * D_MODEL // 3 + 255) // 256) * 256
EPS = 1e-6
NEG = -1e30
FORCE = 1e30

kernel_name = "yoco_rglru_nsa_hybrid"


def rmsnorm(x, g):
    xf = x.astype(jnp.float32)
    y = xf * lax.rsqrt(jnp.mean(xf * xf, axis=-1, keepdims=True) + EPS)
    return (y * g.astype(jnp.float32)).astype(x.dtype)


def rope(x, pos):
    half = HEAD_DIM // 2
    freqs = jnp.power(ROPE_THETA, -jnp.arange(half, dtype=jnp.float32) / half)
    ang = pos.astype(jnp.float32)[:, None] * freqs[None, :]
    cos = jnp.cos(ang)[:, None, :]
    sin = jnp.sin(ang)[:, None, :]
    xf = x.astype(jnp.float32)
    x1, x2 = xf[..., :half], xf[..., half:]
    return jnp.concatenate([x1 * cos - x2 * sin, x1 * sin + x2 * cos], axis=-1).astype(x.dtype)


def masked_softmax(s, mask):
    s = jnp.where(mask, s, NEG)
    return jax.nn.softmax(s, axis=-1) * mask


def causal_conv(x, w, b):
    S = x.shape[1]
    xp = jnp.pad(x, ((0, 0), (CONV_WIDTH - 1, 0), (0, 0)))
    out = b
    for k in range(CONV_WIDTH):
        out = out + xp[:, k:k + S] * w[k]
    return out


def rg_lru(x, gate_w, gate_b, lam):
    B, S, R = x.shape
    xf = x.astype(jnp.float32)
    xh = xf.reshape(B, S, LRU_HEADS, LRU_BLOCK)
    gates = jnp.einsum('bshi,khij->kbshj', xh, gate_w.astype(jnp.float32)).reshape(2, B, S, R)
    gates = gates + gate_b.astype(jnp.float32)[:, None, None, :]
    r = jax.nn.sigmoid(gates[0])
    i = jax.nn.sigmoid(gates[1])
    log_a = -LRU_C * jax.nn.softplus(-lam.astype(jnp.float32)) * r
    a = jnp.exp(log_a)
    bterm = jnp.sqrt(-jnp.expm1(2.0 * log_a)) * (i * xf)

    def combine(e1, e2):
        a1, b1 = e1
        a2, b2 = e2
        return a1 * a2, a2 * b1 + b2

    _, h = lax.associative_scan(combine, (a, bterm), axis=1)
    return h.astype(x.dtype)


def recurrent_block(h, norm_g, w_in, conv_w, conv_b, gate_w, gate_b, lam, w_out):
    u = rmsnorm(h, norm_g)
    z = u @ w_in
    y = jax.nn.gelu(z[..., :LRU_WIDTH])
    xr = causal_conv(z[..., LRU_WIDTH:], conv_w, conv_b)
    hr = rg_lru(xr, gate_w, gate_b, lam)
    return (y * hr) @ w_out


def swiglu(h, norm_g, w_in, w_out):
    u = rmsnorm(h, norm_g)
    gu = u @ w_in
    return (jax.nn.silu(gu[..., :FFN_HIDDEN]) * gu[..., FFN_HIDDEN:]) @ w_out


def compress(t, pos_emb, w1, b1, w2, b2):
    B, S = t.shape[0], t.shape[1]
    n_cmp = (S - CMP_BLOCK) // CMP_STRIDE + 1
    idx = jnp.arange(n_cmp)[:, None] * CMP_STRIDE + jnp.arange(CMP_BLOCK)[None, :]
    blocks = t[:, idx] + pos_emb[:, None, :]
    blocks = blocks.transpose(0, 1, 3, 2, 4).reshape(B, n_cmp, N_KV_GROUPS, CMP_BLOCK * HEAD_DIM)
    return jax.nn.gelu(blocks @ w1 + b1) @ w2 + b2


def shared_kv(h, kv_norm, kv_w, k_norm, cmp_pos, cmp_w1, cmp_b1, cmp_w2, cmp_b2):
    B, S, _ = h.shape
    u = rmsnorm(h, kv_norm)
    kv = (u @ kv_w).reshape(B, S, 6, N_KV_GROUPS, HEAD_DIM)
    k_c, v_c, k_s, v_s, k_w, v_w = [kv[:, :, j] for j in range(6)]
    pos = jnp.arange(S)
    k_s = rope(rmsnorm(k_s, k_norm[1]), pos)
    k_w = rope(rmsnorm(k_w, k_norm[2]), pos)
    n_cmp = (S - CMP_BLOCK) // CMP_STRIDE + 1
    cmp_last = jnp.arange(n_cmp) * CMP_STRIDE + CMP_BLOCK - 1
    k_cmp = compress(k_c, cmp_pos[0], cmp_w1[0], cmp_b1[0], cmp_w2[0], cmp_b2[0])
    k_cmp = rope(rmsnorm(k_cmp, k_norm[0]), cmp_last)
    v_cmp = compress(v_c, cmp_pos[1], cmp_w1[1], cmp_b1[1], cmp_w2[1], cmp_b2[1])
    return (k_cmp, v_cmp, k_s, v_s, k_w, v_w)


def nsa_single(args):
    q, gate, k_cmp, v_cmp, k_slc, v_slc, k_win, v_win = args
    S = q.shape[0]
    G, Hg, dk = N_KV_GROUPS, HEADS_PER_GROUP, HEAD_DIM
    n_cmp = k_cmp.shape[0]
    n_slc = S // SLC_BLOCK
    top_n = min(SLC_TOPK, n_slc)
    scale = HEAD_DIM ** -0.5
    cmp_start = jnp.arange(n_cmp) * CMP_STRIDE
    cmp_last = cmp_start + CMP_BLOCK - 1
    slc_start = jnp.arange(n_slc) * SLC_BLOCK
    overlap = jnp.clip(jnp.minimum(cmp_start[:, None] + CMP_BLOCK, slc_start[None, :] + SLC_BLOCK)
                       - jnp.maximum(cmp_start[:, None], slc_start[None, :]), 0).astype(jnp.float32) / CMP_BLOCK
    kb = k_slc.reshape(n_slc, SLC_BLOCK, G, dk).transpose(2, 0, 1, 3)
    vb = v_slc.reshape(n_slc, SLC_BLOCK, G, dk).transpose(2, 0, 1, 3)
    kw_p = jnp.pad(k_win, ((WINDOW, 0), (0, 0), (0, 0)))
    vw_p = jnp.pad(v_win, ((WINDOW, 0), (0, 0), (0, 0)))
    qg = q.reshape(S, G, Hg, dk)
    blk = jnp.arange(n_slc)
    grp = jnp.arange(G)[None, :, None]

    def block(qb):
        s0 = qb * Q_BLOCK
        qt = lax.dynamic_slice_in_dim(qg, s0, Q_BLOCK, 0)
        gt = lax.dynamic_slice_in_dim(gate, s0, Q_BLOCK, 0).reshape(Q_BLOCK, 3, G, Hg)
        t = s0 + jnp.arange(Q_BLOCK)
        s = jnp.einsum('tghd,cgd->tghc', qt, k_cmp).astype(jnp.float32) * scale
        p_cmp = masked_softmax(s, (cmp_last[None, :] <= t[:, None])[:, None, None, :])
        o_cmp = jnp.einsum('tghc,cgd->tghd', p_cmp.astype(v_cmp.dtype), v_cmp)
        imp = jnp.einsum('tgc,cj->tgj', p_cmp.sum(axis=2), overlap)
        cur = (t // SLC_BLOCK)[:, None]
        causal_blk = blk[None, :] <= cur
        forced = (blk[None, :] == 0) | (causal_blk & (cur - blk[None, :] < N_LOCAL_BLOCKS))
        score = jnp.where(forced[:, None, :], FORCE, jnp.where(causal_blk[:, None, :], imp, NEG))
        _, idx = lax.top_k(score, top_n)
        ksel = kb[grp, idx].reshape(Q_BLOCK, G, top_n * SLC_BLOCK, dk)
        vsel = vb[grp, idx].reshape(Q_BLOCK, G, top_n * SLC_BLOCK, dk)
        kpos = (idx[..., None] * SLC_BLOCK + jnp.arange(SLC_BLOCK)).reshape(Q_BLOCK, G, top_n * SLC_BLOCK)
        s = jnp.einsum('tghd,tgkd->tghk', qt, ksel).astype(jnp.float32) * scale
        p = masked_softmax(s, (kpos <= t[:, None, None])[:, :, None, :])
        o_slc = jnp.einsum('tghk,tgkd->tghd', p.astype(vsel.dtype), vsel)
        kw = lax.dynamic_slice_in_dim(kw_p, s0, WINDOW + Q_BLOCK, 0)
        vw = lax.dynamic_slice_in_dim(vw_p, s0, WINDOW + Q_BLOCK, 0)
        kp = s0 - WINDOW + jnp.arange(WINDOW + Q_BLOCK)
        mw = (kp[None, :] <= t[:, None]) & (kp[None, :] > t[:, None] - WINDOW) & (kp[None, :] >= 0)
        s = jnp.einsum('tghd,kgd->tghk', qt, kw).astype(jnp.float32) * scale
        p = masked_softmax(s, mw[:, None, None, :])
        o_win = jnp.einsum('tghk,kgd->tghd', p.astype(vw.dtype), vw)
        o = gt[:, 0][..., None] * o_cmp + gt[:, 1][..., None] * o_slc + gt[:, 2][..., None] * o_win
        return o.reshape(Q_BLOCK, N_HEADS, dk)

    out = lax.map(block, jnp.arange(S // Q_BLOCK))
    return out.reshape(S, N_HEADS, dk)


def nsa_layer(h, norm_g, w_in, gate_b, q_norm_g, w_out, k_cmp, v_cmp, k_s, v_s, k_w, v_w):
    B, S, _ = h.shape
    u = rmsnorm(h, norm_g)
    z = u @ w_in
    q = z[..., :N_HEADS * HEAD_DIM].reshape(B, S, N_HEADS, HEAD_DIM)
    q = rope(rmsnorm(q, q_norm_g), jnp.arange(S))
    gates = jax.nn.sigmoid(z[..., N_HEADS * HEAD_DIM:] + gate_b).reshape(B, S, 3, N_HEADS)
    o = lax.map(nsa_single, (q, gates, k_cmp, v_cmp, k_s, v_s, k_w, v_w))
    return o.reshape(B, S, N_HEADS * HEAD_DIM) @ w_out


def setup_inputs(seed: int = 0) -> dict:
    key = jax.random.key(seed)
    ks = jax.random.split(key, 26)

    def nrm(k, shape, scale):
        return jax.random.normal(k, shape, jnp.float32) * scale

    def gain(k, shape):
        return 1.0 + 0.02 * jax.random.normal(k, shape, jnp.float32)

    R, NA, NB = LRU_WIDTH, N_A_LAYERS, N_B_LAYERS
    u = jax.random.uniform(ks[7], (NA, R), jnp.float32, minval=0.9, maxval=0.999)
    s = u ** (1.0 / LRU_C)
    a_lambda = jnp.log(s) - jnp.log1p(-s)
    nq = N_HEADS * HEAD_DIM + 3 * N_HEADS
    return {
        "x": nrm(ks[0], (BATCH, SEQ, D_MODEL), 1.0),
        "a_norm": gain(ks[1], (NA, D_MODEL)),
        "a_w_in": nrm(ks[2], (NA, D_MODEL, 2 * R), D_MODEL ** -0.5),
        "a_conv_w": nrm(ks[3], (NA, CONV_WIDTH, R), CONV_WIDTH ** -0.5),
        "a_conv_b": nrm(ks[4], (NA, R), 0.01),
        "a_gate_w": nrm(ks[5], (NA, 2, LRU_HEADS, LRU_BLOCK, LRU_BLOCK), LRU_BLOCK ** -0.5),
        "a_gate_b": nrm(ks[6], (NA, 2, R), 0.01),
        "a_lambda": a_lambda,
        "a_w_out": nrm(ks[8], (NA, R, D_MODEL), R ** -0.5),
        "kv_norm": gain(ks[9], (D_MODEL,)),
        "kv_w": nrm(ks[10], (D_MODEL, 6 * N_KV_GROUPS * HEAD_DIM), D_MODEL ** -0.5),
        "k_norm": gain(ks[11], (3, HEAD_DIM)),
        "cmp_pos": nrm(ks[12], (2, CMP_BLOCK, HEAD_DIM), 0.02),
        "cmp_w1": nrm(ks[13], (2, CMP_BLOCK * HEAD_DIM, CMP_HIDDEN), (CMP_BLOCK * HEAD_DIM) ** -0.5),
        "cmp_b1": nrm(ks[14], (2, CMP_HIDDEN), 0.01),
        "cmp_w2": nrm(ks[15], (2, CMP_HIDDEN, HEAD_DIM), CMP_HIDDEN ** -0.5),
        "cmp_b2": nrm(ks[16], (2, HEAD_DIM), 0.01),
        "b_norm": gain(ks[17], (NB, D_MODEL)),
        "b_w_in": nrm(ks[18], (NB, D_MODEL, nq), D_MODEL ** -0.5),
        "b_gate_b": nrm(ks[19], (NB, 3 * N_HEADS), 0.01),
        "q_norm": gain(ks[20], (NB, HEAD_DIM)),
        "b_w_out": nrm(ks[21], (NB, N_HEADS * HEAD_DIM, D_MODEL), (N_HEADS * HEAD_DIM) ** -0.5),
        "f_norm": gain(ks[22], (DEPTH, D_MODEL)),
        "f_w_in": nrm(ks[23], (DEPTH, D_MODEL, 2 * FFN_HIDDEN), D_MODEL ** -0.5),
        "f_w_out": nrm(ks[24], (DEPTH, FFN_HIDDEN, D_MODEL), FFN_HIDDEN ** -0.5),
    }


def reference(x, a_norm, a_w_in, a_conv_w, a_conv_b, a_gate_w, a_gate_b, a_lambda, a_w_out,
              kv_norm, kv_w, k_norm, cmp_pos, cmp_w1, cmp_b1, cmp_w2, cmp_b2,
              b_norm, b_w_in, b_gate_b, q_norm, b_w_out,
              f_norm, f_w_in, f_w_out):
    h = x
    shared = None
    for layer in range(DEPTH):
        if layer < N_A_LAYERS:
            i = layer
            h = h + recurrent_block(h, a_norm[i], a_w_in[i], a_conv_w[i], a_conv_b[i],
                                    a_gate_w[i], a_gate_b[i], a_lambda[i], a_w_out[i])
        else:
            if layer == N_A_LAYERS:
                shared = shared_kv(h, kv_norm, kv_w, k_norm, cmp_pos, cmp_w1, cmp_b1, cmp_w2, cmp_b2)
            j = layer - N_A_LAYERS
            h = h + nsa_layer(h, b_norm[j], b_w_in[j], b_gate_b[j], q_norm[j], b_w_out[j], *shared)
        h = h + swiglu(h, f_norm[layer], f_w_in[layer], f_w_out[layer])
    return h
```

```python
import functools

import jax
import jax.numpy as jnp
from jax import lax
from jax.experimental import pallas as pl
from jax.experimental.pallas import tpu as pltpu

F32 = jnp.float32
BF16 = jnp.bfloat16

D_MODEL = 1024
LRU_WIDTH = D_MODEL
LRU_HEADS = 8
LRU_BLOCK = LRU_WIDTH // LRU_HEADS
CONV_WIDTH = 4
LRU_C = 8.0
HEAD_DIM = 64
N_HEADS = D_MODEL // HEAD_DIM
N_KV_GROUPS = 4
HEADS_PER_GROUP = N_HEADS // N_KV_GROUPS
CMP_BLOCK = 32
CMP_STRIDE = 16
CMP_HIDDEN = 256
SLC_BLOCK = 64
SLC_TOPK = 16
N_LOCAL_BLOCKS = 2
WINDOW = 512
ROPE_THETA = 10000.0
FFN_HIDDEN = 2816
EPS = 1e-6
NEG = -1e30
FORCE = 1e30

LANES = 128
KV_WIDTH = N_KV_GROUPS * HEAD_DIM
Q_TILE = 128
K_TILE = 128
VMEM_LIMIT = 56 * 1024 * 1024


def _cparams(n_axes):
    return pltpu.CompilerParams(dimension_semantics=("arbitrary",) * n_axes,
                                vmem_limit_bytes=VMEM_LIMIT)


def _const_spec(shape):
    nd = len(shape)
    return pl.BlockSpec(shape, lambda *_: (0,) * nd, pipeline_mode=pl.Buffered(1))


def _rms(x, g):
    ms = jnp.mean(x * x, axis=-1, keepdims=True)
    return x * lax.rsqrt(ms + EPS) * g


def _sigmoid(x):
    return 1.0 / (1.0 + jnp.exp(-x))


def _gelu_tanh(x):
    c = 0.7978845608028654
    return x * (0.5 * (1.0 + jnp.tanh(c * (x + 0.044715 * (x * x * x)))))


def _dot(a, b):
    return jnp.dot(a, b, preferred_element_type=F32)


def _dot_nt(a, b):
    return lax.dot_general(a, b, (((1,), (1,)), ((), ())), preferred_element_type=F32)


def _head_mean_sq(x, ones_bd):
    sq = x * x
    hi = sq.astype(BF16)
    lo = (sq - hi.astype(F32)).astype(BF16)
    return (_dot(hi, ones_bd) + _dot(lo, ones_bd)) * (1.0 / HEAD_DIM)


def _rope_flat(x, cos_t, sin_t):
    width = x.shape[-1]
    lane = lax.broadcasted_iota(jnp.int32, x.shape, 1)
    upper = (lane & (HEAD_DIM // 2)) != 0
    partner = jnp.where(upper, pltpu.roll(x, HEAD_DIM // 2, 1),
                        pltpu.roll(x, width - HEAD_DIM // 2, 1))
    return x * cos_t + partner * sin_t


def _tile_lanes(t, width):
    reps = width // t.shape[-1]
    return t if reps == 1 else jnp.concatenate([t] * reps, axis=1)


def _scan_rows(a, b):
    n = a.shape[0]
    row = lax.broadcasted_iota(jnp.int32, a.shape, 0)
    d = 1
    while d < n:
        valid = row >= d
        b = jnp.where(valid, a * pltpu.roll(b, d, 0) + b, b)
        a = jnp.where(valid, a * pltpu.roll(a, d, 0), a)
        d *= 2
    return a, b


def _rec_kernel(x_ref, g_ref, win_ref, cw_ref, cb_ref, wg_ref, gb_ref, lam_ref, wout_ref, o_ref,
                xbuf, hcar, yh):
    ts = x_ref.shape[0]
    R = LRU_WIDTH

    @pl.when(pl.program_id(1) == 0)
    def _():
        xbuf[0:8, :] = jnp.zeros((8, R), F32)
        hcar[...] = jnp.zeros_like(hcar)

    x = x_ref[...]
    u = _rms(x, g_ref[...]).astype(BF16)
    z = _dot(u, win_ref[...])
    xbuf[8:8 + ts, :] = z[:, R:]
    for hh in range(LRU_HEADS):
        cs = slice(hh * LRU_BLOCK, (hh + 1) * LRU_BLOCK)
        cw = cw_ref[:, cs]
        xr = cb_ref[:, cs] + xbuf[5:5 + ts, cs] * cw[0:1]
        xr = xr + xbuf[6:6 + ts, cs] * cw[1:2]
        xr = xr + xbuf[7:7 + ts, cs] * cw[2:3]
        xr = xr + xbuf[8:8 + ts, cs] * cw[3:4]
        gates = _dot(xr.astype(BF16), wg_ref[hh])
        gb = gb_ref[:, cs]
        r = _sigmoid(gates[:, :LRU_BLOCK] + gb[0:1])
        i = _sigmoid(gates[:, LRU_BLOCK:] + gb[1:2])
        lam = lam_ref[:, cs]
        softplus_neg = jnp.maximum(-lam, 0.0) + jnp.log1p(jnp.exp(-jnp.abs(lam)))
        log_a = (-LRU_C * softplus_neg) * r
        a = jnp.exp(log_a)
        bterm = jnp.sqrt(1.0 - a * a) * (i * xr)
        acum, hs = _scan_rows(a, bterm)
        hs = acum * hcar[:, cs] + hs
        hcar[:, cs] = hs[ts - 1:ts, :]
        yh[:, cs] = (_gelu_tanh(z[:, cs]) * hs).astype(BF16)
    xbuf[0:8, :] = xbuf[ts:ts + 8, :]
    o_ref[...] = x + _dot(yh[...], wout_ref[...])


def _recurrent_block(h, norm_g, w_in, conv_w, conv_b, gate_w, gate_b, lam, w_out, ts=256):
    B, S, D = h.shape
    R = LRU_WIDTH
    wg = jnp.concatenate([gate_w[0], gate_w[1]], axis=-1).astype(BF16)
    return pl.pallas_call(
        _rec_kernel,
        grid=(B, S // ts),
        in_specs=[
            pl.BlockSpec((None, ts, D), lambda b, s: (b, s, 0)),
            _const_spec((1, D)),
            _const_spec((D, 2 * R)),
            _const_spec((CONV_WIDTH, R)),
            _const_spec((1, R)),
            _const_spec((LRU_HEADS, LRU_BLOCK, 2 * LRU_BLOCK)),
            _const_spec((2, R)),
            _const_spec((1, R)),
            _const_spec((R, D)),
        ],
        out_specs=pl.BlockSpec((None, ts, D), lambda b, s: (b, s, 0)),
        out_shape=jax.ShapeDtypeStruct((B, S, D), F32),
        scratch_shapes=[pltpu.VMEM((ts + 8, R), F32), pltpu.VMEM((1, R), F32),
                        pltpu.VMEM((ts, R), BF16)],
        compiler_params=_cparams(2),
        name="rglru_block",
    )(h, norm_g.reshape(1, D), w_in.astype(BF16), conv_w, conv_b.reshape(1, R), wg, gate_b,
      lam.reshape(1, R), w_out.astype(BF16))


FFN_CHUNK = FFN_HIDDEN // 2


def _ffn_body(x, g_ref, win_ref, wout_ref, o_ref):
    u = _rms(x, g_ref[...]).astype(BF16)
    acc = x
    for c in range(FFN_HIDDEN // FFN_CHUNK):
        lo = c * FFN_CHUNK
        gate = _dot(u, win_ref[:, lo:lo + FFN_CHUNK])
        up = _dot(u, win_ref[:, FFN_HIDDEN + lo:FFN_HIDDEN + lo + FFN_CHUNK])
        act = ((gate * _sigmoid(gate)) * up).astype(BF16)
        acc = acc + _dot(act, wout_ref[lo:lo + FFN_CHUNK, :])
    o_ref[...] = acc


def _ffn_kernel(x_ref, g_ref, win_ref, wout_ref, o_ref):
    _ffn_body(x_ref[...], g_ref, win_ref, wout_ref, o_ref)


def _proj_ffn_kernel(x_ref, a_ref, wo_ref, g_ref, win_ref, wout_ref, o_ref):
    _ffn_body(x_ref[...] + _dot(a_ref[...], wo_ref[...]), g_ref, win_ref, wout_ref, o_ref)


def _swiglu(h, norm_g, w_in, w_out, attn=None, w_o=None, tm=512):
    B, S, D = h.shape
    M = B * S
    row_spec = pl.BlockSpec((tm, D), lambda i: (i, 0))
    w_specs = [_const_spec((1, D)), _const_spec((D, 2 * FFN_HIDDEN)), _const_spec((FFN_HIDDEN, D))]
    w_args = (norm_g.reshape(1, D), w_in.astype(BF16), w_out.astype(BF16))
    if attn is None:
        kern, specs, args = _ffn_kernel, [row_spec] + w_specs, (h.reshape(M, D),) + w_args
    else:
        kern = _proj_ffn_kernel
        specs = [row_spec, row_spec, _const_spec((D, D))] + w_specs
        args = (h.reshape(M, D), attn.reshape(M, D), w_o.astype(BF16)) + w_args
    out = pl.pallas_call(
        kern,
        grid=(M // tm,),
        in_specs=specs,
        out_specs=row_spec,
        out_shape=jax.ShapeDtypeStruct((M, D), F32),
        compiler_params=_cparams(1),
        name="swiglu_ffn",
    )(*args)
    return out.reshape(B, S, D)


def _kv_kernel(x_ref, g_ref, w_ref, kn_ref, cos_ref, sin_ref, ones_ref,
               kc_ref, vc_ref, ks_ref, vs_ref, kw_ref, vw_ref):
    W = KV_WIDTH
    u = _rms(x_ref[...], g_ref[...]).astype(BF16)
    kv = _dot(u, w_ref[...])
    cos_t = _tile_lanes(cos_ref[...], W)
    sin_t = _tile_lanes(sin_ref[...], W)
    ones_bd = ones_ref[...]

    def key(j, gain):
        k = kv[:, j * W:(j + 1) * W]
        k = k * lax.rsqrt(_head_mean_sq(k, ones_bd) + EPS) * gain
        return _rope_flat(k, cos_t, sin_t).astype(BF16)

    kc_ref[...] = kv[:, 0 * W:1 * W]
    vc_ref[...] = kv[:, 1 * W:2 * W]
    ks_ref[...] = key(2, kn_ref[1:2, :])
    vs_ref[...] = kv[:, 3 * W:4 * W].astype(BF16)
    kw_ref[...] = key(4, kn_ref[2:3, :])
    vw_ref[...] = kv[:, 5 * W:6 * W].astype(BF16)


def _rope_tables(pos):
    half = HEAD_DIM // 2
    freqs = jnp.power(ROPE_THETA, -jnp.arange(half, dtype=F32) / half)
    ang = pos.astype(F32)[:, None] * freqs[None, :]
    cos, sin = jnp.cos(ang), jnp.sin(ang)
    cos_t = jnp.concatenate([cos, cos], axis=-1)
    sin_t = jnp.concatenate([-sin, sin], axis=-1)
    return cos_t, sin_t


def _block_diag_ones(width):
    seg = jnp.arange(width) // HEAD_DIM
    return (seg[:, None] == seg[None, :]).astype(BF16)


def _kv_proj(h, kv_norm, kv_w, k_norm, cos128, sin128, ts=512):
    B, S, D = h.shape
    W = KV_WIDTH
    kn = jnp.tile(k_norm, (1, N_KV_GROUPS))
    row = lambda width: pl.BlockSpec((None, ts, width), lambda b, s: (b, s, 0))
    tab = pl.BlockSpec((ts, LANES), lambda b, s: (s, 0))
    shapes = [jax.ShapeDtypeStruct((B, S, W), dt) for dt in (F32, F32, BF16, BF16, BF16, BF16)]
    return pl.pallas_call(
        _kv_kernel,
        grid=(B, S // ts),
        in_specs=[row(D), _const_spec((1, D)), _const_spec((D, 6 * W)), _const_spec((3, W)),
                  tab, tab, _const_spec((W, W))],
        out_specs=[row(W)] * 6,
        out_shape=shapes,
        compiler_params=_cparams(2),
        name="shared_kv_proj",
    )(h, kv_norm.reshape(1, D), kv_w.astype(BF16), kn, cos128, sin128, _block_diag_ones(W))


N_CHUNK = 128
CHUNK_W = CMP_STRIDE * HEAD_DIM


def _cmp_kernel(x_ref, pos_ref, w1_ref, b1_ref, w2_ref, b2_ref, kn_ref, cos_ref, sin_ref, o_ref):
    rows = x_ref.shape[0]
    x = x_ref[...]
    ya = _dot((x + pos_ref[0:1, :]).astype(BF16), w1_ref[0:CHUNK_W, :])
    yb = _dot((x + pos_ref[1:2, :]).astype(BF16), w1_ref[CHUNK_W:2 * CHUNK_W, :])
    hid = _gelu_tanh(ya + pltpu.roll(yb, rows - 1, 0) + b1_ref[...])
    out = _dot(hid.astype(BF16), w2_ref[...]) + b2_ref[...]

    @pl.when(pl.program_id(0) == 0)
    def _():
        k = _rms(out, kn_ref[...])
        half = HEAD_DIM // 2
        partner = jnp.concatenate([k[:, half:], k[:, :half]], axis=1)
        o_ref[...] = k * cos_ref[...] + partner * sin_ref[...]

    @pl.when(pl.program_id(0) == 1)
    def _():
        o_ref[...] = out


def _compress(kc, vc, cmp_pos, cmp_w1, cmp_b1, cmp_w2, cmp_b2, k_norm0):
    B, S, _ = kc.shape
    G = N_KV_GROUPS
    rows = G * N_CHUNK

    def chunked(t):
        t = t.reshape(B, N_CHUNK, CMP_STRIDE, G, HEAD_DIM).transpose(0, 3, 1, 2, 4)
        return t.reshape(B, rows, CHUNK_W)

    x = jnp.stack([chunked(kc), chunked(vc)])
    pos = cmp_pos.reshape(2, 2, CHUNK_W)
    cmp_last = jnp.arange(N_CHUNK) * CMP_STRIDE + CMP_BLOCK - 1
    cos_t, sin_t = _rope_tables(cmp_last)
    cos_t, sin_t = jnp.tile(cos_t, (G, 1)), jnp.tile(sin_t, (G, 1))
    per_kv = lambda *shape: pl.BlockSpec((None,) + shape, lambda k, b: (k,) + (0,) * len(shape))
    return pl.pallas_call(
        _cmp_kernel,
        grid=(2, B),
        in_specs=[pl.BlockSpec((None, None, rows, CHUNK_W), lambda k, b: (k, b, 0, 0)),
                  per_kv(2, CHUNK_W), per_kv(2 * CHUNK_W, CMP_HIDDEN), per_kv(1, CMP_HIDDEN),
                  per_kv(CMP_HIDDEN, HEAD_DIM), per_kv(1, HEAD_DIM),
                  _const_spec((1, HEAD_DIM)), _const_spec((rows, HEAD_DIM)),
                  _const_spec((rows, HEAD_DIM))],
        out_specs=pl.BlockSpec((None, None, rows, HEAD_DIM), lambda k, b: (k, b, 0, 0)),
        out_shape=jax.ShapeDtypeStruct((2, B, rows, HEAD_DIM), F32),
        compiler_params=_cparams(2),
        name="kv_compress",
    )(x, pos, cmp_w1.astype(BF16), cmp_b1.reshape(2, 1, CMP_HIDDEN), cmp_w2.astype(BF16),
      cmp_b2.reshape(2, 1, HEAD_DIM), k_norm0.reshape(1, HEAD_DIM), cos_t, sin_t)


GATE_PAD = LANES
Q_SCALE = HEAD_DIM ** -0.5


def _q_kernel(x_ref, g_ref, w_ref, gb_ref, qn_ref, cos_ref, sin_ref, ones_ref, q_ref, gate_ref):
    NQ = N_HEADS * HEAD_DIM
    u = _rms(x_ref[...], g_ref[...]).astype(BF16)
    z = _dot(u, w_ref[...])
    gate_ref[...] = _sigmoid(z[:, NQ:] + gb_ref[...])
    ones_bd = ones_ref[...]
    W = ones_bd.shape[0]
    cos_t = _tile_lanes(cos_ref[...], W)
    sin_t = _tile_lanes(sin_ref[...], W)
    for c in range(NQ // W):
        q = z[:, c * W:(c + 1) * W]
        q = q * lax.rsqrt(_head_mean_sq(q, ones_bd) + EPS) * qn_ref[...]
        q_ref[:, c * W:(c + 1) * W] = (_rope_flat(q, cos_t, sin_t) * Q_SCALE).astype(BF16)


def _q_proj(h, norm_g, w_in, gate_b, q_norm_g, cos128, sin128, ts=512):
    B, S, D = h.shape
    NQ = N_HEADS * HEAD_DIM
    n_gate = 3 * N_HEADS
    w = jnp.pad(w_in, ((0, 0), (0, GATE_PAD - n_gate))).astype(BF16)
    gb = jnp.pad(gate_b, (0, GATE_PAD - n_gate)).reshape(1, GATE_PAD)
    W = KV_WIDTH
    qn = jnp.tile(q_norm_g, W // HEAD_DIM).reshape(1, W)
    row = lambda width: pl.BlockSpec((None, ts, width), lambda b, s: (b, s, 0))
    tab = pl.BlockSpec((ts, LANES), lambda b, s: (s, 0))
    return pl.pallas_call(
        _q_kernel,
        grid=(B, S // ts),
        in_specs=[row(D), _const_spec((1, D)), _const_spec((D, NQ + GATE_PAD)),
                  _const_spec((1, GATE_PAD)), _const_spec((1, W)), tab, tab, _const_spec((W, W))],
        out_specs=[row(NQ), row(GATE_PAD)],
        out_shape=[jax.ShapeDtypeStruct((B, S, NQ), BF16),
                   jax.ShapeDtypeStruct((B, S, GATE_PAD), F32)],
        compiler_params=_cparams(2),
        name="nsa_q_proj",
    )(h, norm_g.reshape(1, D), w, gb, qn, cos128, sin128, _block_diag_ones(W))


N_SLC = 32
PS_PAD = 8


def _online_step(carry, s, valid, v_t):
    m, l, acc = carry
    sm = jnp.where(valid, s, NEG)
    m_new = jnp.maximum(m, jnp.max(sm, axis=0, keepdims=True))
    p = jnp.where(valid, jnp.exp(sm - m_new), 0.0)
    alpha = jnp.exp(m - m_new)
    l = alpha * l + jnp.sum(p, axis=0, keepdims=True)
    acc = alpha * acc + _dot(v_t, p.astype(BF16))
    return m_new, l, acc


def _attn_kernel(q_ref, gt_ref, kc_ref, vct_ref, ks_ref, vst_ref, kw_ref, vwt_ref, o_ref,
                 sel_scr, ps_scr, ot_scr):
    T = Q_TILE
    HG = HEADS_PER_GROUP
    NL = HG * T
    qi = pl.program_id(1)
    tpos = qi * T + lax.broadcasted_iota(jnp.int32, (1, T), 1)
    krow = lax.broadcasted_iota(jnp.int32, (K_TILE, T), 0)
    tile4 = lambda t: jnp.concatenate([t] * HG, axis=1)

    ps_scr[0:PS_PAD, :] = jnp.zeros((PS_PAD, T), F32)
    init = (jnp.full((1, NL), NEG, F32), jnp.zeros((1, NL), F32), jnp.zeros((HEAD_DIM, NL), F32))

    for g in range(N_KV_GROUPS):
        qg = q_ref[HG * g:HG * (g + 1)].reshape(NL, HEAD_DIM)

        sc = _dot_nt(kc_ref[g], qg)
        cvalid = ((krow * CMP_STRIDE + (CMP_BLOCK - 1)) <= tpos) & (krow < N_CHUNK - 1)
        psum = jnp.zeros((N_CHUNK, T), F32)
        probs = []
        for h in range(HG):
            sm = jnp.where(cvalid, sc[:, h * T:(h + 1) * T], NEG)
            e = jnp.where(cvalid, jnp.exp(sm - jnp.max(sm, axis=0, keepdims=True)), 0.0)
            den = jnp.sum(e, axis=0, keepdims=True)
            p = e / jnp.where(den > 0.0, den, 1.0)
            psum = psum + p
            probs.append(p.astype(BF16))
        o_cmp = _dot(vct_ref[g], jnp.concatenate(probs, axis=1))

        ps_scr[PS_PAD:PS_PAD + N_CHUNK, :] = psum
        tap = lambda k: ps_scr[pl.ds(PS_PAD + k, N_SLC, stride=4), :]
        imp = 0.5 * tap(-1) + tap(0) + tap(1) + tap(2) + 0.5 * tap(3)

        jrow = lax.broadcasted_iota(jnp.int32, (N_SLC, T), 0)
        cur = tpos >> 6
        causal_blk = jrow <= cur
        forced = (jrow == 0) | (causal_blk & ((cur - jrow) < N_LOCAL_BLOCKS))
        score = jnp.where(forced, FORCE, jnp.where(causal_blk, imp, NEG))
        rank = jnp.zeros((N_SLC, T), jnp.int32)
        for j2 in range(N_SLC):
            other = score[j2:j2 + 1, :]
            beats = (other > score) | ((other == score) & (j2 < jrow))
            rank = rank + beats.astype(jnp.int32)
        sel_scr[...] = (rank < SLC_TOPK).astype(F32)

        def slc_valid(kt):
            half = K_TILE // 2
            top = jnp.broadcast_to(sel_scr[pl.ds(2 * kt, 1), :], (half, T))
            bot = jnp.broadcast_to(sel_scr[pl.ds(2 * kt + 1, 1), :], (half, T))
            return jnp.concatenate([top, bot], axis=0) > 0.5

        def slc_step(kt, carry):
            return _online_step(carry, _dot_nt(ks_ref[g, kt], qg), tile4(slc_valid(kt)),
                                vst_ref[g, kt])

        carry = lax.fori_loop(0, qi, slc_step, init)
        diag = slc_valid(qi) & ((qi * K_TILE + krow) <= tpos)
        _, l_s, acc_s = _online_step(carry, _dot_nt(ks_ref[g, qi], qg), tile4(diag), vst_ref[g, qi])
        o_slc = acc_s / l_s

        def win_step(kt, carry):
            valid = (kt * K_TILE + krow) > (tpos - WINDOW)
            return _online_step(carry, _dot_nt(kw_ref[g, kt], qg), tile4(valid), vwt_ref[g, kt])

        carry = lax.fori_loop(jnp.maximum(qi - WINDOW // K_TILE, 0), qi, win_step, init)
        diag = (qi * K_TILE + krow) <= tpos
        _, l_w, acc_w = _online_step(carry, _dot_nt(kw_ref[g, qi], qg), tile4(diag), vwt_ref[g, qi])
        o_win = acc_w / l_w

        def gate(branch):
            r0 = branch * N_HEADS + HG * g
            return jnp.concatenate([gt_ref[r0 + h:r0 + h + 1, :] for h in range(HG)], axis=1)

        o_t = gate(0) * o_cmp + gate(1) * o_slc + gate(2) * o_win
        for h in range(HG):
            r0 = (HG * g + h) * HEAD_DIM
            ot_scr[r0:r0 + HEAD_DIM, :] = o_t[:, h * T:(h + 1) * T]

    o_ref[...] = ot_scr[...].T.astype(BF16)


def _nsa_attention(q, gates, k_cmp, v_cmp, ks, vs, kw, vw):
    B, S, NQ = q.shape
    G, T = N_KV_GROUPS, Q_TILE
    NT = S // K_TILE
    qh = q.reshape(B, S, N_HEADS, HEAD_DIM).transpose(0, 2, 1, 3)
    gt = gates[:, :, :3 * N_HEADS].transpose(0, 2, 1)
    kc = k_cmp.reshape(B, G, N_CHUNK, HEAD_DIM).astype(BF16)
    vct = v_cmp.reshape(B, G, N_CHUNK, HEAD_DIM).transpose(0, 1, 3, 2).astype(BF16)
    k_tiles = lambda k: k.reshape(B, NT, K_TILE, G, HEAD_DIM).transpose(0, 3, 1, 2, 4)
    v_tiles = lambda v: v.reshape(B, NT, K_TILE, G, HEAD_DIM).transpose(0, 3, 1, 4, 2)
    per_b = lambda *shape: pl.BlockSpec((None,) + shape, lambda b, i: (b,) + (0,) * len(shape))
    return pl.pallas_call(
        _attn_kernel,
        grid=(B, S // T),
        in_specs=[pl.BlockSpec((None, N_HEADS, T, HEAD_DIM), lambda b, i: (b, 0, i, 0)),
                  pl.BlockSpec((None, 3 * N_HEADS, T), lambda b, i: (b, 0, i)),
                  per_b(G, N_CHUNK, HEAD_DIM), per_b(G, HEAD_DIM, N_CHUNK),
                  per_b(G, NT, K_TILE, HEAD_DIM), per_b(G, NT, HEAD_DIM, K_TILE),
                  per_b(G, NT, K_TILE, HEAD_DIM), per_b(G, NT, HEAD_DIM, K_TILE)],
        out_specs=pl.BlockSpec((None, T, NQ), lambda b, i: (b, i, 0)),
        out_shape=jax.ShapeDtypeStruct((B, S, NQ), BF16),
        scratch_shapes=[pltpu.VMEM((N_SLC, T), F32), pltpu.VMEM((PS_PAD + N_CHUNK, T), F32),
                        pltpu.VMEM((NQ, T), F32)],
        compiler_params=_cparams(2),
        name="nsa_attention",
    )(qh, gt, kc, vct, k_tiles(ks), v_tiles(vs), k_tiles(kw), v_tiles(vw))


def kernel(x, a_norm, a_w_in, a_conv_w, a_conv_b, a_gate_w, a_gate_b, a_lambda, a_w_out,
           kv_norm, kv_w, k_norm, cmp_pos, cmp_w1, cmp_b1, cmp_w2, cmp_b2,
           b_norm, b_w_in, b_gate_b, q_norm, b_w_out, f_norm, f_w_in, f_w_out):
    B, S, D = x.shape
    assert D == D_MODEL and S == N_SLC * SLC_BLOCK and S == N_CHUNK * CMP_STRIDE
    n_a = a_norm.shape[0]
    n_b = b_norm.shape[0]
    h = x
    for i in range(n_a):
        h = _recurrent_block(h, a_norm[i], a_w_in[i], a_conv_w[i], a_conv_b[i], a_gate_w[i],
                             a_gate_b[i], a_lambda[i], a_w_out[i])
        h = _swiglu(h, f_norm[i], f_w_in[i], f_w_out[i])

    cos_t, sin_t = _rope_tables(jnp.arange(S))
    cos128, sin128 = jnp.tile(cos_t, (1, 2)), jnp.tile(sin_t, (1, 2))
    kc, vc, ks, vs, kw, vw = _kv_proj(h, kv_norm, kv_w, k_norm, cos128, sin128)
    cmp = _compress(kc, vc, cmp_pos, cmp_w1, cmp_b1, cmp_w2, cmp_b2, k_norm[0])
    for j in range(n_b):
        q, gates = _q_proj(h, b_norm[j], b_w_in[j], b_gate_b[j], q_norm[j], cos128, sin128)
        o = _nsa_attention(q, gates, cmp[0], cmp[1], ks, vs, kw, vw)
        layer = n_a + j
        h = _swiglu(h, f_norm[layer], f_w_in[layer], f_w_out[layer], attn=o, w_o=b_w_out[j])
    return h
```

```python
import functools

import jax
import jax.numpy as jnp
from jax import lax
from jax.experimental import pallas as pl
from jax.experimental.pallas import tpu as pltpu

F32 = jnp.float32
BF16 = jnp.bfloat16

D_MODEL = 1024
LRU_WIDTH = D_MODEL
LRU_HEADS = 8
LRU_BLOCK = LRU_WIDTH // LRU_HEADS
CONV_WIDTH = 4
LRU_C = 8.0
HEAD_DIM = 64
N_HEADS = D_MODEL // HEAD_DIM
N_KV_GROUPS = 4
HEADS_PER_GROUP = N_HEADS // N_KV_GROUPS
CMP_BLOCK = 32
CMP_STRIDE = 16
CMP_HIDDEN = 256
SLC_BLOCK = 64
SLC_TOPK = 16
N_LOCAL_BLOCKS = 2
WINDOW = 512
ROPE_THETA = 10000.0
FFN_HIDDEN = 2816
EPS = 1e-6
NEG = -1e30
FORCE = 1e30

LANES = 128
KV_WIDTH = N_KV_GROUPS * HEAD_DIM
Q_TILE = 128
K_TILE = 128
VMEM_LIMIT = 56 * 1024 * 1024


def _cparams(n_axes):
    return pltpu.CompilerParams(dimension_semantics=("arbitrary",) * n_axes,
                                vmem_limit_bytes=VMEM_LIMIT)


def _const_spec(shape):
    nd = len(shape)
    return pl.BlockSpec(shape, lambda *_: (0,) * nd, pipeline_mode=pl.Buffered(1))


def _rms(x, g):
    ms = jnp.mean(x * x, axis=-1, keepdims=True)
    return x * lax.rsqrt(ms + EPS) * g


def _sigmoid(x):
    return 1.0 / (1.0 + jnp.exp(-x))


def _gelu_tanh(x):
    c = 0.7978845608028654
    return x * (0.5 * (1.0 + jnp.tanh(c * (x + 0.044715 * (x * x * x)))))


def _dot(a, b):
    return jnp.dot(a, b, preferred_element_type=F32)


def _dot_nt(a, b):
    return lax.dot_general(a, b, (((1,), (1,)), ((), ())), preferred_element_type=F32)


def _head_mean_sq(x, ones_bd):
    sq = x * x
    hi = sq.astype(BF16)
    lo = (sq - hi.astype(F32)).astype(BF16)
    return (_dot(hi, ones_bd) + _dot(lo, ones_bd)) * (1.0 / HEAD_DIM)


def _rope_flat(x, cos_t, sin_t):
    width = x.shape[-1]
    lane = lax.broadcasted_iota(jnp.int32, x.shape, 1)
    upper = (lane & (HEAD_DIM // 2)) != 0
    partner = jnp.where(upper, pltpu.roll(x, HEAD_DIM // 2, 1),
                        pltpu.roll(x, width - HEAD_DIM // 2, 1))
    return x * cos_t + partner * sin_t


def _tile_lanes(t, width):
    reps = width // t.shape[-1]
    return t if reps == 1 else jnp.concatenate([t] * reps, axis=1)


def _scan_rows(a, b):
    n = a.shape[0]
    row = lax.broadcasted_iota(jnp.int32, a.shape, 0)
    d = 1
    while d < n:
        valid = row >= d
        b = jnp.where(valid, a * pltpu.roll(b, d, 0) + b, b)
        a = jnp.where(valid, a * pltpu.roll(a, d, 0), a)
        d *= 2
    return a, b


def _rec_kernel(x_ref, g_ref, win_ref, cw_ref, cb_ref, wg_ref, gb_ref, lam_ref, wout_ref, o_ref,
                xbuf, hcar, yh):
    ts = x_ref.shape[0]
    R = LRU_WIDTH

    @pl.when(pl.program_id(1) == 0)
    def _():
        xbuf[0:8, :] = jnp.zeros((8, R), F32)
        hcar[...] = jnp.zeros_like(hcar)

    x = x_ref[...]
    u = _rms(x, g_ref[...]).astype(BF16)
    z = _dot(u, win_ref[...])
    xbuf[8:8 + ts, :] = z[:, R:]
    for hh in range(LRU_HEADS):
        cs = slice(hh * LRU_BLOCK, (hh + 1) * LRU_BLOCK)
        cw = cw_ref[:, cs]
        xr = cb_ref[:, cs] + xbuf[5:5 + ts, cs] * cw[0:1]
        xr = xr + xbuf[6:6 + ts, cs] * cw[1:2]
        xr = xr + xbuf[7:7 + ts, cs] * cw[2:3]
        xr = xr + xbuf[8:8 + ts, cs] * cw[3:4]
        gates = _dot(xr.astype(BF16), wg_ref[hh])
        gb = gb_ref[:, cs]
        r = _sigmoid(gates[:, :LRU_BLOCK] + gb[0:1])
        i = _sigmoid(gates[:, LRU_BLOCK:] + gb[1:2])
        lam = lam_ref[:, cs]
        softplus_neg = jnp.maximum(-lam, 0.0) + jnp.log1p(jnp.exp(-jnp.abs(lam)))
        log_a = (-LRU_C * softplus_neg) * r
        a = jnp.exp(log_a)
        bterm = jnp.sqrt(1.0 - a * a) * (i * xr)
        acum, hs = _scan_rows(a, bterm)
        hs = acum * hcar[:, cs] + hs
        hcar[:, cs] = hs[ts - 1:ts, :]
        yh[:, cs] = (_gelu_tanh(z[:, cs]) * hs).astype(BF16)
    xbuf[0:8, :] = xbuf[ts:ts + 8, :]
    o_ref[...] = x + _dot(yh[...], wout_ref[...])


def _recurrent_block(h, norm_g, w_in, conv_w, conv_b, gate_w, gate_b, lam, w_out, ts=256):
    B, S, D = h.shape
    R = LRU_WIDTH
    wg = jnp.concatenate([gate_w[0], gate_w[1]], axis=-1).astype(BF16)
    return pl.pallas_call(
        _rec_kernel,
        grid=(B, S // ts),
        in_specs=[
            pl.BlockSpec((None, ts, D), lambda b, s: (b, s, 0)),
            _const_spec((1, D)),
            _const_spec((D, 2 * R)),
            _const_spec((CONV_WIDTH, R)),
            _const_spec((1, R)),
            _const_spec((LRU_HEADS, LRU_BLOCK, 2 * LRU_BLOCK)),
            _const_spec((2, R)),
            _const_spec((1, R)),
            _const_spec((R, D)),
        ],
        out_specs=pl.BlockSpec((None, ts, D), lambda b, s: (b, s, 0)),
        out_shape=jax.ShapeDtypeStruct((B, S, D), F32),
        scratch_shapes=[pltpu.VMEM((ts + 8, R), F32), pltpu.VMEM((1, R), F32),
                        pltpu.VMEM((ts, R), BF16)],
        compiler_params=_cparams(2),
        name="rglru_block",
    )(h, norm_g.reshape(1, D), w_in.astype(BF16), conv_w, conv_b.reshape(1, R), wg, gate_b,
      lam.reshape(1, R), w_out.astype(BF16))


FFN_CHUNK = FFN_HIDDEN // 2


def _ffn_body(x, g_ref, win_ref, wout_ref, o_ref):
    u = _rms(x, g_ref[...]).astype(BF16)
    acc = x
    for c in range(FFN_HIDDEN // FFN_CHUNK):
        lo = c * FFN_CHUNK
        gate = _dot(u, win_ref[:, lo:lo + FFN_CHUNK])
        up = _dot(u, win_ref[:, FFN_HIDDEN + lo:FFN_HIDDEN + lo + FFN_CHUNK])
        act = ((gate * _sigmoid(gate)) * up).astype(BF16)
        acc = acc + _dot(act, wout_ref[lo:lo + FFN_CHUNK, :])
    o_ref[...] = acc


def _ffn_kernel(x_ref, g_ref, win_ref, wout_ref, o_ref):
    _ffn_body(x_ref[...], g_ref, win_ref, wout_ref, o_ref)


def _proj_ffn_kernel(x_ref, a_ref, wo_ref, g_ref, win_ref, wout_ref, o_ref):
    _ffn_body(x_ref[...] + _dot(a_ref[...], wo_ref[...]), g_ref, win_ref, wout_ref, o_ref)


def _swiglu(h, norm_g, w_in, w_out, attn=None, w_o=None, tm=512):
    B, S, D = h.shape
    M = B * S
    row_spec = pl.BlockSpec((tm, D), lambda i: (i, 0))
    w_specs = [_const_spec((1, D)), _const_spec((D, 2 * FFN_HIDDEN)), _const_spec((FFN_HIDDEN, D))]
    w_args = (norm_g.reshape(1, D), w_in.astype(BF16), w_out.astype(BF16))
    if attn is None:
        kern, specs, args = _ffn_kernel, [row_spec] + w_specs, (h.reshape(M, D),) + w_args
    else:
        kern = _proj_ffn_kernel
        specs = [row_spec, row_spec, _const_spec((D, D))] + w_specs
        args = (h.reshape(M, D), attn.reshape(M, D), w_o.astype(BF16)) + w_args
    out = pl.pallas_call(
        kern,
        grid=(M // tm,),
        in_specs=specs,
        out_specs=row_spec,
        out_shape=jax.ShapeDtypeStruct((M, D), F32),
        compiler_params=_cparams(1),
        name="swiglu_ffn",
    )(*args)
    return out.reshape(B, S, D)


def _kv_kernel(x_ref, g_ref, w_ref, kn_ref, cos_ref, sin_ref, ones_ref,
               kc_ref, vc_ref, ks_ref, vs_ref, kw_ref, vw_ref):
    W = KV_WIDTH
    u = _rms(x_ref[...], g_ref[...]).astype(BF16)
    kv = _dot(u, w_ref[...])
    cos_t = _tile_lanes(cos_ref[...], W)
    sin_t = _tile_lanes(sin_ref[...], W)
    ones_bd = ones_ref[...]

    def key(j, gain):
        k = kv[:, j * W:(j + 1) * W]
        k = k * lax.rsqrt(_head_mean_sq(k, ones_bd) + EPS) * gain
        return _rope_flat(k, cos_t, sin_t).astype(BF16)

    kc_ref[...] = kv[:, 0 * W:1 * W]
    vc_ref[...] = kv[:, 1 * W:2 * W]
    ks_ref[...] = key(2, kn_ref[1:2, :])
    vs_ref[...] = kv[:, 3 * W:4 * W].astype(BF16)
    kw_ref[...] = key(4, kn_ref[2:3, :])
    vw_ref[...] = kv[:, 5 * W:6 * W].astype(BF16)


def _rope_tables(pos):
    half = HEAD_DIM // 2
    freqs = jnp.power(ROPE_THETA, -jnp.arange(half, dtype=F32) / half)
    ang = pos.astype(F32)[:, None] * freqs[None, :]
    cos, sin = jnp.cos(ang), jnp.sin(ang)
    cos_t = jnp.concatenate([cos, cos], axis=-1)
    sin_t = jnp.concatenate([-sin, sin], axis=-1)
    return cos_t, sin_t


def _block_diag_ones(width):
    seg = jnp.arange(width) // HEAD_DIM
    return (seg[:, None] == seg[None, :]).astype(BF16)


def _kv_proj(h, kv_norm, kv_w, k_norm, cos128, sin128, ts=512):
    B, S, D = h.shape
    W = KV_WIDTH
    kn = jnp.tile(k_norm, (1, N_KV_GROUPS))
    row = lambda width: pl.BlockSpec((None, ts, width), lambda b, s: (b, s, 0))
    tab = pl.BlockSpec((ts, LANES), lambda b, s: (s, 0))
    shapes = [jax.ShapeDtypeStruct((B, S, W), dt) for dt in (F32, F32, BF16, BF16, BF16, BF16)]
    return pl.pallas_call(
        _kv_kernel,
        grid=(B, S // ts),
        in_specs=[row(D), _const_spec((1, D)), _const_spec((D, 6 * W)), _const_spec((3, W)),
                  tab, tab, _const_spec((W, W))],
        out_specs=[row(W)] * 6,
        out_shape=shapes,
        compiler_params=_cparams(2),
        name="shared_kv_proj",
    )(h, kv_norm.reshape(1, D), kv_w.astype(BF16), kn, cos128, sin128, _block_diag_ones(W))


N_CHUNK = 128
CHUNK_W = CMP_STRIDE * HEAD_DIM


def _cmp_kernel(x_ref, pos_ref, w1_ref, b1_ref, w2_ref, b2_ref, kn_ref, cos_ref, sin_ref, o_ref):
    rows = x_ref.shape[0]
    x = x_ref[...]
    ya = _dot((x + pos_ref[0:1, :]).astype(BF16), w1_ref[0:CHUNK_W, :])
    yb = _dot((x + pos_ref[1:2, :]).astype(BF16), w1_ref[CHUNK_W:2 * CHUNK_W, :])
    hid = _gelu_tanh(ya + pltpu.roll(yb, rows - 1, 0) + b1_ref[...])
    out = _dot(hid.astype(BF16), w2_ref[...]) + b2_ref[...]

    @pl.when(pl.program_id(0) == 0)
    def _():
        k = _rms(out, kn_ref[...])
        half = HEAD_DIM // 2
        partner = jnp.concatenate([k[:, half:], k[:, :half]], axis=1)
        o_ref[...] = k * cos_ref[...] + partner * sin_ref[...]

    @pl.when(pl.program_id(0) == 1)
    def _():
        o_ref[...] = out


def _compress(kc, vc, cmp_pos, cmp_w1, cmp_b1, cmp_w2, cmp_b2, k_norm0):
    B, S, _ = kc.shape
    G = N_KV_GROUPS
    rows = G * N_CHUNK

    def chunked(t):
        t = t.reshape(B, N_CHUNK, CMP_STRIDE, G, HEAD_DIM).transpose(0, 3, 1, 2, 4)
        return t.reshape(B, rows, CHUNK_W)

    x = jnp.stack([chunked(kc), chunked(vc)])
    pos = cmp_pos.reshape(2, 2, CHUNK_W)
    cmp_last = jnp.arange(N_CHUNK) * CMP_STRIDE + CMP_BLOCK - 1
    cos_t, sin_t = _rope_tables(cmp_last)
    cos_t, sin_t = jnp.tile(cos_t, (G, 1)), jnp.tile(sin_t, (G, 1))
    per_kv = lambda *shape: pl.BlockSpec((None,) + shape, lambda k, b: (k,) + (0,) * len(shape))
    return pl.pallas_call(
        _cmp_kernel,
        grid=(2, B),
        in_specs=[pl.BlockSpec((None, None, rows, CHUNK_W), lambda k, b: (k, b, 0, 0)),
                  per_kv(2, CHUNK_W), per_kv(2 * CHUNK_W, CMP_HIDDEN), per_kv(1, CMP_HIDDEN),
                  per_kv(CMP_HIDDEN, HEAD_DIM), per_kv(1, HEAD_DIM),
                  _const_spec((1, HEAD_DIM)), _const_spec((rows, HEAD_DIM)),
                  _const_spec((rows, HEAD_DIM))],
        out_specs=pl.BlockSpec((None, None, rows, HEAD_DIM), lambda k, b: (k, b, 0, 0)),
        out_shape=jax.ShapeDtypeStruct((2, B, rows, HEAD_DIM), F32),
        compiler_params=_cparams(2),
        name="kv_compress",
    )(x, pos, cmp_w1.astype(BF16), cmp_b1.reshape(2, 1, CMP_HIDDEN), cmp_w2.astype(BF16),
      cmp_b2.reshape(2, 1, HEAD_DIM), k_norm0.reshape(1, HEAD_DIM), cos_t, sin_t)


GATE_PAD = LANES
LOG2_E = 1.4426950408889634
Q_SCALE = HEAD_DIM ** -0.5 * LOG2_E


def _q_kernel(x_ref, g_ref, w_ref, gb_ref, qn_ref, cos_ref, sin_ref, ones_ref, q_ref, gate_ref):
    NQ = N_HEADS * HEAD_DIM
    u = _rms(x_ref[...], g_ref[...]).astype(BF16)
    z = _dot(u, w_ref[...])
    gate_ref[...] = _sigmoid(z[:, NQ:] + gb_ref[...])
    ones_bd = ones_ref[...]
    W = ones_bd.shape[0]
    cos_t = _tile_lanes(cos_ref[...], W)
    sin_t = _tile_lanes(sin_ref[...], W)
    for c in range(NQ // W):
        q = z[:, c * W:(c + 1) * W]
        q = q * lax.rsqrt(_head_mean_sq(q, ones_bd) + EPS) * qn_ref[...]
        q_ref[:, c * W:(c + 1) * W] = (_rope_flat(q, cos_t, sin_t) * Q_SCALE).astype(BF16)


def _q_proj(h, norm_g, w_in, gate_b, q_norm_g, cos128, sin128, ts=512):
    B, S, D = h.shape
    NQ = N_HEADS * HEAD_DIM
    n_gate = 3 * N_HEADS
    w = jnp.pad(w_in, ((0, 0), (0, GATE_PAD - n_gate))).astype(BF16)
    gb = jnp.pad(gate_b, (0, GATE_PAD - n_gate)).reshape(1, GATE_PAD)
    W = KV_WIDTH
    qn = jnp.tile(q_norm_g, W // HEAD_DIM).reshape(1, W)
    row = lambda width: pl.BlockSpec((None, ts, width), lambda b, s: (b, s, 0))
    tab = pl.BlockSpec((ts, LANES), lambda b, s: (s, 0))
    return pl.pallas_call(
        _q_kernel,
        grid=(B, S // ts),
        in_specs=[row(D), _const_spec((1, D)), _const_spec((D, NQ + GATE_PAD)),
                  _const_spec((1, GATE_PAD)), _const_spec((1, W)), tab, tab, _const_spec((W, W))],
        out_specs=[row(NQ), row(GATE_PAD)],
        out_shape=[jax.ShapeDtypeStruct((B, S, NQ), BF16),
                   jax.ShapeDtypeStruct((B, S, GATE_PAD), F32)],
        compiler_params=_cparams(2),
        name="nsa_q_proj",
    )(h, norm_g.reshape(1, D), w, gb, qn, cos128, sin128, _block_diag_ones(W))


N_SLC = 32
PS_PAD = 8


def _flash_step(g, s, valid, v_t, m_scr, l_scr, acc_scr):
    T = Q_TILE
    m_old = m_scr[g]
    m_new, p_all = [], []
    for h in range(HEADS_PER_GROUP):
        hs = slice(h * T, (h + 1) * T)
        sm = jnp.where(valid, s[:, hs], NEG)
        m_h = jnp.maximum(m_old[:, hs], jnp.max(sm, axis=0, keepdims=True))
        m_new.append(m_h)
        p_all.append(jnp.exp2(sm - m_h))
    m_new = jnp.concatenate(m_new, axis=1)
    p = jnp.concatenate(p_all, axis=1)
    alpha = jnp.exp2(m_old - m_new)
    m_scr[g] = m_new
    l_scr[g] = alpha * l_scr[g] + jnp.sum(p, axis=0, keepdims=True)
    acc_scr[g] = alpha * acc_scr[g] + _dot(v_t, p.astype(BF16))


def _attn_kernel(q_ref, gt_ref, kc_ref, vct_ref, ks_ref, vst_ref, kw_ref, vwt_ref, o_ref,
                 sel_scr, ps_scr, m_scr, l_scr, acc_scr, ot_scr):
    T = Q_TILE
    G = N_KV_GROUPS
    HG = HEADS_PER_GROUP
    NL = HG * T
    qi = pl.program_id(1)
    tpos = qi * T + lax.broadcasted_iota(jnp.int32, (1, T), 1)
    krow = lax.broadcasted_iota(jnp.int32, (K_TILE, T), 0)
    q_group = lambda g: q_ref[HG * g:HG * (g + 1)].reshape(NL, HEAD_DIM)

    def gate(branch, g):
        r0 = branch * N_HEADS + HG * g
        return jnp.concatenate([gt_ref[r0 + h:r0 + h + 1, :] for h in range(HG)], axis=1)

    def emit(g, o_t, first):
        for h in range(HG):
            rows = slice((HG * g + h) * HEAD_DIM, (HG * g + h + 1) * HEAD_DIM)
            piece = o_t[:, h * T:(h + 1) * T]
            ot_scr[rows, :] = piece if first else ot_scr[rows, :] + piece

    def reset_state():
        for g in range(G):
            m_scr[g] = jnp.full((1, NL), NEG, F32)
            l_scr[g] = jnp.zeros((1, NL), F32)
            acc_scr[g] = jnp.zeros((HEAD_DIM, NL), F32)

    def emit_state(branch):
        for g in range(G):
            emit(g, acc_scr[g] * (gate(branch, g) / l_scr[g]), first=False)

    for g in range(G):
        sc = _dot_nt(kc_ref[g], q_group(g))
        cvalid = ((krow * CMP_STRIDE + (CMP_BLOCK - 1)) <= tpos) & (krow < N_CHUNK - 1)
        psum = jnp.zeros((N_CHUNK, T), F32)
        probs = []
        for h in range(HG):
            sm = jnp.where(cvalid, sc[:, h * T:(h + 1) * T], NEG)
            e = jnp.where(cvalid, jnp.exp2(sm - jnp.max(sm, axis=0, keepdims=True)), 0.0)
            den = jnp.sum(e, axis=0, keepdims=True)
            p = e / jnp.where(den > 0.0, den, 1.0)
            psum = psum + p
            probs.append(p.astype(BF16))
        emit(g, gate(0, g) * _dot(vct_ref[g], jnp.concatenate(probs, axis=1)), first=True)

        ps_scr[g, 0:PS_PAD, :] = jnp.zeros((PS_PAD, T), F32)
        ps_scr[g, PS_PAD:PS_PAD + N_CHUNK, :] = psum
        tap = lambda k: ps_scr[g, pl.ds(PS_PAD + k, N_SLC, stride=4), :]
        imp = 0.5 * tap(-1) + tap(0) + tap(1) + tap(2) + 0.5 * tap(3)

        jrow = lax.broadcasted_iota(jnp.int32, (N_SLC, T), 0)
        cur = tpos >> 6
        causal_blk = jrow <= cur
        forced = (jrow == 0) | (causal_blk & ((cur - jrow) < N_LOCAL_BLOCKS))
        score = jnp.where(forced, FORCE, jnp.where(causal_blk, imp, NEG))
        rank = jnp.zeros((N_SLC, T), jnp.int32)
        for j2 in range(N_SLC):
            other = score[j2:j2 + 1, :]
            beats = (other > score) | ((other == score) & (j2 < jrow))
            rank = rank + beats.astype(jnp.int32)
        sel_scr[g] = (rank < SLC_TOPK).astype(F32)

    def slc_valid(g, kt):
        half = K_TILE // 2
        top = jnp.broadcast_to(sel_scr[g, pl.ds(2 * kt, 1), :], (half, T))
        bot = jnp.broadcast_to(sel_scr[g, pl.ds(2 * kt + 1, 1), :], (half, T))
        return jnp.concatenate([top, bot], axis=0) > 0.5

    def slc_tile(kt, causal):
        scores = [_dot_nt(ks_ref[g, kt], q_group(g)) for g in range(G)]
        for g in range(G):
            valid = slc_valid(g, kt)
            if causal:
                valid = valid & ((kt * K_TILE + krow) <= tpos)
            _flash_step(g, scores[g], valid, vst_ref[g, kt], m_scr, l_scr, acc_scr)

    def slc_body(kt, c):
        slc_tile(kt, causal=False)
        return c

    reset_state()
    lax.fori_loop(0, qi, slc_body, 0)
    slc_tile(qi, causal=True)
    emit_state(1)

    def win_tile(kt, causal):
        kpos = kt * K_TILE + krow
        valid = (kpos <= tpos) if causal else (kpos > (tpos - WINDOW))
        scores = [_dot_nt(kw_ref[g, kt], q_group(g)) for g in range(G)]
        for g in range(G):
            _flash_step(g, scores[g], valid, vwt_ref[g, kt], m_scr, l_scr, acc_scr)

    def win_body(kt, c):
        win_tile(kt, causal=False)
        return c

    reset_state()
    win_tile(qi, causal=True)
    lax.fori_loop(jnp.maximum(qi - WINDOW // K_TILE, 0), qi, win_body, 0)
    emit_state(2)

    o_ref[...] = ot_scr[...].T.astype(BF16)


def _nsa_attention(q, gates, k_cmp, v_cmp, ks, vs, kw, vw):
    B, S, NQ = q.shape
    G, T = N_KV_GROUPS, Q_TILE
    NT = S // K_TILE
    qh = q.reshape(B, S, N_HEADS, HEAD_DIM).transpose(0, 2, 1, 3)
    gt = gates[:, :, :3 * N_HEADS].transpose(0, 2, 1)
    kc = k_cmp.reshape(B, G, N_CHUNK, HEAD_DIM).astype(BF16)
    vct = v_cmp.reshape(B, G, N_CHUNK, HEAD_DIM).transpose(0, 1, 3, 2).astype(BF16)
    k_tiles = lambda k: k.reshape(B, NT, K_TILE, G, HEAD_DIM).transpose(0, 3, 1, 2, 4)
    v_tiles = lambda v: v.reshape(B, NT, K_TILE, G, HEAD_DIM).transpose(0, 3, 1, 4, 2)
    per_b = lambda *shape: pl.BlockSpec((None,) + shape, lambda b, i: (b,) + (0,) * len(shape))
    return pl.pallas_call(
        _attn_kernel,
        grid=(B, S // T),
        in_specs=[pl.BlockSpec((None, N_HEADS, T, HEAD_DIM), lambda b, i: (b, 0, i, 0)),
                  pl.BlockSpec((None, 3 * N_HEADS, T), lambda b, i: (b, 0, i)),
                  per_b(G, N_CHUNK, HEAD_DIM), per_b(G, HEAD_DIM, N_CHUNK),
                  per_b(G, NT, K_TILE, HEAD_DIM), per_b(G, NT, HEAD_DIM, K_TILE),
                  per_b(G, NT, K_TILE, HEAD_DIM), per_b(G, NT, HEAD_DIM, K_TILE)],
        out_specs=pl.BlockSpec((None, T, NQ), lambda b, i: (b, i, 0)),
        out_shape=jax.ShapeDtypeStruct((B, S, NQ), BF16),
        scratch_shapes=[pltpu.VMEM((G, N_SLC, T), F32), pltpu.VMEM((G, PS_PAD + N_CHUNK, T), F32),
                        pltpu.VMEM((G, 1, HEADS_PER_GROUP * T), F32),
                        pltpu.VMEM((G, 1, HEADS_PER_GROUP * T), F32),
                        pltpu.VMEM((G, HEAD_DIM, HEADS_PER_GROUP * T), F32),
                        pltpu.VMEM((NQ, T), F32)],
        compiler_params=_cparams(2),
        name="nsa_attention",
    )(qh, gt, kc, vct, k_tiles(ks), v_tiles(vs), k_tiles(kw), v_tiles(vw))


def kernel(x, a_norm, a_w_in, a_conv_w, a_conv_b, a_gate_w, a_gate_b, a_lambda, a_w_out,
           kv_norm, kv_w, k_norm, cmp_pos, cmp_w1, cmp_b1, cmp_w2, cmp_b2,
           b_norm, b_w_in, b_gate_b, q_norm, b_w_out, f_norm, f_w_in, f_w_out):
    B, S, D = x.shape
    assert D == D_MODEL and S == N_SLC * SLC_BLOCK and S == N_CHUNK * CMP_STRIDE
    n_a = a_norm.shape[0]
    n_b = b_norm.shape[0]
    h = x
    for i in range(n_a):
        h = _recurrent_block(h, a_norm[i], a_w_in[i], a_conv_w[i], a_conv_b[i], a_gate_w[i],
                             a_gate_b[i], a_lambda[i], a_w_out[i])
        h = _swiglu(h, f_norm[i], f_w_in[i], f_w_out[i])

    cos_t, sin_t = _rope_tables(jnp.arange(S))
    cos128, sin128 = jnp.tile(cos_t, (1, 2)), jnp.tile(sin_t, (1, 2))
    kc, vc, ks, vs, kw, vw = _kv_proj(h, kv_norm, kv_w, k_norm, cos128, sin128)
    cmp = _compress(kc, vc, cmp_pos, cmp_w1, cmp_b1, cmp_w2, cmp_b2, k_norm[0])
    for j in range(n_b):
        q, gates = _q_proj(h, b_norm[j], b_w_in[j], b_gate_b[j], q_norm[j], cos128, sin128)
        o = _nsa_attention(q, gates, cmp[0], cmp[1], ks, vs, kw, vw)
        layer = n_a + j
        h = _swiglu(h, f_norm[layer], f_w_in[layer], f_w_out[layer], attn=o, w_o=b_w_out[j])
    return h
```

```python
import functools

import jax
import jax.numpy as jnp
from jax import lax
from jax.experimental import pallas as pl
from jax.experimental.pallas import tpu as pltpu

F32 = jnp.float32
BF16 = jnp.bfloat16

D_MODEL = 1024
LRU_WIDTH = D_MODEL
LRU_HEADS = 8
LRU_BLOCK = LRU_WIDTH // LRU_HEADS
CONV_WIDTH = 4
LRU_C = 8.0
HEAD_DIM = 64
N_HEADS = D_MODEL // HEAD_DIM
N_KV_GROUPS = 4
HEADS_PER_GROUP = N_HEADS // N_KV_GROUPS
CMP_BLOCK = 32
CMP_STRIDE = 16
CMP_HIDDEN = 256
SLC_BLOCK = 64
SLC_TOPK = 16
N_LOCAL_BLOCKS = 2
WINDOW = 512
ROPE_THETA = 10000.0
FFN_HIDDEN = 2816
EPS = 1e-6
NEG = -1e30
FORCE = 1e30

LANES = 128
KV_WIDTH = N_KV_GROUPS * HEAD_DIM
Q_TILE = 128
K_TILE = 128
VT_ROWS = HEAD_DIM + 16
VMEM_LIMIT = 56 * 1024 * 1024


def _cparams(n_axes):
    return pltpu.CompilerParams(dimension_semantics=("arbitrary",) * n_axes,
                                vmem_limit_bytes=VMEM_LIMIT)


def _const_spec(shape):
    nd = len(shape)
    return pl.BlockSpec(shape, lambda *_: (0,) * nd, pipeline_mode=pl.Buffered(1))


def _rms(x, g):
    ms = jnp.mean(x * x, axis=-1, keepdims=True)
    return x * lax.rsqrt(ms + EPS) * g


def _sigmoid(x):
    return 1.0 / (1.0 + jnp.exp(-x))


def _gelu_tanh(x):
    c = 0.7978845608028654
    return x * (0.5 * (1.0 + jnp.tanh(c * (x + 0.044715 * (x * x * x)))))


def _dot(a, b):
    return jnp.dot(a, b, preferred_element_type=F32)


def _dot_nt(a, b):
    return lax.dot_general(a, b, (((1,), (1,)), ((), ())), preferred_element_type=F32)


def _head_mean_sq(x, ones_bd):
    sq = x * x
    hi = sq.astype(BF16)
    lo = (sq - hi.astype(F32)).astype(BF16)
    return (_dot(hi, ones_bd) + _dot(lo, ones_bd)) * (1.0 / HEAD_DIM)


def _rope_flat(x, cos_t, sin_t):
    width = x.shape[-1]
    lane = lax.broadcasted_iota(jnp.int32, x.shape, 1)
    upper = (lane & (HEAD_DIM // 2)) != 0
    partner = jnp.where(upper, pltpu.roll(x, HEAD_DIM // 2, 1),
                        pltpu.roll(x, width - HEAD_DIM // 2, 1))
    return x * cos_t + partner * sin_t


def _tile_lanes(t, width):
    reps = width // t.shape[-1]
    return t if reps == 1 else jnp.concatenate([t] * reps, axis=1)


SUBLANES = 8


def _scan_rows(a, b, h_in):
    n = a.shape[0]
    a3 = a.reshape(n // SUBLANES, SUBLANES, LANES)
    b3 = b.reshape(n // SUBLANES, SUBLANES, LANES)
    sub = lax.broadcasted_iota(jnp.int32, a3.shape, 1)
    d = 1
    while d < SUBLANES:
        valid = sub >= d
        b3 = jnp.where(valid, a3 * pltpu.roll(b3, d, 1) + b3, b3)
        a3 = jnp.where(valid, a3 * pltpu.roll(a3, d, 1), a3)
        d *= 2
    out = []
    for g in range(n // SUBLANES):
        hg = a3[g] * h_in + b3[g]
        out.append(hg)
        h_in = hg[SUBLANES - 1:SUBLANES, :]
    return jnp.concatenate(out, axis=0)


def _rec_kernel(x_ref, g_ref, win_ref, cw_ref, cb_ref, wg_ref, gb_ref, lam_ref, wout_ref, o_ref,
                xbuf, hcar, yh):
    ts = x_ref.shape[0]
    R = LRU_WIDTH

    @pl.when(pl.program_id(1) == 0)
    def _():
        xbuf[0:8, :] = jnp.zeros((8, R), F32)
        hcar[...] = jnp.zeros_like(hcar)

    x = x_ref[...]
    u = _rms(x, g_ref[...]).astype(BF16)
    z = _dot(u, win_ref[...])
    xbuf[8:8 + ts, :] = z[:, R:]
    for hh in range(LRU_HEADS):
        cs = slice(hh * LRU_BLOCK, (hh + 1) * LRU_BLOCK)
        cw = cw_ref[:, cs]
        xr = cb_ref[:, cs] + xbuf[5:5 + ts, cs] * cw[0:1]
        xr = xr + xbuf[6:6 + ts, cs] * cw[1:2]
        xr = xr + xbuf[7:7 + ts, cs] * cw[2:3]
        xr = xr + xbuf[8:8 + ts, cs] * cw[3:4]
        gates = _dot(xr.astype(BF16), wg_ref[hh])
        gb = gb_ref[:, cs]
        r = _sigmoid(gates[:, :LRU_BLOCK] + gb[0:1])
        i = _sigmoid(gates[:, LRU_BLOCK:] + gb[1:2])
        lam = lam_ref[:, cs]
        softplus_neg = jnp.maximum(-lam, 0.0) + jnp.log1p(jnp.exp(-jnp.abs(lam)))
        log_a = (-LRU_C * softplus_neg) * r
        a = jnp.exp(log_a)
        bterm = jnp.sqrt(1.0 - a * a) * (i * xr)
        hs = _scan_rows(a, bterm, hcar[:, cs])
        hcar[:, cs] = hs[ts - 1:ts, :]
        yh[:, cs] = (_gelu_tanh(z[:, cs]) * hs).astype(BF16)
    xbuf[0:8, :] = xbuf[ts:ts + 8, :]
    o_ref[...] = x + _dot(yh[...], wout_ref[...])


def _recurrent_block(h, norm_g, w_in, conv_w, conv_b, gate_w, gate_b, lam, w_out, ts=256):
    B, S, D = h.shape
    R = LRU_WIDTH
    wg = jnp.concatenate([gate_w[0], gate_w[1]], axis=-1).astype(BF16)
    return pl.pallas_call(
        _rec_kernel,
        grid=(B, S // ts),
        in_specs=[
            pl.BlockSpec((None, ts, D), lambda b, s: (b, s, 0)),
            _const_spec((1, D)),
            _const_spec((D, 2 * R)),
            _const_spec((CONV_WIDTH, R)),
            _const_spec((1, R)),
            _const_spec((LRU_HEADS, LRU_BLOCK, 2 * LRU_BLOCK)),
            _const_spec((2, R)),
            _const_spec((1, R)),
            _const_spec((R, D)),
        ],
        out_specs=pl.BlockSpec((None, ts, D), lambda b, s: (b, s, 0)),
        out_shape=jax.ShapeDtypeStruct((B, S, D), F32),
        scratch_shapes=[pltpu.VMEM((ts + 8, R), F32), pltpu.VMEM((1, R), F32),
                        pltpu.VMEM((ts, R), BF16)],
        compiler_params=_cparams(2),
        name="rglru_block",
    )(h, norm_g.reshape(1, D), w_in.astype(BF16), conv_w, conv_b.reshape(1, R), wg, gate_b,
      lam.reshape(1, R), w_out.astype(BF16))


FFN_CHUNK = FFN_HIDDEN // 2


def _ffn_body(x, g_ref, win_ref, wout_ref, o_ref):
    u = _rms(x, g_ref[...]).astype(BF16)
    acc = x
    for c in range(FFN_HIDDEN // FFN_CHUNK):
        lo = c * FFN_CHUNK
        gate = _dot(u, win_ref[:, lo:lo + FFN_CHUNK])
        up = _dot(u, win_ref[:, FFN_HIDDEN + lo:FFN_HIDDEN + lo + FFN_CHUNK])
        act = ((gate * _sigmoid(gate)) * up).astype(BF16)
        acc = acc + _dot(act, wout_ref[lo:lo + FFN_CHUNK, :])
    o_ref[...] = acc


def _ffn_kernel(x_ref, g_ref, win_ref, wout_ref, o_ref):
    _ffn_body(x_ref[...], g_ref, win_ref, wout_ref, o_ref)


def _proj_ffn_kernel(x_ref, a_ref, wo_ref, g_ref, win_ref, wout_ref, o_ref):
    _ffn_body(x_ref[...] + _dot(a_ref[...], wo_ref[...]), g_ref, win_ref, wout_ref, o_ref)


def _swiglu(h, norm_g, w_in, w_out, attn=None, w_o=None, tm=512):
    B, S, D = h.shape
    M = B * S
    row_spec = pl.BlockSpec((tm, D), lambda i: (i, 0))
    w_specs = [_const_spec((1, D)), _const_spec((D, 2 * FFN_HIDDEN)), _const_spec((FFN_HIDDEN, D))]
    w_args = (norm_g.reshape(1, D), w_in.astype(BF16), w_out.astype(BF16))
    if attn is None:
        kern, specs, args = _ffn_kernel, [row_spec] + w_specs, (h.reshape(M, D),) + w_args
    else:
        kern = _proj_ffn_kernel
        specs = [row_spec, row_spec, _const_spec((D, D))] + w_specs
        args = (h.reshape(M, D), attn.reshape(M, D), w_o.astype(BF16)) + w_args
    out = pl.pallas_call(
        kern,
        grid=(M // tm,),
        in_specs=specs,
        out_specs=row_spec,
        out_shape=jax.ShapeDtypeStruct((M, D), F32),
        compiler_params=_cparams(1),
        name="swiglu_ffn",
    )(*args)
    return out.reshape(B, S, D)


def _kv_kernel(x_ref, g_ref, w_ref, kn_ref, cos_ref, sin_ref, ones_ref,
               kc_ref, vc_ref, ks_ref, vs_ref, kw_ref, vw_ref):
    W = KV_WIDTH
    u = _rms(x_ref[...], g_ref[...]).astype(BF16)
    kv = _dot(u, w_ref[...])
    cos_t = _tile_lanes(cos_ref[...], W)
    sin_t = _tile_lanes(sin_ref[...], W)
    ones_bd = ones_ref[...]

    def put_key(j, gain, k_ref):
        k = kv[:, j * W:(j + 1) * W]
        k = k * lax.rsqrt(_head_mean_sq(k, ones_bd) + EPS) * gain
        k = _rope_flat(k, cos_t, sin_t).astype(BF16)
        for g in range(N_KV_GROUPS):
            k_ref[g] = k[:, g * HEAD_DIM:(g + 1) * HEAD_DIM]

    def put_value_t(j, vt_ref):
        v_t = kv[:, j * W:(j + 1) * W].T
        for g in range(N_KV_GROUPS):
            for t in range(v_t.shape[1] // K_TILE):
                vt_ref[g, t, 0:HEAD_DIM, :] = v_t[g * HEAD_DIM:(g + 1) * HEAD_DIM,
                                                  t * K_TILE:(t + 1) * K_TILE].astype(BF16)
                vt_ref[g, t, HEAD_DIM:VT_ROWS, :] = jnp.ones((VT_ROWS - HEAD_DIM, K_TILE), BF16)

    kc_ref[...] = kv[:, 0 * W:1 * W]
    vc_ref[...] = kv[:, 1 * W:2 * W]
    put_key(2, kn_ref[1:2, :], ks_ref)
    put_value_t(3, vs_ref)
    put_key(4, kn_ref[2:3, :], kw_ref)
    put_value_t(5, vw_ref)


def _rope_tables(pos):
    half = HEAD_DIM // 2
    freqs = jnp.power(ROPE_THETA, -jnp.arange(half, dtype=F32) / half)
    ang = pos.astype(F32)[:, None] * freqs[None, :]
    cos, sin = jnp.cos(ang), jnp.sin(ang)
    cos_t = jnp.concatenate([cos, cos], axis=-1)
    sin_t = jnp.concatenate([-sin, sin], axis=-1)
    return cos_t, sin_t


def _block_diag_ones(width):
    seg = jnp.arange(width) // HEAD_DIM
    return (seg[:, None] == seg[None, :]).astype(BF16)


def _kv_proj(h, kv_norm, kv_w, k_norm, cos128, sin128, ts=512):
    B, S, D = h.shape
    W = KV_WIDTH
    kn = jnp.tile(k_norm, (1, N_KV_GROUPS))
    G = N_KV_GROUPS
    row = lambda width: pl.BlockSpec((None, ts, width), lambda b, s: (b, s, 0))
    tab = pl.BlockSpec((ts, LANES), lambda b, s: (s, 0))
    key_spec = pl.BlockSpec((None, G, ts, HEAD_DIM), lambda b, s: (b, 0, s, 0))
    val_spec = pl.BlockSpec((None, G, ts // K_TILE, VT_ROWS, K_TILE), lambda b, s: (b, 0, s, 0, 0))
    flat = jax.ShapeDtypeStruct((B, S, W), F32)
    keys = jax.ShapeDtypeStruct((B, G, S, HEAD_DIM), BF16)
    vals = jax.ShapeDtypeStruct((B, G, S // K_TILE, VT_ROWS, K_TILE), BF16)
    return pl.pallas_call(
        _kv_kernel,
        grid=(B, S // ts),
        in_specs=[row(D), _const_spec((1, D)), _const_spec((D, 6 * W)), _const_spec((3, W)),
                  tab, tab, _const_spec((W, W))],
        out_specs=[row(W), row(W), key_spec, val_spec, key_spec, val_spec],
        out_shape=[flat, flat, keys, vals, keys, vals],
        compiler_params=_cparams(2),
        name="shared_kv_proj",
    )(h, kv_norm.reshape(1, D), kv_w.astype(BF16), kn, cos128, sin128, _block_diag_ones(W))


N_CHUNK = 128
CHUNK_W = CMP_STRIDE * HEAD_DIM


def _cmp_kernel(x_ref, pos_ref, w1_ref, b1_ref, w2_ref, b2_ref, kn_ref, cos_ref, sin_ref, o_ref):
    rows = x_ref.shape[0]
    x = x_ref[...]
    ya = _dot((x + pos_ref[0:1, :]).astype(BF16), w1_ref[0:CHUNK_W, :])
    yb = _dot((x + pos_ref[1:2, :]).astype(BF16), w1_ref[CHUNK_W:2 * CHUNK_W, :])
    hid = _gelu_tanh(ya + pltpu.roll(yb, rows - 1, 0) + b1_ref[...])
    out = _dot(hid.astype(BF16), w2_ref[...]) + b2_ref[...]

    @pl.when(pl.program_id(0) == 0)
    def _():
        k = _rms(out, kn_ref[...])
        half = HEAD_DIM // 2
        partner = jnp.concatenate([k[:, half:], k[:, :half]], axis=1)
        o_ref[...] = k * cos_ref[...] + partner * sin_ref[...]

    @pl.when(pl.program_id(0) == 1)
    def _():
        o_ref[...] = out


def _compress(kc, vc, cmp_pos, cmp_w1, cmp_b1, cmp_w2, cmp_b2, k_norm0):
    B, S, _ = kc.shape
    G = N_KV_GROUPS
    rows = G * N_CHUNK

    def chunked(t):
        t = t.reshape(B, N_CHUNK, CMP_STRIDE, G, HEAD_DIM).transpose(0, 3, 1, 2, 4)
        return t.reshape(B, rows, CHUNK_W)

    x = jnp.stack([chunked(kc), chunked(vc)])
    pos = cmp_pos.reshape(2, 2, CHUNK_W)
    cmp_last = jnp.arange(N_CHUNK) * CMP_STRIDE + CMP_BLOCK - 1
    cos_t, sin_t = _rope_tables(cmp_last)
    cos_t, sin_t = jnp.tile(cos_t, (G, 1)), jnp.tile(sin_t, (G, 1))
    per_kv = lambda *shape: pl.BlockSpec((None,) + shape, lambda k, b: (k,) + (0,) * len(shape))
    return pl.pallas_call(
        _cmp_kernel,
        grid=(2, B),
        in_specs=[pl.BlockSpec((None, None, rows, CHUNK_W), lambda k, b: (k, b, 0, 0)),
                  per_kv(2, CHUNK_W), per_kv(2 * CHUNK_W, CMP_HIDDEN), per_kv(1, CMP_HIDDEN),
                  per_kv(CMP_HIDDEN, HEAD_DIM), per_kv(1, HEAD_DIM),
                  _const_spec((1, HEAD_DIM)), _const_spec((rows, HEAD_DIM)),
                  _const_spec((rows, HEAD_DIM))],
        out_specs=pl.BlockSpec((None, None, rows, HEAD_DIM), lambda k, b: (k, b, 0, 0)),
        out_shape=jax.ShapeDtypeStruct((2, B, rows, HEAD_DIM), F32),
        compiler_params=_cparams(2),
        name="kv_compress",
    )(x, pos, cmp_w1.astype(BF16), cmp_b1.reshape(2, 1, CMP_HIDDEN), cmp_w2.astype(BF16),
      cmp_b2.reshape(2, 1, HEAD_DIM), k_norm0.reshape(1, HEAD_DIM), cos_t, sin_t)


GATE_PAD = LANES
LOG2_E = 1.4426950408889634
Q_SCALE = HEAD_DIM ** -0.5 * LOG2_E


def _q_kernel(x_ref, g_ref, w_ref, gb_ref, qn_ref, cos_ref, sin_ref, ones_ref, q_ref, gate_ref):
    NQ = N_HEADS * HEAD_DIM
    u = _rms(x_ref[...], g_ref[...]).astype(BF16)
    z = _dot(u, w_ref[...])
    gate_ref[...] = _sigmoid(z[:, NQ:] + gb_ref[...]).T[0:3 * N_HEADS, :]
    ones_bd = ones_ref[...]
    W = ones_bd.shape[0]
    cos_t = _tile_lanes(cos_ref[...], W)
    sin_t = _tile_lanes(sin_ref[...], W)
    for c in range(NQ // W):
        q = z[:, c * W:(c + 1) * W]
        q = q * lax.rsqrt(_head_mean_sq(q, ones_bd) + EPS) * qn_ref[...]
        q = (_rope_flat(q, cos_t, sin_t) * Q_SCALE).astype(BF16)
        for j in range(W // HEAD_DIM):
            q_ref[c * (W // HEAD_DIM) + j] = q[:, j * HEAD_DIM:(j + 1) * HEAD_DIM]


def _q_proj(h, norm_g, w_in, gate_b, q_norm_g, cos128, sin128, ts=512):
    B, S, D = h.shape
    NQ = N_HEADS * HEAD_DIM
    n_gate = 3 * N_HEADS
    w = jnp.pad(w_in, ((0, 0), (0, GATE_PAD - n_gate))).astype(BF16)
    gb = jnp.pad(gate_b, (0, GATE_PAD - n_gate)).reshape(1, GATE_PAD)
    W = KV_WIDTH
    qn = jnp.tile(q_norm_g, W // HEAD_DIM).reshape(1, W)
    row = lambda width: pl.BlockSpec((None, ts, width), lambda b, s: (b, s, 0))
    tab = pl.BlockSpec((ts, LANES), lambda b, s: (s, 0))
    return pl.pallas_call(
        _q_kernel,
        grid=(B, S // ts),
        in_specs=[row(D), _const_spec((1, D)), _const_spec((D, NQ + GATE_PAD)),
                  _const_spec((1, GATE_PAD)), _const_spec((1, W)), tab, tab, _const_spec((W, W))],
        out_specs=[pl.BlockSpec((None, N_HEADS, ts, HEAD_DIM), lambda b, s: (b, 0, s, 0)),
                   pl.BlockSpec((None, n_gate, ts), lambda b, s: (b, 0, s))],
        out_shape=[jax.ShapeDtypeStruct((B, N_HEADS, S, HEAD_DIM), BF16),
                   jax.ShapeDtypeStruct((B, n_gate, S), F32)],
        compiler_params=_cparams(2),
        name="nsa_q_proj",
    )(h, norm_g.reshape(1, D), w, gb, qn, cos128, sin128, _block_diag_ones(W))


N_SLC = 32
PS_PAD = 8


def _flash_step(g, s, bias, v_t, m_scr, acc_scr):
    T = Q_TILE
    m_old = m_scr[g]
    m_new, p_all = [], []
    for h in range(HEADS_PER_GROUP):
        hs = slice(h * T, (h + 1) * T)
        sm = s[:, hs] + bias
        m_h = jnp.maximum(m_old[:, hs], jnp.max(sm, axis=0, keepdims=True))
        m_new.append(m_h)
        p_all.append(jnp.exp2(sm - m_h).astype(BF16))
    m_new = jnp.concatenate(m_new, axis=1)
    alpha = jnp.exp2(m_old - m_new)
    m_scr[g] = m_new
    acc_scr[g] = alpha * acc_scr[g] + _dot(v_t, jnp.concatenate(p_all, axis=1))


def _attn_kernel(q_ref, gt_ref, kc_ref, vct_ref, ks_ref, vst_ref, kw_ref, vwt_ref, o_ref,
                 sel_scr, ps_scr, m_scr, acc_scr, ot_scr):
    T = Q_TILE
    G = N_KV_GROUPS
    HG = HEADS_PER_GROUP
    NL = HG * T
    qi = pl.program_id(1)
    tpos = qi * T + lax.broadcasted_iota(jnp.int32, (1, T), 1)
    krow = lax.broadcasted_iota(jnp.int32, (K_TILE, T), 0)
    q_group = lambda g: q_ref[HG * g:HG * (g + 1)].reshape(NL, HEAD_DIM)
    key_tile = lambda k_ref, g, kt: k_ref[g, pl.ds(pl.multiple_of(kt * K_TILE, K_TILE), K_TILE), :]

    def gate(branch, g):
        r0 = branch * N_HEADS + HG * g
        return jnp.concatenate([gt_ref[r0 + h:r0 + h + 1, :] for h in range(HG)], axis=1)

    def emit(g, o_t, first):
        for h in range(HG):
            rows = slice((HG * g + h) * HEAD_DIM, (HG * g + h + 1) * HEAD_DIM)
            piece = o_t[:, h * T:(h + 1) * T]
            ot_scr[rows, :] = piece if first else ot_scr[rows, :] + piece

    def reset_state():
        for g in range(G):
            m_scr[g] = jnp.full((1, NL), NEG, F32)
            acc_scr[g] = jnp.zeros((VT_ROWS, NL), F32)

    def emit_state(branch):
        for g in range(G):
            denom = acc_scr[g, HEAD_DIM:HEAD_DIM + 1, :]
            emit(g, acc_scr[g, 0:HEAD_DIM, :] * (gate(branch, g) / denom), first=False)

    def sweep(k_ref, vt_ref, tiles, bias_of):
        scores = [[_dot_nt(key_tile(k_ref, g, kt), q_group(g)) for g in range(G)] for kt in tiles]
        for i, kt in enumerate(tiles):
            for g in range(G):
                _flash_step(g, scores[i][g], bias_of(g, kt), vt_ref[g, kt], m_scr, acc_scr)

    def sweep_pairs(k_ref, vt_ref, n_tiles, tile_of, bias_of):
        def pair(j, c):
            sweep(k_ref, vt_ref, [tile_of(2 * j), tile_of(2 * j + 1)], bias_of)
            return c

        lax.fori_loop(0, n_tiles // 2, pair, 0)

        @pl.when(n_tiles % 2 == 1)
        def _():
            sweep(k_ref, vt_ref, [tile_of(n_tiles - 1)], bias_of)

    for g in range(G):
        sc = _dot_nt(kc_ref[g], q_group(g))
        cvalid = ((krow * CMP_STRIDE + (CMP_BLOCK - 1)) <= tpos) & (krow < N_CHUNK - 1)
        psum = jnp.zeros((N_CHUNK, T), F32)
        probs = []
        for h in range(HG):
            sm = jnp.where(cvalid, sc[:, h * T:(h + 1) * T], NEG)
            e = jnp.where(cvalid, jnp.exp2(sm - jnp.max(sm, axis=0, keepdims=True)), 0.0)
            den = jnp.sum(e, axis=0, keepdims=True)
            p = e / jnp.where(den > 0.0, den, 1.0)
            psum = psum + p
            probs.append(p.astype(BF16))
        emit(g, gate(0, g) * _dot(vct_ref[g], jnp.concatenate(probs, axis=1)), first=True)
        ps_scr[g, 0:PS_PAD, :] = jnp.zeros((PS_PAD, T), F32)
        ps_scr[g, PS_PAD:PS_PAD + N_CHUNK, :] = psum

    all_causal_fit = (qi * T + T - 1) // SLC_BLOCK + 1 <= SLC_TOPK

    @pl.when(all_causal_fit)
    def _():
        for g in range(G):
            sel_scr[g] = jnp.zeros((N_SLC, T), F32)

    @pl.when(jnp.logical_not(all_causal_fit))
    def _():
        ROWS = 8
        jrow = lax.broadcasted_iota(jnp.int32, (N_SLC, T), 0)
        cur = tpos >> 6
        causal_blk = jrow <= cur
        forced = (jrow == 0) | (causal_blk & ((cur - jrow) < N_LOCAL_BLOCKS))
        for g in range(G):
            tap = lambda k: ps_scr[g, pl.ds(PS_PAD + k, N_SLC, stride=4), :]
            imp = 0.5 * tap(-1) + tap(0) + tap(1) + tap(2) + 0.5 * tap(3)
            score = jnp.where(forced, FORCE, jnp.where(causal_blk, imp, NEG))
            parts = [score[r:r + ROWS] for r in range(0, N_SLC, ROWS)]
            ranks = [jnp.zeros((ROWS, T), F32) for _ in parts]
            for j2 in range(N_SLC):
                other = score[j2:j2 + 1, :]
                for i, part in enumerate(parts):
                    r0 = i * ROWS
                    if r0 + ROWS - 1 < j2:
                        beats = other > part
                    elif r0 > j2:
                        beats = other >= part
                    else:
                        beats = (other > part) | ((other == part) & (jrow[r0:r0 + ROWS] > j2))
                    ranks[i] = ranks[i] + jnp.where(beats, 1.0, 0.0)
            rank = jnp.concatenate(ranks, axis=0)
            sel_scr[g] = jnp.where(rank < SLC_TOPK, 0.0, NEG)

    def causal_bias(kt):
        return jnp.where((kt * K_TILE + krow) <= tpos, 0.0, NEG)

    def slc_bias(g, kt):
        half = K_TILE // 2
        top = jnp.broadcast_to(sel_scr[g, pl.ds(2 * kt, 1), :], (half, T))
        bot = jnp.broadcast_to(sel_scr[g, pl.ds(2 * kt + 1, 1), :], (half, T))
        return jnp.concatenate([top, bot], axis=0) + causal_bias(kt)

    reset_state()
    sweep_pairs(ks_ref, vst_ref, qi + 1, lambda i: i, slc_bias)
    emit_state(1)

    def win_bias(g, kt):
        kpos = kt * K_TILE + krow
        return jnp.where((kpos <= tpos) & (kpos > (tpos - WINDOW)), 0.0, NEG)

    reset_state()
    sweep_pairs(kw_ref, vwt_ref, jnp.minimum(qi, WINDOW // K_TILE) + 1, lambda i: qi - i, win_bias)
    emit_state(2)

    o_ref[...] = ot_scr[...].T.astype(BF16)


def _nsa_attention(q, gates_t, k_cmp, v_cmp, ks, vst, kw, vwt):
    B, _, S, _ = q.shape
    G, T = N_KV_GROUPS, Q_TILE
    NQ = N_HEADS * HEAD_DIM
    NT = S // K_TILE
    kc = k_cmp.reshape(B, G, N_CHUNK, HEAD_DIM).astype(BF16)
    vct = v_cmp.reshape(B, G, N_CHUNK, HEAD_DIM).transpose(0, 1, 3, 2).astype(BF16)
    per_b = lambda *shape: pl.BlockSpec((None,) + shape, lambda b, i: (b,) + (0,) * len(shape))
    return pl.pallas_call(
        _attn_kernel,
        grid=(B, S // T),
        in_specs=[pl.BlockSpec((None, N_HEADS, T, HEAD_DIM), lambda b, i: (b, 0, i, 0)),
                  pl.BlockSpec((None, 3 * N_HEADS, T), lambda b, i: (b, 0, i)),
                  per_b(G, N_CHUNK, HEAD_DIM), per_b(G, HEAD_DIM, N_CHUNK),
                  per_b(G, S, HEAD_DIM), per_b(G, NT, VT_ROWS, K_TILE),
                  per_b(G, S, HEAD_DIM), per_b(G, NT, VT_ROWS, K_TILE)],
        out_specs=pl.BlockSpec((None, T, NQ), lambda b, i: (b, i, 0)),
        out_shape=jax.ShapeDtypeStruct((B, S, NQ), BF16),
        scratch_shapes=[pltpu.VMEM((G, N_SLC, T), F32), pltpu.VMEM((G, PS_PAD + N_CHUNK, T), F32),
                        pltpu.VMEM((G, 1, HEADS_PER_GROUP * T), F32),
                        pltpu.VMEM((G, VT_ROWS, HEADS_PER_GROUP * T), F32),
                        pltpu.VMEM((NQ, T), F32)],
        compiler_params=_cparams(2),
        name="nsa_attention",
    )(q, gates_t, kc, vct, ks, vst, kw, vwt)


def kernel(x, a_norm, a_w_in, a_conv_w, a_conv_b, a_gate_w, a_gate_b, a_lambda, a_w_out,
           kv_norm, kv_w, k_norm, cmp_pos, cmp_w1, cmp_b1, cmp_w2, cmp_b2,
           b_norm, b_w_in, b_gate_b, q_norm, b_w_out, f_norm, f_w_in, f_w_out):
    B, S, D = x.shape
    assert D == D_MODEL and S == N_SLC * SLC_BLOCK and S == N_CHUNK * CMP_STRIDE
    n_a = a_norm.shape[0]
    n_b = b_norm.shape[0]
    h = x
    for i in range(n_a):
        h = _recurrent_block(h, a_norm[i], a_w_in[i], a_conv_w[i], a_conv_b[i], a_gate_w[i],
                             a_gate_b[i], a_lambda[i], a_w_out[i])
        h = _swiglu(h, f_norm[i], f_w_in[i], f_w_out[i])

    cos_t, sin_t = _rope_tables(jnp.arange(S))
    cos128, sin128 = jnp.tile(cos_t, (1, 2)), jnp.tile(sin_t, (1, 2))
    kc, vc, ks, vs, kw, vw = _kv_proj(h, kv_norm, kv_w, k_norm, cos128, sin128)
    cmp = _compress(kc, vc, cmp_pos, cmp_w1, cmp_b1, cmp_w2, cmp_b2, k_norm[0])
    for j in range(n_b):
        q, gates = _q_proj(h, b_norm[j], b_w_in[j], b_gate_b[j], q_norm[j], cos128, sin128)
        o = _nsa_attention(q, gates, cmp[0], cmp[1], ks, vs, kw, vw)
        layer = n_a + j
        h = _swiglu(h, f_norm[layer], f_w_in[layer], f_w_out[layer], attn=o, w_o=b_w_out[j])
    return h
```

```python
import functools

import jax
import jax.numpy as jnp
from jax import lax
from jax.experimental import pallas as pl
from jax.experimental.pallas import tpu as pltpu

F32 = jnp.float32
BF16 = jnp.bfloat16

D_MODEL = 1024
LRU_WIDTH = D_MODEL
LRU_HEADS = 8
LRU_BLOCK = LRU_WIDTH // LRU_HEADS
CONV_WIDTH = 4
LRU_C = 8.0
HEAD_DIM = 64
N_HEADS = D_MODEL // HEAD_DIM
N_KV_GROUPS = 4
HEADS_PER_GROUP = N_HEADS // N_KV_GROUPS
CMP_BLOCK = 32
CMP_STRIDE = 16
CMP_HIDDEN = 256
SLC_BLOCK = 64
SLC_TOPK = 16
N_LOCAL_BLOCKS = 2
WINDOW = 512
ROPE_THETA = 10000.0
FFN_HIDDEN = 2816
EPS = 1e-6
NEG = -1e30
FORCE = 1e30

LANES = 128
KV_WIDTH = N_KV_GROUPS * HEAD_DIM
Q_TILE = 128
K_TILE = 128
VT_ROWS = HEAD_DIM + 16
VMEM_LIMIT = 56 * 1024 * 1024


def _cparams(n_axes):
    return pltpu.CompilerParams(dimension_semantics=("arbitrary",) * n_axes,
                                vmem_limit_bytes=VMEM_LIMIT)


def _const_spec(shape):
    nd = len(shape)
    return pl.BlockSpec(shape, lambda *_: (0,) * nd, pipeline_mode=pl.Buffered(1))


def _rms(x, g):
    ms = jnp.mean(x * x, axis=-1, keepdims=True)
    return x * lax.rsqrt(ms + EPS) * g


def _sigmoid(x):
    return 1.0 / (1.0 + jnp.exp(-x))


def _gelu_tanh(x):
    c = 0.7978845608028654
    return x * (0.5 * (1.0 + jnp.tanh(c * (x + 0.044715 * (x * x * x)))))


def _dot(a, b):
    return jnp.dot(a, b, preferred_element_type=F32)


def _head_mean_sq(x, ones_bd):
    sq = x * x
    hi = sq.astype(BF16)
    lo = (sq - hi.astype(F32)).astype(BF16)
    return (_dot(hi, ones_bd) + _dot(lo, ones_bd)) * (1.0 / HEAD_DIM)


def _rope_flat(x, cos_t, sin_t):
    width = x.shape[-1]
    lane = lax.broadcasted_iota(jnp.int32, x.shape, 1)
    upper = (lane & (HEAD_DIM // 2)) != 0
    partner = jnp.where(upper, pltpu.roll(x, HEAD_DIM // 2, 1),
                        pltpu.roll(x, width - HEAD_DIM // 2, 1))
    return x * cos_t + partner * sin_t


def _tile_lanes(t, width):
    reps = width // t.shape[-1]
    return t if reps == 1 else jnp.concatenate([t] * reps, axis=1)


SUBLANES = 8


def _scan_rows(a, b, h_in):
    n = a.shape[0]
    a3 = a.reshape(n // SUBLANES, SUBLANES, LANES)
    b3 = b.reshape(n // SUBLANES, SUBLANES, LANES)
    sub = lax.broadcasted_iota(jnp.int32, a3.shape, 1)
    d = 1
    while d < SUBLANES:
        valid = sub >= d
        b3 = jnp.where(valid, a3 * pltpu.roll(b3, d, 1) + b3, b3)
        a3 = jnp.where(valid, a3 * pltpu.roll(a3, d, 1), a3)
        d *= 2
    out = []
    for g in range(n // SUBLANES):
        hg = a3[g] * h_in + b3[g]
        out.append(hg)
        h_in = hg[SUBLANES - 1:SUBLANES, :]
    return jnp.concatenate(out, axis=0)


def _rec_kernel(x_ref, g_ref, win_ref, cw_ref, cb_ref, wg_ref, gb_ref, lam_ref, wout_ref, o_ref,
                xbuf, hcar, yh):
    ts = x_ref.shape[0]
    R = LRU_WIDTH

    @pl.when(pl.program_id(1) == 0)
    def _():
        xbuf[0:8, :] = jnp.zeros((8, R), F32)
        hcar[...] = jnp.zeros_like(hcar)

    x = x_ref[...]
    u = _rms(x, g_ref[...]).astype(BF16)
    z = _dot(u, win_ref[...])
    xbuf[8:8 + ts, :] = z[:, R:]
    for hh in range(LRU_HEADS):
        cs = slice(hh * LRU_BLOCK, (hh + 1) * LRU_BLOCK)
        cw = cw_ref[:, cs]
        xr = cb_ref[:, cs] + xbuf[5:5 + ts, cs] * cw[0:1]
        xr = xr + xbuf[6:6 + ts, cs] * cw[1:2]
        xr = xr + xbuf[7:7 + ts, cs] * cw[2:3]
        xr = xr + xbuf[8:8 + ts, cs] * cw[3:4]
        gates = _dot(xr.astype(BF16), wg_ref[hh])
        gb = gb_ref[:, cs]
        r = _sigmoid(gates[:, :LRU_BLOCK] + gb[0:1])
        i = _sigmoid(gates[:, LRU_BLOCK:] + gb[1:2])
        lam = lam_ref[:, cs]
        softplus_neg = jnp.maximum(-lam, 0.0) + jnp.log1p(jnp.exp(-jnp.abs(lam)))
        log_a = (-LRU_C * softplus_neg) * r
        a = jnp.exp(log_a)
        bterm = jnp.sqrt(1.0 - a * a) * (i * xr)
        hs = _scan_rows(a, bterm, hcar[:, cs])
        hcar[:, cs] = hs[ts - 1:ts, :]
        yh[:, cs] = (_gelu_tanh(z[:, cs]) * hs).astype(BF16)
    xbuf[0:8, :] = xbuf[ts:ts + 8, :]
    o_ref[...] = x + _dot(yh[...], wout_ref[...])


def _recurrent_block(h, norm_g, w_in, conv_w, conv_b, gate_w, gate_b, lam, w_out, ts=256):
    B, S, D = h.shape
    R = LRU_WIDTH
    wg = jnp.concatenate([gate_w[0], gate_w[1]], axis=-1).astype(BF16)
    return pl.pallas_call(
        _rec_kernel,
        grid=(B, S // ts),
        in_specs=[
            pl.BlockSpec((None, ts, D), lambda b, s: (b, s, 0)),
            _const_spec((1, D)),
            _const_spec((D, 2 * R)),
            _const_spec((CONV_WIDTH, R)),
            _const_spec((1, R)),
            _const_spec((LRU_HEADS, LRU_BLOCK, 2 * LRU_BLOCK)),
            _const_spec((2, R)),
            _const_spec((1, R)),
            _const_spec((R, D)),
        ],
        out_specs=pl.BlockSpec((None, ts, D), lambda b, s: (b, s, 0)),
        out_shape=jax.ShapeDtypeStruct((B, S, D), F32),
        scratch_shapes=[pltpu.VMEM((ts + 8, R), F32), pltpu.VMEM((1, R), F32),
                        pltpu.VMEM((ts, R), BF16)],
        compiler_params=_cparams(2),
        name="rglru_block",
    )(h, norm_g.reshape(1, D), w_in.astype(BF16), conv_w, conv_b.reshape(1, R), wg, gate_b,
      lam.reshape(1, R), w_out.astype(BF16))


FFN_CHUNK = FFN_HIDDEN // 2


def _ffn_body(x, g_ref, win_ref, wout_ref, o_ref):
    u = _rms(x, g_ref[...]).astype(BF16)
    acc = x
    for c in range(FFN_HIDDEN // FFN_CHUNK):
        lo = c * FFN_CHUNK
        gate = _dot(u, win_ref[:, lo:lo + FFN_CHUNK])
        up = _dot(u, win_ref[:, FFN_HIDDEN + lo:FFN_HIDDEN + lo + FFN_CHUNK])
        act = ((gate * _sigmoid(gate)) * up).astype(BF16)
        acc = acc + _dot(act, wout_ref[lo:lo + FFN_CHUNK, :])
    o_ref[...] = acc


def _ffn_kernel(x_ref, g_ref, win_ref, wout_ref, o_ref):
    _ffn_body(x_ref[...], g_ref, win_ref, wout_ref, o_ref)


def _proj_ffn_kernel(x_ref, a_ref, wo_ref, g_ref, win_ref, wout_ref, o_ref):
    _ffn_body(x_ref[...] + _dot(a_ref[...], wo_ref[...]), g_ref, win_ref, wout_ref, o_ref)


def _swiglu(h, norm_g, w_in, w_out, attn=None, w_o=None, tm=512):
    B, S, D = h.shape
    M = B * S
    row_spec = pl.BlockSpec((tm, D), lambda i: (i, 0))
    w_specs = [_const_spec((1, D)), _const_spec((D, 2 * FFN_HIDDEN)), _const_spec((FFN_HIDDEN, D))]
    w_args = (norm_g.reshape(1, D), w_in.astype(BF16), w_out.astype(BF16))
    if attn is None:
        kern, specs, args = _ffn_kernel, [row_spec] + w_specs, (h.reshape(M, D),) + w_args
    else:
        kern = _proj_ffn_kernel
        specs = [row_spec, row_spec, _const_spec((D, D))] + w_specs
        args = (h.reshape(M, D), attn.reshape(M, D), w_o.astype(BF16)) + w_args
    out = pl.pallas_call(
        kern,
        grid=(M // tm,),
        in_specs=specs,
        out_specs=row_spec,
        out_shape=jax.ShapeDtypeStruct((M, D), F32),
        compiler_params=_cparams(1),
        name="swiglu_ffn",
    )(*args)
    return out.reshape(B, S, D)


def _kv_kernel(x_ref, g_ref, w_ref, kn_ref, cos_ref, sin_ref, ones_ref,
               kc_ref, vc_ref, ks_ref, vs_ref, kw_ref, vw_ref):
    W = KV_WIDTH
    u = _rms(x_ref[...], g_ref[...]).astype(BF16)
    kv = _dot(u, w_ref[...])
    cos_t = _tile_lanes(cos_ref[...], W)
    sin_t = _tile_lanes(sin_ref[...], W)
    ones_bd = ones_ref[...]

    def put_key(j, gain, k_ref):
        k = kv[:, j * W:(j + 1) * W]
        k = k * lax.rsqrt(_head_mean_sq(k, ones_bd) + EPS) * gain
        k = _rope_flat(k, cos_t, sin_t).astype(BF16)
        for g in range(N_KV_GROUPS):
            k_ref[g] = k[:, g * HEAD_DIM:(g + 1) * HEAD_DIM]

    def put_value_t(j, vt_ref):
        v_t = kv[:, j * W:(j + 1) * W].T
        for g in range(N_KV_GROUPS):
            for t in range(v_t.shape[1] // K_TILE):
                vt_ref[g, t, 0:HEAD_DIM, :] = v_t[g * HEAD_DIM:(g + 1) * HEAD_DIM,
                                                  t * K_TILE:(t + 1) * K_TILE].astype(BF16)
                vt_ref[g, t, HEAD_DIM:VT_ROWS, :] = jnp.ones((VT_ROWS - HEAD_DIM, K_TILE), BF16)

    kc_ref[...] = kv[:, 0 * W:1 * W]
    vc_ref[...] = kv[:, 1 * W:2 * W]
    put_key(2, kn_ref[1:2, :], ks_ref)
    put_value_t(3, vs_ref)
    put_key(4, kn_ref[2:3, :], kw_ref)
    put_value_t(5, vw_ref)


def _rope_tables(pos):
    half = HEAD_DIM // 2
    freqs = jnp.power(ROPE_THETA, -jnp.arange(half, dtype=F32) / half)
    ang = pos.astype(F32)[:, None] * freqs[None, :]
    cos, sin = jnp.cos(ang), jnp.sin(ang)
    cos_t = jnp.concatenate([cos, cos], axis=-1)
    sin_t = jnp.concatenate([-sin, sin], axis=-1)
    return cos_t, sin_t


def _block_diag_ones(width):
    seg = jnp.arange(width) // HEAD_DIM
    return (seg[:, None] == seg[None, :]).astype(BF16)


def _kv_proj(h, kv_norm, kv_w, k_norm, cos128, sin128, ts=512):
    B, S, D = h.shape
    W = KV_WIDTH
    kn = jnp.tile(k_norm, (1, N_KV_GROUPS))
    G = N_KV_GROUPS
    row = lambda width: pl.BlockSpec((None, ts, width), lambda b, s: (b, s, 0))
    tab = pl.BlockSpec((ts, LANES), lambda b, s: (s, 0))
    key_spec = pl.BlockSpec((None, G, ts, HEAD_DIM), lambda b, s: (b, 0, s, 0))
    val_spec = pl.BlockSpec((None, G, ts // K_TILE, VT_ROWS, K_TILE), lambda b, s: (b, 0, s, 0, 0))
    flat = jax.ShapeDtypeStruct((B, S, W), F32)
    keys = jax.ShapeDtypeStruct((B, G, S, HEAD_DIM), BF16)
    vals = jax.ShapeDtypeStruct((B, G, S // K_TILE, VT_ROWS, K_TILE), BF16)
    return pl.pallas_call(
        _kv_kernel,
        grid=(B, S // ts),
        in_specs=[row(D), _const_spec((1, D)), _const_spec((D, 6 * W)), _const_spec((3, W)),
                  tab, tab, _const_spec((W, W))],
        out_specs=[row(W), row(W), key_spec, val_spec, key_spec, val_spec],
        out_shape=[flat, flat, keys, vals, keys, vals],
        compiler_params=_cparams(2),
        name="shared_kv_proj",
    )(h, kv_norm.reshape(1, D), kv_w.astype(BF16), kn, cos128, sin128, _block_diag_ones(W))


N_CHUNK = 128
CHUNK_W = CMP_STRIDE * HEAD_DIM


def _cmp_kernel(x_ref, pos_ref, w1_ref, b1_ref, w2_ref, b2_ref, kn_ref, cos_ref, sin_ref, o_ref):
    rows = x_ref.shape[0]
    x = x_ref[...]
    ya = _dot((x + pos_ref[0:1, :]).astype(BF16), w1_ref[0:CHUNK_W, :])
    yb = _dot((x + pos_ref[1:2, :]).astype(BF16), w1_ref[CHUNK_W:2 * CHUNK_W, :])
    hid = _gelu_tanh(ya + pltpu.roll(yb, rows - 1, 0) + b1_ref[...])
    out = _dot(hid.astype(BF16), w2_ref[...]) + b2_ref[...]

    @pl.when(pl.program_id(0) == 0)
    def _():
        k = _rms(out, kn_ref[...])
        half = HEAD_DIM // 2
        partner = jnp.concatenate([k[:, half:], k[:, :half]], axis=1)
        o_ref[...] = k * cos_ref[...] + partner * sin_ref[...]

    @pl.when(pl.program_id(0) == 1)
    def _():
        o_ref[...] = out


def _compress(kc, vc, cmp_pos, cmp_w1, cmp_b1, cmp_w2, cmp_b2, k_norm0):
    B, S, _ = kc.shape
    G = N_KV_GROUPS
    rows = G * N_CHUNK

    def chunked(t):
        t = t.reshape(B, N_CHUNK, CMP_STRIDE, G, HEAD_DIM).transpose(0, 3, 1, 2, 4)
        return t.reshape(B, rows, CHUNK_W)

    x = jnp.stack([chunked(kc), chunked(vc)])
    pos = cmp_pos.reshape(2, 2, CHUNK_W)
    cmp_last = jnp.arange(N_CHUNK) * CMP_STRIDE + CMP_BLOCK - 1
    cos_t, sin_t = _rope_tables(cmp_last)
    cos_t, sin_t = jnp.tile(cos_t, (G, 1)), jnp.tile(sin_t, (G, 1))
    per_kv = lambda *shape: pl.BlockSpec((None,) + shape, lambda k, b: (k,) + (0,) * len(shape))
    return pl.pallas_call(
        _cmp_kernel,
        grid=(2, B),
        in_specs=[pl.BlockSpec((None, None, rows, CHUNK_W), lambda k, b: (k, b, 0, 0)),
                  per_kv(2, CHUNK_W), per_kv(2 * CHUNK_W, CMP_HIDDEN), per_kv(1, CMP_HIDDEN),
                  per_kv(CMP_HIDDEN, HEAD_DIM), per_kv(1, HEAD_DIM),
                  _const_spec((1, HEAD_DIM)), _const_spec((rows, HEAD_DIM)),
                  _const_spec((rows, HEAD_DIM))],
        out_specs=pl.BlockSpec((None, None, rows, HEAD_DIM), lambda k, b: (k, b, 0, 0)),
        out_shape=jax.ShapeDtypeStruct((2, B, rows, HEAD_DIM), F32),
        compiler_params=_cparams(2),
        name="kv_compress",
    )(x, pos, cmp_w1.astype(BF16), cmp_b1.reshape(2, 1, CMP_HIDDEN), cmp_w2.astype(BF16),
      cmp_b2.reshape(2, 1, HEAD_DIM), k_norm0.reshape(1, HEAD_DIM), cos_t, sin_t)


GATE_PAD = LANES
LOG2_E = 1.4426950408889634
Q_SCALE = HEAD_DIM ** -0.5 * LOG2_E


def _q_kernel(x_ref, g_ref, w_ref, gb_ref, qn_ref, cos_ref, sin_ref, ones_ref, q_ref, gate_ref):
    NQ = N_HEADS * HEAD_DIM
    u = _rms(x_ref[...], g_ref[...]).astype(BF16)
    z = _dot(u, w_ref[...])
    gate_ref[...] = _sigmoid(z[:, NQ:] + gb_ref[...]).T[0:3 * N_HEADS, :]
    ones_bd = ones_ref[...]
    W = ones_bd.shape[0]
    cos_t = _tile_lanes(cos_ref[...], W)
    sin_t = _tile_lanes(sin_ref[...], W)
    for c in range(NQ // W):
        q = z[:, c * W:(c + 1) * W]
        q = q * lax.rsqrt(_head_mean_sq(q, ones_bd) + EPS) * qn_ref[...]
        q_tr = (_rope_flat(q, cos_t, sin_t) * Q_SCALE).T
        for t in range(q_tr.shape[1] // Q_TILE):
            q_ref[c, t] = jnp.concatenate(
                [q_tr[j * HEAD_DIM:(j + 1) * HEAD_DIM, t * Q_TILE:(t + 1) * Q_TILE]
                 for j in range(W // HEAD_DIM)], axis=1).astype(BF16)


def _q_proj(h, norm_g, w_in, gate_b, q_norm_g, cos128, sin128, ts=512):
    B, S, D = h.shape
    NQ = N_HEADS * HEAD_DIM
    n_gate = 3 * N_HEADS
    w = jnp.pad(w_in, ((0, 0), (0, GATE_PAD - n_gate))).astype(BF16)
    gb = jnp.pad(gate_b, (0, GATE_PAD - n_gate)).reshape(1, GATE_PAD)
    W = KV_WIDTH
    qn = jnp.tile(q_norm_g, W // HEAD_DIM).reshape(1, W)
    row = lambda width: pl.BlockSpec((None, ts, width), lambda b, s: (b, s, 0))
    tab = pl.BlockSpec((ts, LANES), lambda b, s: (s, 0))
    return pl.pallas_call(
        _q_kernel,
        grid=(B, S // ts),
        in_specs=[row(D), _const_spec((1, D)), _const_spec((D, NQ + GATE_PAD)),
                  _const_spec((1, GATE_PAD)), _const_spec((1, W)), tab, tab, _const_spec((W, W))],
        out_specs=[pl.BlockSpec((None, N_KV_GROUPS, ts // Q_TILE, HEAD_DIM, HEADS_PER_GROUP * Q_TILE),
                                lambda b, s: (b, 0, s, 0, 0)),
                   pl.BlockSpec((None, n_gate, ts), lambda b, s: (b, 0, s))],
        out_shape=[jax.ShapeDtypeStruct((B, N_KV_GROUPS, S // Q_TILE, HEAD_DIM,
                                         HEADS_PER_GROUP * Q_TILE), BF16),
                   jax.ShapeDtypeStruct((B, n_gate, S), F32)],
        compiler_params=_cparams(2),
        name="nsa_q_proj",
    )(h, norm_g.reshape(1, D), w, gb, qn, cos128, sin128, _block_diag_ones(W))


N_SLC = 32
PS_PAD = 8


def _flash_step(g, parts, v_t, m_scr, acc_scr):
    T = Q_TILE
    m_old = m_scr[g]
    m_new, p_all = [], []
    for h in range(HEADS_PER_GROUP):
        hs = slice(h * T, (h + 1) * T)
        sm = [s[:, hs] if bias is None else s[:, hs] + bias for s, bias in parts]
        m_h = m_old[:, hs]
        for x in sm:
            m_h = jnp.maximum(m_h, jnp.max(x, axis=0, keepdims=True))
        m_new.append(m_h)
        p_all.append(jnp.concatenate([jnp.exp2(x - m_h).astype(BF16) for x in sm], axis=0))
    m_new = jnp.concatenate(m_new, axis=1)
    alpha = jnp.exp2(m_old - m_new)
    m_scr[g] = m_new
    acc_scr[g] = alpha * acc_scr[g] + _dot(v_t, jnp.concatenate(p_all, axis=1))


def _attn_kernel(q_ref, gt_ref, kc_ref, vct_ref, ks_ref, vst_ref, kw_ref, vwt_ref, o_ref,
                 sel_scr, ps_scr, m_scr, acc_scr, ot_scr):
    T = Q_TILE
    G = N_KV_GROUPS
    HG = HEADS_PER_GROUP
    NL = HG * T
    qi = pl.program_id(1)
    tpos = qi * T + lax.broadcasted_iota(jnp.int32, (1, T), 1)
    krow = lax.broadcasted_iota(jnp.int32, (K_TILE, T), 0)
    q_t = lambda g: q_ref[g]
    key_tile = lambda k_ref, g, kt: k_ref[g, pl.ds(pl.multiple_of(kt * K_TILE, K_TILE), K_TILE), :]

    def gate(branch, g):
        r0 = branch * N_HEADS + HG * g
        return jnp.concatenate([gt_ref[r0 + h:r0 + h + 1, :] for h in range(HG)], axis=1)

    def emit(g, o_t, first):
        for h in range(HG):
            rows = slice((HG * g + h) * HEAD_DIM, (HG * g + h + 1) * HEAD_DIM)
            piece = o_t[:, h * T:(h + 1) * T]
            ot_scr[rows, :] = piece if first else ot_scr[rows, :] + piece

    def reset_state():
        for g in range(G):
            m_scr[g] = jnp.full((1, NL), NEG, F32)
            acc_scr[g] = jnp.zeros((VT_ROWS, NL), F32)

    def emit_state(branch):
        for g in range(G):
            denom = acc_scr[g, HEAD_DIM:HEAD_DIM + 1, :]
            emit(g, acc_scr[g, 0:HEAD_DIM, :] * (gate(branch, g) / denom), first=False)

    def sweep(k_ref, vt_ref, tiles, biases):
        pairs = [range(lo, min(lo + 2, len(tiles))) for lo in range(0, len(tiles), 2)]
        scores = {(i, g): _dot(key_tile(k_ref, g, tiles[i]), q_t(g))
                  for pair in pairs for g in range(G) for i in pair}
        for pair in pairs:
            for g in range(G):
                parts = [(scores[i, g], None if biases[i] is None else biases[i](g)) for i in pair]
                v_t = jnp.concatenate([vt_ref[g, tiles[i]] for i in pair], axis=1)
                _flash_step(g, parts, v_t, m_scr, acc_scr)

    def sweep_range(k_ref, vt_ref, n_tiles, bias_of):
        def run(tiles):
            sweep(k_ref, vt_ref, tiles,
                  [None if bias_of is None else functools.partial(bias_of, kt=kt) for kt in tiles])

        def quad(j, c):
            run([4 * j + i for i in range(4)])
            return c

        lax.fori_loop(0, n_tiles // 4, quad, 0)
        rem = n_tiles % 4
        base = n_tiles - rem

        @pl.when(rem >= 2)
        def _():
            run([base, base + 1])

        @pl.when(rem % 2 == 1)
        def _():
            run([n_tiles - 1])

    for g in range(G):
        sc = _dot(kc_ref[g], q_t(g))
        cvalid = ((krow * CMP_STRIDE + (CMP_BLOCK - 1)) <= tpos) & (krow < N_CHUNK - 1)
        psum = jnp.zeros((N_CHUNK, T), F32)
        probs = []
        for h in range(HG):
            sm = jnp.where(cvalid, sc[:, h * T:(h + 1) * T], NEG)
            e = jnp.where(cvalid, jnp.exp2(sm - jnp.max(sm, axis=0, keepdims=True)), 0.0)
            den = jnp.sum(e, axis=0, keepdims=True)
            p = e / jnp.where(den > 0.0, den, 1.0)
            psum = psum + p
            probs.append(p.astype(BF16))
        emit(g, gate(0, g) * _dot(vct_ref[g], jnp.concatenate(probs, axis=1)), first=True)
        ps_scr[g, 0:PS_PAD, :] = jnp.zeros((PS_PAD, T), F32)
        ps_scr[g, PS_PAD:PS_PAD + N_CHUNK, :] = psum

    all_causal_fit = (qi * T + T - 1) // SLC_BLOCK + 1 <= SLC_TOPK

    @pl.when(all_causal_fit)
    def _():
        for g in range(G):
            sel_scr[g] = jnp.zeros((N_SLC, T), F32)

    @pl.when(jnp.logical_not(all_causal_fit))
    def _():
        ROWS = 8
        jrow = lax.broadcasted_iota(jnp.int32, (N_SLC, T), 0)
        cur = tpos >> 6
        causal_blk = jrow <= cur
        forced = (jrow == 0) | (causal_blk & ((cur - jrow) < N_LOCAL_BLOCKS))
        for g in range(G):
            tap = lambda k: ps_scr[g, pl.ds(PS_PAD + k, N_SLC, stride=4), :]
            imp = 0.5 * tap(-1) + tap(0) + tap(1) + tap(2) + 0.5 * tap(3)
            score = jnp.where(forced, FORCE, jnp.where(causal_blk, imp, NEG))
            parts = [score[r:r + ROWS] for r in range(0, N_SLC, ROWS)]
            ranks = [jnp.zeros((ROWS, T), F32) for _ in parts]
            for j2 in range(N_SLC):
                other = score[j2:j2 + 1, :]
                for i, part in enumerate(parts):
                    r0 = i * ROWS
                    if r0 + ROWS - 1 < j2:
                        beats = other > part
                    elif r0 > j2:
                        beats = other >= part
                    else:
                        beats = (other > part) | ((other == part) & (jrow[r0:r0 + ROWS] > j2))
                    ranks[i] = ranks[i] + jnp.where(beats, 1.0, 0.0)
            rank = jnp.concatenate(ranks, axis=0)
            sel_scr[g] = jnp.where(rank < SLC_TOPK, 0.0, NEG)

    def slc_bias(g, kt):
        half = K_TILE // 2
        top = jnp.broadcast_to(sel_scr[g, pl.ds(2 * kt, 1), :], (half, T))
        bot = jnp.broadcast_to(sel_scr[g, pl.ds(2 * kt + 1, 1), :], (half, T))
        causal = jnp.where((kt * K_TILE + krow) <= tpos, 0.0, NEG)
        return jnp.concatenate([top, bot], axis=0) + causal

    reset_state()
    sweep_range(ks_ref, vst_ref, qi + 1, slc_bias)
    emit_state(1)

    n_back = WINDOW // K_TILE

    def causal_bias(g):
        return jnp.where((qi * K_TILE + krow) <= tpos, 0.0, NEG)

    def window_edge_bias(g):
        return jnp.where(((qi - n_back) * K_TILE + krow) > (tpos - WINDOW), 0.0, NEG)

    reset_state()
    sweep(kw_ref, vwt_ref, [qi], [causal_bias])

    @pl.when(qi >= n_back)
    def _():
        sweep(kw_ref, vwt_ref, [qi - 1 - i for i in range(n_back)],
              [None] * (n_back - 1) + [window_edge_bias])

    @pl.when(qi < n_back)
    def _():
        sweep_range(kw_ref, vwt_ref, qi, None)
    emit_state(2)

    o_ref[...] = ot_scr[...].T.astype(BF16)


def _nsa_attention(q, gates_t, k_cmp, v_cmp, ks, vst, kw, vwt):
    B, _, S, _ = ks.shape
    G, T = N_KV_GROUPS, Q_TILE
    NQ = N_HEADS * HEAD_DIM
    NT = S // K_TILE
    kc = k_cmp.reshape(B, G, N_CHUNK, HEAD_DIM).astype(BF16)
    vct = v_cmp.reshape(B, G, N_CHUNK, HEAD_DIM).transpose(0, 1, 3, 2).astype(BF16)
    per_b = lambda *shape: pl.BlockSpec((None,) + shape, lambda b, i: (b,) + (0,) * len(shape))
    return pl.pallas_call(
        _attn_kernel,
        grid=(B, S // T),
        in_specs=[pl.BlockSpec((None, G, None, HEAD_DIM, HEADS_PER_GROUP * T),
                               lambda b, i: (b, 0, i, 0, 0)),
                  pl.BlockSpec((None, 3 * N_HEADS, T), lambda b, i: (b, 0, i)),
                  per_b(G, N_CHUNK, HEAD_DIM), per_b(G, HEAD_DIM, N_CHUNK),
                  per_b(G, S, HEAD_DIM), per_b(G, NT, VT_ROWS, K_TILE),
                  per_b(G, S, HEAD_DIM), per_b(G, NT, VT_ROWS, K_TILE)],
        out_specs=pl.BlockSpec((None, T, NQ), lambda b, i: (b, i, 0)),
        out_shape=jax.ShapeDtypeStruct((B, S, NQ), BF16),
        scratch_shapes=[pltpu.VMEM((G, N_SLC, T), F32), pltpu.VMEM((G, PS_PAD + N_CHUNK, T), F32),
                        pltpu.VMEM((G, 1, HEADS_PER_GROUP * T), F32),
                        pltpu.VMEM((G, VT_ROWS, HEADS_PER_GROUP * T), F32),
                        pltpu.VMEM((NQ, T), F32)],
        compiler_params=_cparams(2),
        name="nsa_attention",
    )(q, gates_t, kc, vct, ks, vst, kw, vwt)


def kernel(x, a_norm, a_w_in, a_conv_w, a_conv_b, a_gate_w, a_gate_b, a_lambda, a_w_out,
           kv_norm, kv_w, k_norm, cmp_pos, cmp_w1, cmp_b1, cmp_w2, cmp_b2,
           b_norm, b_w_in, b_gate_b, q_norm, b_w_out, f_norm, f_w_in, f_w_out):
    B, S, D = x.shape
    assert D == D_MODEL and S == N_SLC * SLC_BLOCK and S == N_CHUNK * CMP_STRIDE
    n_a = a_norm.shape[0]
    n_b = b_norm.shape[0]
    h = x
    for i in range(n_a):
        h = _recurrent_block(h, a_norm[i], a_w_in[i], a_conv_w[i], a_conv_b[i], a_gate_w[i],
                             a_gate_b[i], a_lambda[i], a_w_out[i])
        h = _swiglu(h, f_norm[i], f_w_in[i], f_w_out[i])

    cos_t, sin_t = _rope_tables(jnp.arange(S))
    cos128, sin128 = jnp.tile(cos_t, (1, 2)), jnp.tile(sin_t, (1, 2))
    kc, vc, ks, vs, kw, vw = _kv_proj(h, kv_norm, kv_w, k_norm, cos128, sin128)
    cmp = _compress(kc, vc, cmp_pos, cmp_w1, cmp_b1, cmp_w2, cmp_b2, k_norm[0])
    for j in range(n_b):
        q, gates = _q_proj(h, b_norm[j], b_w_in[j], b_gate_b[j], q_norm[j], cos128, sin128)
        o = _nsa_attention(q, gates, cmp[0], cmp[1], ks, vs, kw, vw)
        layer = n_a + j
        h = _swiglu(h, f_norm[layer], f_w_in[layer], f_w_out[layer], attn=o, w_o=b_w_out[j])
    return h
```

```python
import functools

import jax
import jax.numpy as jnp
from jax import lax
from jax.experimental import pallas as pl
from jax.experimental.pallas import tpu as pltpu

F32 = jnp.float32
BF16 = jnp.bfloat16

D_MODEL = 1024
LRU_WIDTH = D_MODEL
LRU_HEADS = 8
LRU_BLOCK = LRU_WIDTH // LRU_HEADS
CONV_WIDTH = 4
LRU_C = 8.0
HEAD_DIM = 64
N_HEADS = D_MODEL // HEAD_DIM
N_KV_GROUPS = 4
HEADS_PER_GROUP = N_HEADS // N_KV_GROUPS
CMP_BLOCK = 32
CMP_STRIDE = 16
CMP_HIDDEN = 256
SLC_BLOCK = 64
SLC_TOPK = 16
N_LOCAL_BLOCKS = 2
WINDOW = 512
ROPE_THETA = 10000.0
FFN_HIDDEN = 2816
EPS = 1e-6
NEG = -1e30
FORCE = 1e30

LANES = 128
KV_WIDTH = N_KV_GROUPS * HEAD_DIM
Q_TILE = 128
K_TILE = 128
VT_ROWS = HEAD_DIM + 16
VMEM_LIMIT = 56 * 1024 * 1024


def _cparams(n_axes):
    return pltpu.CompilerParams(dimension_semantics=("arbitrary",) * n_axes,
                                vmem_limit_bytes=VMEM_LIMIT)


def _const_spec(shape):
    nd = len(shape)
    return pl.BlockSpec(shape, lambda *_: (0,) * nd, pipeline_mode=pl.Buffered(1))


def _rms(x, g):
    ms = jnp.mean(x * x, axis=-1, keepdims=True)
    return x * lax.rsqrt(ms + EPS) * g


def _sigmoid(x):
    return 1.0 / (1.0 + jnp.exp(-x))


def _gelu_tanh(x):
    c = 0.7978845608028654
    return x * (0.5 * (1.0 + jnp.tanh(c * (x + 0.044715 * (x * x * x)))))


def _dot(a, b):
    return jnp.dot(a, b, preferred_element_type=F32)


def _head_mean_sq(x, ones_bd):
    sq = x * x
    hi = sq.astype(BF16)
    lo = (sq - hi.astype(F32)).astype(BF16)
    return (_dot(hi, ones_bd) + _dot(lo, ones_bd)) * (1.0 / HEAD_DIM)


def _rope_flat(x, cos_t, sin_t):
    width = x.shape[-1]
    lane = lax.broadcasted_iota(jnp.int32, x.shape, 1)
    upper = (lane & (HEAD_DIM // 2)) != 0
    partner = jnp.where(upper, pltpu.roll(x, HEAD_DIM // 2, 1),
                        pltpu.roll(x, width - HEAD_DIM // 2, 1))
    return x * cos_t + partner * sin_t


def _tile_lanes(t, width):
    reps = width // t.shape[-1]
    return t if reps == 1 else jnp.concatenate([t] * reps, axis=1)


SUBLANES = 8


def _scan_rows(a, b, h_in):
    n = a.shape[0]
    a3 = a.reshape(n // SUBLANES, SUBLANES, LANES)
    b3 = b.reshape(n // SUBLANES, SUBLANES, LANES)
    sub = lax.broadcasted_iota(jnp.int32, a3.shape, 1)
    d = 1
    while d < SUBLANES:
        valid = sub >= d
        b3 = jnp.where(valid, a3 * pltpu.roll(b3, d, 1) + b3, b3)
        a3 = jnp.where(valid, a3 * pltpu.roll(a3, d, 1), a3)
        d *= 2
    out = []
    for g in range(n // SUBLANES):
        hg = a3[g] * h_in + b3[g]
        out.append(hg)
        h_in = hg[SUBLANES - 1:SUBLANES, :]
    return jnp.concatenate(out, axis=0)


def _rec_kernel(x_ref, g_ref, win_ref, cw_ref, cb_ref, wg_ref, gb_ref, lam_ref, wout_ref, o_ref,
                xbuf, hcar, yh):
    ts = x_ref.shape[0]
    R = LRU_WIDTH

    @pl.when(pl.program_id(1) == 0)
    def _():
        xbuf[0:8, :] = jnp.zeros((8, R), F32)
        hcar[...] = jnp.zeros_like(hcar)

    x = x_ref[...]
    u = _rms(x, g_ref[...]).astype(BF16)
    z = _dot(u, win_ref[...])
    xbuf[8:8 + ts, :] = z[:, R:]
    for hh in range(LRU_HEADS):
        cs = slice(hh * LRU_BLOCK, (hh + 1) * LRU_BLOCK)
        cw = cw_ref[:, cs]
        xr = cb_ref[:, cs] + xbuf[5:5 + ts, cs] * cw[0:1]
        xr = xr + xbuf[6:6 + ts, cs] * cw[1:2]
        xr = xr + xbuf[7:7 + ts, cs] * cw[2:3]
        xr = xr + xbuf[8:8 + ts, cs] * cw[3:4]
        gates = _dot(xr.astype(BF16), wg_ref[hh])
        gb = gb_ref[:, cs]
        r = _sigmoid(gates[:, :LRU_BLOCK] + gb[0:1])
        i = _sigmoid(gates[:, LRU_BLOCK:] + gb[1:2])
        lam = lam_ref[:, cs]
        softplus_neg = jnp.maximum(-lam, 0.0) + jnp.log1p(jnp.exp(-jnp.abs(lam)))
        log_a = (-LRU_C * softplus_neg) * r
        a = jnp.exp(log_a)
        bterm = jnp.sqrt(1.0 - a * a) * (i * xr)
        hs = _scan_rows(a, bterm, hcar[:, cs])
        hcar[:, cs] = hs[ts - 1:ts, :]
        yh[:, cs] = (_gelu_tanh(z[:, cs]) * hs).astype(BF16)
    xbuf[0:8, :] = xbuf[ts:ts + 8, :]
    o_ref[...] = x + _dot(yh[...], wout_ref[...])


def _recurrent_block(h, norm_g, w_in, conv_w, conv_b, gate_w, gate_b, lam, w_out, ts=256):
    B, S, D = h.shape
    R = LRU_WIDTH
    wg = jnp.concatenate([gate_w[0], gate_w[1]], axis=-1).astype(BF16)
    return pl.pallas_call(
        _rec_kernel,
        grid=(B, S // ts),
        in_specs=[
            pl.BlockSpec((None, ts, D), lambda b, s: (b, s, 0)),
            _const_spec((1, D)),
            _const_spec((D, 2 * R)),
            _const_spec((CONV_WIDTH, R)),
            _const_spec((1, R)),
            _const_spec((LRU_HEADS, LRU_BLOCK, 2 * LRU_BLOCK)),
            _const_spec((2, R)),
            _const_spec((1, R)),
            _const_spec((R, D)),
        ],
        out_specs=pl.BlockSpec((None, ts, D), lambda b, s: (b, s, 0)),
        out_shape=jax.ShapeDtypeStruct((B, S, D), F32),
        scratch_shapes=[pltpu.VMEM((ts + 8, R), F32), pltpu.VMEM((1, R), F32),
                        pltpu.VMEM((ts, R), BF16)],
        compiler_params=_cparams(2),
        name="rglru_block",
    )(h, norm_g.reshape(1, D), w_in.astype(BF16), conv_w, conv_b.reshape(1, R), wg, gate_b,
      lam.reshape(1, R), w_out.astype(BF16))


FFN_CHUNK = FFN_HIDDEN // 2


def _ffn_body(x, g_ref, win_ref, wout_ref, o_ref):
    u = _rms(x, g_ref[...]).astype(BF16)

    acc = x
    for c in range(FFN_HIDDEN // FFN_CHUNK):
        lo = c * FFN_CHUNK
        gate = _dot(u, win_ref[:, lo:lo + FFN_CHUNK])
        up = _dot(u, win_ref[:, FFN_HIDDEN + lo:FFN_HIDDEN + lo + FFN_CHUNK])
        act = ((gate * _sigmoid(gate)) * up).astype(BF16)
        acc = acc + _dot(act, wout_ref[lo:lo + FFN_CHUNK, :])
    o_ref[...] = acc


def _ffn_kernel(x_ref, g_ref, win_ref, wout_ref, o_ref):
    _ffn_body(x_ref[...], g_ref, win_ref, wout_ref, o_ref)


def _proj_ffn_kernel(x_ref, a_ref, wo_ref, g_ref, win_ref, wout_ref, o_ref):
    _ffn_body(x_ref[...] + _dot(a_ref[...], wo_ref[...]), g_ref, win_ref, wout_ref, o_ref)


def _swiglu(h, norm_g, w_in, w_out, attn=None, w_o=None, tm=512):
    B, S, D = h.shape
    M = B * S
    row_spec = pl.BlockSpec((tm, D), lambda i: (i, 0))
    w_specs = [_const_spec((1, D)), _const_spec((D, 2 * FFN_HIDDEN)), _const_spec((FFN_HIDDEN, D))]
    w_args = (norm_g.reshape(1, D), w_in.astype(BF16), w_out.astype(BF16))
    if attn is None:
        kern, specs, args = _ffn_kernel, [row_spec] + w_specs, (h.reshape(M, D),) + w_args
    else:
        kern = _proj_ffn_kernel
        specs = [row_spec, row_spec, _const_spec((D, D))] + w_specs
        args = (h.reshape(M, D), attn.reshape(M, D), w_o.astype(BF16)) + w_args
    out = pl.pallas_call(
        kern,
        grid=(M // tm,),
        in_specs=specs,
        out_specs=row_spec,
        out_shape=jax.ShapeDtypeStruct((M, D), F32),
        compiler_params=_cparams(1),
        name="swiglu_ffn",
    )(*args)
    return out.reshape(B, S, D)


def _kv_kernel(x_ref, g_ref, w_ref, kn_ref, cos_ref, sin_ref, ones_ref,
               kvc_ref, ks_ref, vs_ref, kw_ref, vw_ref):
    W = KV_WIDTH
    u = _rms(x_ref[...], g_ref[...]).astype(BF16)
    kv = _dot(u, w_ref[...])
    cos_t = _tile_lanes(cos_ref[...], W)
    sin_t = _tile_lanes(sin_ref[...], W)
    ones_bd = ones_ref[...]

    def put_key(j, gain, k_ref):
        k = kv[:, j * W:(j + 1) * W]
        k = k * lax.rsqrt(_head_mean_sq(k, ones_bd) + EPS) * gain
        k = _rope_flat(k, cos_t, sin_t).astype(BF16)
        for g in range(N_KV_GROUPS):
            k_ref[g] = k[:, g * HEAD_DIM:(g + 1) * HEAD_DIM]

    def put_value_t(j, vt_ref):
        v_t = kv[:, j * W:(j + 1) * W].T
        for g in range(N_KV_GROUPS):
            for t in range(v_t.shape[1] // K_TILE):
                vt_ref[g, t, 0:HEAD_DIM, :] = v_t[g * HEAD_DIM:(g + 1) * HEAD_DIM,
                                                  t * K_TILE:(t + 1) * K_TILE].astype(BF16)
                vt_ref[g, t, HEAD_DIM:VT_ROWS, :] = jnp.ones((VT_ROWS - HEAD_DIM, K_TILE), BF16)

    kvc_ref[...] = kv[:, 0:2 * W]
    put_key(2, kn_ref[1:2, :], ks_ref)
    put_value_t(3, vs_ref)
    put_key(4, kn_ref[2:3, :], kw_ref)
    put_value_t(5, vw_ref)


def _rope_tables(pos):
    half = HEAD_DIM // 2
    freqs = jnp.power(ROPE_THETA, -jnp.arange(half, dtype=F32) / half)
    ang = pos.astype(F32)[:, None] * freqs[None, :]
    cos, sin = jnp.cos(ang), jnp.sin(ang)
    cos_t = jnp.concatenate([cos, cos], axis=-1)
    sin_t = jnp.concatenate([-sin, sin], axis=-1)
    return cos_t, sin_t


def _block_diag_ones(width):
    seg = jnp.arange(width) // HEAD_DIM
    return (seg[:, None] == seg[None, :]).astype(BF16)


def _kv_proj(h, kv_norm, kv_w, k_norm, cos128, sin128, ts=512):
    B, S, D = h.shape
    W = KV_WIDTH
    kn = jnp.tile(k_norm, (1, N_KV_GROUPS))
    G = N_KV_GROUPS
    row = lambda width: pl.BlockSpec((None, ts, width), lambda b, s: (b, s, 0))
    tab = pl.BlockSpec((ts, LANES), lambda b, s: (s, 0))
    key_spec = pl.BlockSpec((None, G, ts, HEAD_DIM), lambda b, s: (b, 0, s, 0))
    val_spec = pl.BlockSpec((None, G, ts // K_TILE, VT_ROWS, K_TILE), lambda b, s: (b, 0, s, 0, 0))
    flat = jax.ShapeDtypeStruct((B, S, 2 * W), F32)
    keys = jax.ShapeDtypeStruct((B, G, S, HEAD_DIM), BF16)
    vals = jax.ShapeDtypeStruct((B, G, S // K_TILE, VT_ROWS, K_TILE), BF16)
    return pl.pallas_call(
        _kv_kernel,
        grid=(B, S // ts),
        in_specs=[row(D), _const_spec((1, D)), _const_spec((D, 6 * W)), _const_spec((3, W)),
                  tab, tab, _const_spec((W, W))],
        out_specs=[row(2 * W), key_spec, val_spec, key_spec, val_spec],
        out_shape=[flat, keys, vals, keys, vals],
        compiler_params=_cparams(2),
        name="shared_kv_proj",
    )(h, kv_norm.reshape(1, D), kv_w.astype(BF16), kn, cos128, sin128, _block_diag_ones(W))


N_CHUNK = 128
CHUNK_W = CMP_STRIDE * HEAD_DIM


def _cmp_kernel(xa_ref, xb_ref, pos_ref, w1_ref, b1_ref, w2_ref, b2_ref, kn_ref, cos_ref, sin_ref, o_ref):
    groups = []
    for x_ref in (xa_ref, xb_ref):
        nth = [x_ref[pl.ds(r, N_CHUNK, stride=CMP_STRIDE), :] for r in range(CMP_STRIDE)]
        for j in range(LANES // HEAD_DIM):
            groups.append(jnp.concatenate([t[:, j * HEAD_DIM:(j + 1) * HEAD_DIM] for t in nth], axis=1))
    x = jnp.concatenate(groups, axis=0)
    rows = x.shape[0]
    ya = _dot((x + pos_ref[0:1, :]).astype(BF16), w1_ref[0:CHUNK_W, :])
    yb = _dot((x + pos_ref[1:2, :]).astype(BF16), w1_ref[CHUNK_W:2 * CHUNK_W, :])
    hid = _gelu_tanh(ya + pltpu.roll(yb, rows - 1, 0) + b1_ref[...])
    out = _dot(hid.astype(BF16), w2_ref[...]) + b2_ref[...]

    @pl.when(pl.program_id(0) == 0)
    def _():
        k = _rms(out, kn_ref[...])
        half = HEAD_DIM // 2
        partner = jnp.concatenate([k[:, half:], k[:, :half]], axis=1)
        o_ref[...] = k * cos_ref[...] + partner * sin_ref[...]

    @pl.when(pl.program_id(0) == 1)
    def _():
        o_ref[...] = out


def _compress(kvc, cmp_pos, cmp_w1, cmp_b1, cmp_w2, cmp_b2, k_norm0):
    B, S, _ = kvc.shape
    G = N_KV_GROUPS
    rows = G * N_CHUNK
    pos = cmp_pos.reshape(2, 2, CHUNK_W)
    cmp_last = jnp.arange(N_CHUNK) * CMP_STRIDE + CMP_BLOCK - 1
    cos_t, sin_t = _rope_tables(cmp_last)
    cos_t, sin_t = jnp.tile(cos_t, (G, 1)), jnp.tile(sin_t, (G, 1))
    per_kv = lambda *shape: pl.BlockSpec((None,) + shape, lambda k, b: (k,) + (0,) * len(shape))
    return pl.pallas_call(
        _cmp_kernel,
        grid=(2, B),
        in_specs=[pl.BlockSpec((None, S, LANES), lambda k, b: (b, 0, 2 * k)),
                  pl.BlockSpec((None, S, LANES), lambda k, b: (b, 0, 2 * k + 1)),
                  per_kv(2, CHUNK_W), per_kv(2 * CHUNK_W, CMP_HIDDEN), per_kv(1, CMP_HIDDEN),
                  per_kv(CMP_HIDDEN, HEAD_DIM), per_kv(1, HEAD_DIM),
                  _const_spec((1, HEAD_DIM)), _const_spec((rows, HEAD_DIM)),
                  _const_spec((rows, HEAD_DIM))],
        out_specs=pl.BlockSpec((None, None, rows, HEAD_DIM), lambda k, b: (k, b, 0, 0)),
        out_shape=jax.ShapeDtypeStruct((2, B, rows, HEAD_DIM), F32),
        compiler_params=_cparams(2),
        name="kv_compress",
    )(kvc, kvc, pos, cmp_w1.astype(BF16), cmp_b1.reshape(2, 1, CMP_HIDDEN), cmp_w2.astype(BF16),
      cmp_b2.reshape(2, 1, HEAD_DIM), k_norm0.reshape(1, HEAD_DIM), cos_t, sin_t)


GATE_PAD = LANES
LOG2_E = 1.4426950408889634
Q_SCALE = HEAD_DIM ** -0.5 * LOG2_E


def _q_kernel(x_ref, g_ref, w_ref, gb_ref, qn_ref, cos_ref, sin_ref, ones_ref, q_ref, gate_ref):
    NQ = N_HEADS * HEAD_DIM
    u = _rms(x_ref[...], g_ref[...]).astype(BF16)
    z = _dot(u, w_ref[...])
    gate_ref[...] = _sigmoid(z[:, NQ:] + gb_ref[...]).T[0:3 * N_HEADS, :]
    ones_bd = ones_ref[...]
    W = ones_bd.shape[0]
    cos_t = _tile_lanes(cos_ref[...], W)
    sin_t = _tile_lanes(sin_ref[...], W)
    for c in range(NQ // W):
        q = z[:, c * W:(c + 1) * W]
        q = q * lax.rsqrt(_head_mean_sq(q, ones_bd) + EPS) * qn_ref[...]
        q_tr = (_rope_flat(q, cos_t, sin_t) * Q_SCALE).T
        for t in range(q_tr.shape[1] // Q_TILE):
            q_ref[c, t] = jnp.concatenate(
                [q_tr[j * HEAD_DIM:(j + 1) * HEAD_DIM, t * Q_TILE:(t + 1) * Q_TILE]
                 for j in range(W // HEAD_DIM)], axis=1).astype(BF16)


def _q_proj(h, norm_g, w_in, gate_b, q_norm_g, cos128, sin128, ts=512):
    B, S, D = h.shape
    NQ = N_HEADS * HEAD_DIM
    n_gate = 3 * N_HEADS
    w = jnp.pad(w_in, ((0, 0), (0, GATE_PAD - n_gate))).astype(BF16)
    gb = jnp.pad(gate_b, (0, GATE_PAD - n_gate)).reshape(1, GATE_PAD)
    W = KV_WIDTH
    qn = jnp.tile(q_norm_g, W // HEAD_DIM).reshape(1, W)
    row = lambda width: pl.BlockSpec((None, ts, width), lambda b, s: (b, s, 0))
    tab = pl.BlockSpec((ts, LANES), lambda b, s: (s, 0))
    return pl.pallas_call(
        _q_kernel,
        grid=(B, S // ts),
        in_specs=[row(D), _const_spec((1, D)), _const_spec((D, NQ + GATE_PAD)),
                  _const_spec((1, GATE_PAD)), _const_spec((1, W)), tab, tab, _const_spec((W, W))],
        out_specs=[pl.BlockSpec((None, N_KV_GROUPS, ts // Q_TILE, HEAD_DIM, HEADS_PER_GROUP * Q_TILE),
                                lambda b, s: (b, 0, s, 0, 0)),
                   pl.BlockSpec((None, n_gate, ts), lambda b, s: (b, 0, s))],
        out_shape=[jax.ShapeDtypeStruct((B, N_KV_GROUPS, S // Q_TILE, HEAD_DIM,
                                         HEADS_PER_GROUP * Q_TILE), BF16),
                   jax.ShapeDtypeStruct((B, n_gate, S), F32)],
        compiler_params=_cparams(2),
        name="nsa_q_proj",
    )(h, norm_g.reshape(1, D), w, gb, qn, cos128, sin128, _block_diag_ones(W))


N_SLC = 32
PS_PAD = 8


def _flash_step(g, parts, v_t, m_scr, acc_scr):
    T = Q_TILE
    m_old = m_scr[g]
    m_new, p_all = [], []
    for h in range(HEADS_PER_GROUP):
        hs = slice(h * T, (h + 1) * T)
        sm = [s[:, hs] if bias is None else s[:, hs] + bias for s, bias in parts]
        m_h = m_old[:, hs]
        for x in sm:
            m_h = jnp.maximum(m_h, jnp.max(x, axis=0, keepdims=True))
        m_new.append(m_h)
        p_all.append(jnp.concatenate([jnp.exp2(x - m_h).astype(BF16) for x in sm], axis=0))
    m_new = jnp.concatenate(m_new, axis=1)
    alpha = jnp.exp2(m_old - m_new)
    m_scr[g] = m_new
    acc_scr[g] = alpha * acc_scr[g] + _dot(v_t, jnp.concatenate(p_all, axis=1))


def _attn_kernel(q_ref, gt_ref, kc_ref, vct_ref, ks_ref, vst_ref, kw_ref, vwt_ref, o_ref,
                 sel_scr, ps_scr, m_scr, acc_scr, ot_scr):
    T = Q_TILE
    G = N_KV_GROUPS
    HG = HEADS_PER_GROUP
    NL = HG * T
    qi = pl.program_id(1)
    tpos = qi * T + lax.broadcasted_iota(jnp.int32, (1, T), 1)
    krow = lax.broadcasted_iota(jnp.int32, (K_TILE, T), 0)
    q_t = lambda g: q_ref[g]
    key_tile = lambda k_ref, g, kt: k_ref[g, pl.ds(pl.multiple_of(kt * K_TILE, K_TILE), K_TILE), :]

    def gate(branch, g):
        r0 = branch * N_HEADS + HG * g
        return jnp.concatenate([gt_ref[r0 + h:r0 + h + 1, :] for h in range(HG)], axis=1)

    def emit(g, o_t, first):
        for h in range(HG):
            rows = slice((HG * g + h) * HEAD_DIM, (HG * g + h + 1) * HEAD_DIM)
            piece = o_t[:, h * T:(h + 1) * T]
            ot_scr[rows, :] = piece if first else ot_scr[rows, :] + piece

    def reset_state():
        for g in range(G):
            m_scr[g] = jnp.full((1, NL), NEG, F32)
            acc_scr[g] = jnp.zeros((VT_ROWS, NL), F32)

    def emit_state(branch):
        for g in range(G):
            denom = acc_scr[g, HEAD_DIM:HEAD_DIM + 1, :]
            emit(g, acc_scr[g, 0:HEAD_DIM, :] * (gate(branch, g) / denom), first=False)

    def sweep(k_ref, vt_ref, tiles, biases, scores=None):
        pairs = [range(lo, min(lo + 2, len(tiles))) for lo in range(0, len(tiles), 2)]
        if scores is None:
            scores = {(i, g): _dot(key_tile(k_ref, g, tiles[i]), q_t(g))
                      for pair in pairs for g in range(G) for i in pair}
        for pair in pairs:
            for g in range(G):
                parts = [(scores[i, g], None if biases[i] is None else biases[i](g)) for i in pair]
                v_t = jnp.concatenate([vt_ref[g, tiles[i]] for i in pair], axis=1)
                _flash_step(g, parts, v_t, m_scr, acc_scr)

    def sweep_range(k_ref, vt_ref, n_tiles, bias_of):
        def run(tiles):
            sweep(k_ref, vt_ref, tiles,
                  [None if bias_of is None else functools.partial(bias_of, kt=kt) for kt in tiles])

        def quad(j, c):
            run([4 * j + i for i in range(4)])
            return c

        lax.fori_loop(0, n_tiles // 4, quad, 0)
        rem = n_tiles % 4
        base = n_tiles - rem

        @pl.when(rem >= 2)
        def _():
            run([base, base + 1])

        @pl.when(rem % 2 == 1)
        def _():
            run([n_tiles - 1])

    n_back = WINDOW // K_TILE

    def causal_bias(g):
        return jnp.where((qi * K_TILE + krow) <= tpos, 0.0, NEG)

    def window_edge_bias(g):
        return jnp.where(((qi - n_back) * K_TILE + krow) > (tpos - WINDOW), 0.0, NEG)

    cmp_scores = [_dot(kc_ref[g], q_t(g)) for g in range(G)]
    win_diag_scores = {(0, g): _dot(key_tile(kw_ref, g, qi), q_t(g)) for g in range(G)}
    cvalid = ((krow * CMP_STRIDE + (CMP_BLOCK - 1)) <= tpos) & (krow < N_CHUNK - 1)
    cmp_probs = []
    for g in range(G):
        psum = jnp.zeros((N_CHUNK, T), F32)
        probs = []
        for h in range(HG):
            sm = jnp.where(cvalid, cmp_scores[g][:, h * T:(h + 1) * T], NEG)
            e = jnp.where(cvalid, jnp.exp2(sm - jnp.max(sm, axis=0, keepdims=True)), 0.0)
            den = jnp.sum(e, axis=0, keepdims=True)
            p = e / jnp.where(den > 0.0, den, 1.0)
            psum = psum + p
            probs.append(p.astype(BF16))
        cmp_probs.append(jnp.concatenate(probs, axis=1))
        ps_scr[g, 0:PS_PAD, :] = jnp.zeros((PS_PAD, T), F32)
        ps_scr[g, PS_PAD:PS_PAD + N_CHUNK, :] = psum
    for g in range(G):
        emit(g, gate(0, g) * _dot(vct_ref[g], cmp_probs[g]), first=True)

    reset_state()
    sweep(kw_ref, vwt_ref, [qi], [causal_bias], scores=win_diag_scores)

    @pl.when(qi >= n_back)
    def _():
        sweep(kw_ref, vwt_ref, [qi - 1 - i for i in range(n_back)],
              [None] * (n_back - 1) + [window_edge_bias])

    @pl.when(qi < n_back)
    def _():
        sweep_range(kw_ref, vwt_ref, qi, None)
    emit_state(2)

    all_causal_fit = (qi * T + T - 1) // SLC_BLOCK + 1 <= SLC_TOPK

    @pl.when(all_causal_fit)
    def _():
        for g in range(G):
            sel_scr[g] = jnp.zeros((N_SLC, T), F32)

    @pl.when(jnp.logical_not(all_causal_fit))
    def _():
        ROWS = 8
        jrow = lax.broadcasted_iota(jnp.int32, (N_SLC, T), 0)
        cur = tpos >> 6
        causal_blk = jrow <= cur
        forced = (jrow == 0) | (causal_blk & ((cur - jrow) < N_LOCAL_BLOCKS))
        for g in range(G):
            tap = lambda k: ps_scr[g, pl.ds(PS_PAD + k, N_SLC, stride=4), :]
            imp = 0.5 * tap(-1) + tap(0) + tap(1) + tap(2) + 0.5 * tap(3)
            score = jnp.where(forced, FORCE, jnp.where(causal_blk, imp, NEG))
            parts = [score[r:r + ROWS] for r in range(0, N_SLC, ROWS)]
            ranks = [jnp.zeros((ROWS, T), F32) for _ in parts]
            for j2 in range(N_SLC):
                other = score[j2:j2 + 1, :]
                for i, part in enumerate(parts):
                    r0 = i * ROWS
                    if r0 + ROWS - 1 < j2:
                        beats = other > part
                    elif r0 > j2:
                        beats = other >= part
                    else:
                        beats = (other > part) | ((other == part) & (jrow[r0:r0 + ROWS] > j2))
                    ranks[i] = ranks[i] + jnp.where(beats, 1.0, 0.0)
            rank = jnp.concatenate(ranks, axis=0)
            sel_scr[g] = jnp.where(rank < SLC_TOPK, 0.0, NEG)

    def slc_bias(g, kt):
        half = K_TILE // 2
        top = jnp.broadcast_to(sel_scr[g, pl.ds(2 * kt, 1), :], (half, T))
        bot = jnp.broadcast_to(sel_scr[g, pl.ds(2 * kt + 1, 1), :], (half, T))
        causal = jnp.where((kt * K_TILE + krow) <= tpos, 0.0, NEG)
        return jnp.concatenate([top, bot], axis=0) + causal

    reset_state()
    sweep_range(ks_ref, vst_ref, qi + 1, slc_bias)
    emit_state(1)

    o_ref[...] = ot_scr[...].T.astype(BF16)


def _nsa_attention(q, gates_t, k_cmp, v_cmp, ks, vst, kw, vwt):
    B, _, S, _ = ks.shape
    G, T = N_KV_GROUPS, Q_TILE
    NQ = N_HEADS * HEAD_DIM
    NT = S // K_TILE
    kc = k_cmp.reshape(B, G, N_CHUNK, HEAD_DIM).astype(BF16)
    vct = v_cmp.reshape(B, G, N_CHUNK, HEAD_DIM).transpose(0, 1, 3, 2).astype(BF16)
    per_b = lambda *shape: pl.BlockSpec((None,) + shape, lambda b, i: (b,) + (0,) * len(shape))
    return pl.pallas_call(
        _attn_kernel,
        grid=(B, S // T),
        in_specs=[pl.BlockSpec((None, G, None, HEAD_DIM, HEADS_PER_GROUP * T),
                               lambda b, i: (b, 0, i, 0, 0)),
                  pl.BlockSpec((None, 3 * N_HEADS, T), lambda b, i: (b, 0, i)),
                  per_b(G, N_CHUNK, HEAD_DIM), per_b(G, HEAD_DIM, N_CHUNK),
                  per_b(G, S, HEAD_DIM), per_b(G, NT, VT_ROWS, K_TILE),
                  per_b(G, S, HEAD_DIM), per_b(G, NT, VT_ROWS, K_TILE)],
        out_specs=pl.BlockSpec((None, T, NQ), lambda b, i: (b, i, 0)),
        out_shape=jax.ShapeDtypeStruct((B, S, NQ), BF16),
        scratch_shapes=[pltpu.VMEM((G, N_SLC, T), F32), pltpu.VMEM((G, PS_PAD + N_CHUNK, T), F32),
                        pltpu.VMEM((G, 1, HEADS_PER_GROUP * T), F32),
                        pltpu.VMEM((G, VT_ROWS, HEADS_PER_GROUP * T), F32),
                        pltpu.VMEM((NQ, T), F32)],
        compiler_params=_cparams(2),
        name="nsa_attention",
    )(q, gates_t, kc, vct, ks, vst, kw, vwt)


def kernel(x, a_norm, a_w_in, a_conv_w, a_conv_b, a_gate_w, a_gate_b, a_lambda, a_w_out,
           kv_norm, kv_w, k_norm, cmp_pos, cmp_w1, cmp_b1, cmp_w2, cmp_b2,
           b_norm, b_w_in, b_gate_b, q_norm, b_w_out, f_norm, f_w_in, f_w_out):
    B, S, D = x.shape
    assert D == D_MODEL and S == N_SLC * SLC_BLOCK and S == N_CHUNK * CMP_STRIDE
    n_a = a_norm.shape[0]
    n_b = b_norm.shape[0]
    h = x
    for i in range(n_a):
        h = _recurrent_block(h, a_norm[i], a_w_in[i], a_conv_w[i], a_conv_b[i], a_gate_w[i],
                             a_gate_b[i], a_lambda[i], a_w_out[i])
        h = _swiglu(h, f_norm[i], f_w_in[i], f_w_out[i])

    cos_t, sin_t = _rope_tables(jnp.arange(S))
    cos128, sin128 = jnp.tile(cos_t, (1, 2)), jnp.tile(sin_t, (1, 2))
    kvc, ks, vs, kw, vw = _kv_proj(h, kv_norm, kv_w, k_norm, cos128, sin128)
    cmp = _compress(kvc, cmp_pos, cmp_w1, cmp_b1, cmp_w2, cmp_b2, k_norm[0])
    for j in range(n_b):
        q, gates = _q_proj(h, b_norm[j], b_w_in[j], b_gate_b[j], q_norm[j], cos128, sin128)
        o = _nsa_attention(q, gates, cmp[0], cmp[1], ks, vs, kw, vw)
        layer = n_a + j
        h = _swiglu(h, f_norm[layer], f_w_in[layer], f_w_out[layer], attn=o, w_o=b_w_out[j])
    return h
```

```python
import functools

import jax
import jax.numpy as jnp
from jax import lax
from jax.experimental import pallas as pl
from jax.experimental.pallas import tpu as pltpu

F32 = jnp.float32
BF16 = jnp.bfloat16

D_MODEL = 1024
LRU_WIDTH = D_MODEL
LRU_HEADS = 8
LRU_BLOCK = LRU_WIDTH // LRU_HEADS
CONV_WIDTH = 4
LRU_C = 8.0
HEAD_DIM = 64
N_HEADS = D_MODEL // HEAD_DIM
N_KV_GROUPS = 4
HEADS_PER_GROUP = N_HEADS // N_KV_GROUPS
CMP_BLOCK = 32
CMP_STRIDE = 16
CMP_HIDDEN = 256
SLC_BLOCK = 64
SLC_TOPK = 16
N_LOCAL_BLOCKS = 2
WINDOW = 512
ROPE_THETA = 10000.0
FFN_HIDDEN = 2816
EPS = 1e-6
NEG = -1e30
FORCE = 1e30

LANES = 128
KV_WIDTH = N_KV_GROUPS * HEAD_DIM
Q_TILE = 128
K_TILE = 128
VT_ROWS = HEAD_DIM + 16
VMEM_LIMIT = 56 * 1024 * 1024


def _cparams(n_axes):
    return pltpu.CompilerParams(dimension_semantics=("arbitrary",) * n_axes,
                                vmem_limit_bytes=VMEM_LIMIT)


def _const_spec(shape):
    nd = len(shape)
    return pl.BlockSpec(shape, lambda *_: (0,) * nd, pipeline_mode=pl.Buffered(1))


def _rms(x, g):
    ms = jnp.mean(x * x, axis=-1, keepdims=True)
    return x * lax.rsqrt(ms + EPS) * g


def _sigmoid(x):
    return 1.0 / (1.0 + jnp.exp(-x))


def _gelu_tanh(x):
    c = 0.7978845608028654
    return x * (0.5 * (1.0 + jnp.tanh(c * (x + 0.044715 * (x * x * x)))))


def _dot(a, b):
    return jnp.dot(a, b, preferred_element_type=F32)


def _head_mean_sq(x, ones_bd):
    sq = x * x
    hi = sq.astype(BF16)
    lo = (sq - hi.astype(F32)).astype(BF16)
    return (_dot(hi, ones_bd) + _dot(lo, ones_bd)) * (1.0 / HEAD_DIM)


def _rope_flat(x, cos_t, sin_t):
    width = x.shape[-1]
    lane = lax.broadcasted_iota(jnp.int32, x.shape, 1)
    upper = (lane & (HEAD_DIM // 2)) != 0
    partner = jnp.where(upper, pltpu.roll(x, HEAD_DIM // 2, 1),
                        pltpu.roll(x, width - HEAD_DIM // 2, 1))
    return x * cos_t + partner * sin_t


def _tile_lanes(t, width):
    reps = width // t.shape[-1]
    return t if reps == 1 else jnp.concatenate([t] * reps, axis=1)


SUBLANES = 8


def _scan_rows(a, b, h_in):
    n = a.shape[0]
    a3 = a.reshape(n // SUBLANES, SUBLANES, LANES)
    b3 = b.reshape(n // SUBLANES, SUBLANES, LANES)
    sub = lax.broadcasted_iota(jnp.int32, a3.shape, 1)
    d = 1
    while d < SUBLANES:
        valid = sub >= d
        b3 = jnp.where(valid, a3 * pltpu.roll(b3, d, 1) + b3, b3)
        a3 = jnp.where(valid, a3 * pltpu.roll(a3, d, 1), a3)
        d *= 2
    out = []
    for g in range(n // SUBLANES):
        hg = a3[g] * h_in + b3[g]
        out.append(hg)
        h_in = hg[SUBLANES - 1:SUBLANES, :]
    return jnp.concatenate(out, axis=0)


def _rec_kernel(x_ref, g_ref, win_ref, cw_ref, cb_ref, wg_ref, gb_ref, lam_ref, wout_ref, o_ref,
                xbuf, hcar, yh):
    ts = x_ref.shape[0]
    R = LRU_WIDTH

    @pl.when(pl.program_id(1) == 0)
    def _():
        xbuf[0:8, :] = jnp.zeros((8, R), F32)
        hcar[...] = jnp.zeros_like(hcar)

    x = x_ref[...]
    u = _rms(x, g_ref[...]).astype(BF16)
    z = _dot(u, win_ref[...])
    xbuf[8:8 + ts, :] = z[:, R:]
    for hh in range(LRU_HEADS):
        cs = slice(hh * LRU_BLOCK, (hh + 1) * LRU_BLOCK)
        cw = cw_ref[:, cs]
        xr = cb_ref[:, cs] + xbuf[5:5 + ts, cs] * cw[0:1]
        xr = xr + xbuf[6:6 + ts, cs] * cw[1:2]
        xr = xr + xbuf[7:7 + ts, cs] * cw[2:3]
        xr = xr + xbuf[8:8 + ts, cs] * cw[3:4]
        gates = _dot(xr.astype(BF16), wg_ref[hh])
        gb = gb_ref[:, cs]
        r = _sigmoid(gates[:, :LRU_BLOCK] + gb[0:1])
        i = _sigmoid(gates[:, LRU_BLOCK:] + gb[1:2])
        lam = lam_ref[:, cs]
        softplus_neg = jnp.maximum(-lam, 0.0) + jnp.log1p(jnp.exp(-jnp.abs(lam)))
        log_a = (-LRU_C * softplus_neg) * r
        a = jnp.exp(log_a)
        bterm = jnp.sqrt(1.0 - a * a) * (i * xr)
        hs = _scan_rows(a, bterm, hcar[:, cs])
        hcar[:, cs] = hs[ts - 1:ts, :]
        yh[:, cs] = (_gelu_tanh(z[:, cs]) * hs).astype(BF16)
    xbuf[0:8, :] = xbuf[ts:ts + 8, :]
    o_ref[...] = x + _dot(yh[...], wout_ref[...])


def _recurrent_block(h, norm_g, w_in, conv_w, conv_b, gate_w, gate_b, lam, w_out, ts=256):
    B, S, D = h.shape
    R = LRU_WIDTH
    wg = jnp.concatenate([gate_w[0], gate_w[1]], axis=-1).astype(BF16)
    return pl.pallas_call(
        _rec_kernel,
        grid=(B, S // ts),
        in_specs=[
            pl.BlockSpec((None, ts, D), lambda b, s: (b, s, 0)),
            _const_spec((1, D)),
            _const_spec((D, 2 * R)),
            _const_spec((CONV_WIDTH, R)),
            _const_spec((1, R)),
            _const_spec((LRU_HEADS, LRU_BLOCK, 2 * LRU_BLOCK)),
            _const_spec((2, R)),
            _const_spec((1, R)),
            _const_spec((R, D)),
        ],
        out_specs=pl.BlockSpec((None, ts, D), lambda b, s: (b, s, 0)),
        out_shape=jax.ShapeDtypeStruct((B, S, D), F32),
        scratch_shapes=[pltpu.VMEM((ts + 8, R), F32), pltpu.VMEM((1, R), F32),
                        pltpu.VMEM((ts, R), BF16)],
        compiler_params=_cparams(2),
        name="rglru_block",
    )(h, norm_g.reshape(1, D), w_in.astype(BF16), conv_w, conv_b.reshape(1, R), wg, gate_b,
      lam.reshape(1, R), w_out.astype(BF16))


FFN_CHUNK = FFN_HIDDEN // 2


def _ffn_body(x, g_ref, win_ref, wout_ref, o_ref):
    u = _rms(x, g_ref[...]).astype(BF16)

    acc = x
    for c in range(FFN_HIDDEN // FFN_CHUNK):
        lo = c * FFN_CHUNK
        gate = _dot(u, win_ref[:, lo:lo + FFN_CHUNK])
        up = _dot(u, win_ref[:, FFN_HIDDEN + lo:FFN_HIDDEN + lo + FFN_CHUNK])
        act = ((gate * _sigmoid(gate)) * up).astype(BF16)
        acc = acc + _dot(act, wout_ref[lo:lo + FFN_CHUNK, :])
    o_ref[...] = acc


def _ffn_kernel(x_ref, g_ref, win_ref, wout_ref, o_ref):
    _ffn_body(x_ref[...], g_ref, win_ref, wout_ref, o_ref)


def _proj_ffn_kernel(x_ref, a_ref, wo_ref, g_ref, win_ref, wout_ref, o_ref):
    _ffn_body(x_ref[...] + _dot(a_ref[...], wo_ref[...]), g_ref, win_ref, wout_ref, o_ref)


def _swiglu(h, norm_g, w_in, w_out, attn=None, w_o=None, tm=512):
    B, S, D = h.shape
    M = B * S
    row_spec = pl.BlockSpec((tm, D), lambda i: (i, 0))
    w_specs = [_const_spec((1, D)), _const_spec((D, 2 * FFN_HIDDEN)), _const_spec((FFN_HIDDEN, D))]
    w_args = (norm_g.reshape(1, D), w_in.astype(BF16), w_out.astype(BF16))
    if attn is None:
        kern, specs, args = _ffn_kernel, [row_spec] + w_specs, (h.reshape(M, D),) + w_args
    else:
        kern = _proj_ffn_kernel
        specs = [row_spec, row_spec, _const_spec((D, D))] + w_specs
        args = (h.reshape(M, D), attn.reshape(M, D), w_o.astype(BF16)) + w_args
    out = pl.pallas_call(
        kern,
        grid=(M // tm,),
        in_specs=specs,
        out_specs=row_spec,
        out_shape=jax.ShapeDtypeStruct((M, D), F32),
        compiler_params=_cparams(1),
        name="swiglu_ffn",
    )(*args)
    return out.reshape(B, S, D)


def _kv_kernel(x_ref, g_ref, w_ref, kn_ref, cos_ref, sin_ref, ones_ref,
               kvc_ref, ks_ref, vs_ref, kw_ref, vw_ref):
    W = KV_WIDTH
    u = _rms(x_ref[...], g_ref[...]).astype(BF16)
    kv = _dot(u, w_ref[...])
    cos_t = _tile_lanes(cos_ref[...], W)
    sin_t = _tile_lanes(sin_ref[...], W)
    ones_bd = ones_ref[...]

    def put_key(j, gain, k_ref):
        k = kv[:, j * W:(j + 1) * W]
        k = k * lax.rsqrt(_head_mean_sq(k, ones_bd) + EPS) * gain
        k = _rope_flat(k, cos_t, sin_t).astype(BF16)
        for g in range(N_KV_GROUPS):
            k_ref[g] = k[:, g * HEAD_DIM:(g + 1) * HEAD_DIM]

    def put_value_t(j, vt_ref):
        v_t = kv[:, j * W:(j + 1) * W].T
        for g in range(N_KV_GROUPS):
            for t in range(v_t.shape[1] // K_TILE):
                vt_ref[g, t, 0:HEAD_DIM, :] = v_t[g * HEAD_DIM:(g + 1) * HEAD_DIM,
                                                  t * K_TILE:(t + 1) * K_TILE].astype(BF16)
                vt_ref[g, t, HEAD_DIM:VT_ROWS, :] = jnp.ones((VT_ROWS - HEAD_DIM, K_TILE), BF16)

    kvc_ref[...] = kv[:, 0:2 * W]
    put_key(2, kn_ref[1:2, :], ks_ref)
    put_value_t(3, vs_ref)
    put_key(4, kn_ref[2:3, :], kw_ref)
    put_value_t(5, vw_ref)


def _rope_tables(pos):
    half = HEAD_DIM // 2
    freqs = jnp.power(ROPE_THETA, -jnp.arange(half, dtype=F32) / half)
    ang = pos.astype(F32)[:, None] * freqs[None, :]
    cos, sin = jnp.cos(ang), jnp.sin(ang)
    cos_t = jnp.concatenate([cos, cos], axis=-1)
    sin_t = jnp.concatenate([-sin, sin], axis=-1)
    return cos_t, sin_t


def _block_diag_ones(width):
    seg = jnp.arange(width) // HEAD_DIM
    return (seg[:, None] == seg[None, :]).astype(BF16)


def _kv_proj(h, kv_norm, kv_w, k_norm, cos128, sin128, ts=512):
    B, S, D = h.shape
    W = KV_WIDTH
    kn = jnp.tile(k_norm, (1, N_KV_GROUPS))
    G = N_KV_GROUPS
    row = lambda width: pl.BlockSpec((None, ts, width), lambda b, s: (b, s, 0))
    tab = pl.BlockSpec((ts, LANES), lambda b, s: (s, 0))
    key_spec = pl.BlockSpec((None, G, ts, HEAD_DIM), lambda b, s: (b, 0, s, 0))
    val_spec = pl.BlockSpec((None, G, ts // K_TILE, VT_ROWS, K_TILE), lambda b, s: (b, 0, s, 0, 0))
    flat = jax.ShapeDtypeStruct((B, S, 2 * W), F32)
    keys = jax.ShapeDtypeStruct((B, G, S, HEAD_DIM), BF16)
    vals = jax.ShapeDtypeStruct((B, G, S // K_TILE, VT_ROWS, K_TILE), BF16)
    return pl.pallas_call(
        _kv_kernel,
        grid=(B, S // ts),
        in_specs=[row(D), _const_spec((1, D)), _const_spec((D, 6 * W)), _const_spec((3, W)),
                  tab, tab, _const_spec((W, W))],
        out_specs=[row(2 * W), key_spec, val_spec, key_spec, val_spec],
        out_shape=[flat, keys, vals, keys, vals],
        compiler_params=_cparams(2),
        name="shared_kv_proj",
    )(h, kv_norm.reshape(1, D), kv_w.astype(BF16), kn, cos128, sin128, _block_diag_ones(W))


N_CHUNK = 128
CHUNK_W = CMP_STRIDE * HEAD_DIM


def _cmp_kernel(xa_ref, xb_ref, pos_ref, w1_ref, b1_ref, w2_ref, b2_ref, kn_ref, cos_ref, sin_ref, o_ref):
    groups = []
    for x_ref in (xa_ref, xb_ref):
        nth = [x_ref[pl.ds(r, N_CHUNK, stride=CMP_STRIDE), :] for r in range(CMP_STRIDE)]
        for j in range(LANES // HEAD_DIM):
            groups.append(jnp.concatenate([t[:, j * HEAD_DIM:(j + 1) * HEAD_DIM] for t in nth], axis=1))
    x = jnp.concatenate(groups, axis=0)
    rows = x.shape[0]
    ya = _dot((x + pos_ref[0:1, :]).astype(BF16), w1_ref[0:CHUNK_W, :])
    yb = _dot((x + pos_ref[1:2, :]).astype(BF16), w1_ref[CHUNK_W:2 * CHUNK_W, :])
    hid = _gelu_tanh(ya + pltpu.roll(yb, rows - 1, 0) + b1_ref[...])
    out = _dot(hid.astype(BF16), w2_ref[...]) + b2_ref[...]

    @pl.when(pl.program_id(0) == 0)
    def _():
        k = _rms(out, kn_ref[...])
        half = HEAD_DIM // 2
        partner = jnp.concatenate([k[:, half:], k[:, :half]], axis=1)
        o_ref[...] = k * cos_ref[...] + partner * sin_ref[...]

    @pl.when(pl.program_id(0) == 1)
    def _():
        o_ref[...] = out


def _compress(kvc, cmp_pos, cmp_w1, cmp_b1, cmp_w2, cmp_b2, k_norm0):
    B, S, _ = kvc.shape
    G = N_KV_GROUPS
    rows = G * N_CHUNK
    pos = cmp_pos.reshape(2, 2, CHUNK_W)
    cmp_last = jnp.arange(N_CHUNK) * CMP_STRIDE + CMP_BLOCK - 1
    cos_t, sin_t = _rope_tables(cmp_last)
    cos_t, sin_t = jnp.tile(cos_t, (G, 1)), jnp.tile(sin_t, (G, 1))
    per_kv = lambda *shape: pl.BlockSpec((None,) + shape, lambda k, b: (k,) + (0,) * len(shape))
    return pl.pallas_call(
        _cmp_kernel,
        grid=(2, B),
        in_specs=[pl.BlockSpec((None, S, LANES), lambda k, b: (b, 0, 2 * k)),
                  pl.BlockSpec((None, S, LANES), lambda k, b: (b, 0, 2 * k + 1)),
                  per_kv(2, CHUNK_W), per_kv(2 * CHUNK_W, CMP_HIDDEN), per_kv(1, CMP_HIDDEN),
                  per_kv(CMP_HIDDEN, HEAD_DIM), per_kv(1, HEAD_DIM),
                  _const_spec((1, HEAD_DIM)), _const_spec((rows, HEAD_DIM)),
                  _const_spec((rows, HEAD_DIM))],
        out_specs=pl.BlockSpec((None, None, rows, HEAD_DIM), lambda k, b: (k, b, 0, 0)),
        out_shape=jax.ShapeDtypeStruct((2, B, rows, HEAD_DIM), F32),
        compiler_params=_cparams(2),
        name="kv_compress",
    )(kvc, kvc, pos, cmp_w1.astype(BF16), cmp_b1.reshape(2, 1, CMP_HIDDEN), cmp_w2.astype(BF16),
      cmp_b2.reshape(2, 1, HEAD_DIM), k_norm0.reshape(1, HEAD_DIM), cos_t, sin_t)


GATE_PAD = LANES
LOG2_E = 1.4426950408889634
Q_SCALE = HEAD_DIM ** -0.5 * LOG2_E


def _q_kernel(x_ref, g_ref, w_ref, gb_ref, qn_ref, cos_ref, sin_ref, ones_ref, q_ref, gate_ref):
    NQ = N_HEADS * HEAD_DIM
    u = _rms(x_ref[...], g_ref[...]).astype(BF16)
    z = _dot(u, w_ref[...])
    gate_ref[...] = _sigmoid(z[:, NQ:] + gb_ref[...]).T[0:3 * N_HEADS, :]
    ones_bd = ones_ref[...]
    W = ones_bd.shape[0]
    cos_t = _tile_lanes(cos_ref[...], W)
    sin_t = _tile_lanes(sin_ref[...], W)
    for c in range(NQ // W):
        q = z[:, c * W:(c + 1) * W]
        q = q * lax.rsqrt(_head_mean_sq(q, ones_bd) + EPS) * qn_ref[...]
        q_tr = (_rope_flat(q, cos_t, sin_t) * Q_SCALE).T
        for t in range(q_tr.shape[1] // Q_TILE):
            q_ref[c, t] = jnp.concatenate(
                [q_tr[j * HEAD_DIM:(j + 1) * HEAD_DIM, t * Q_TILE:(t + 1) * Q_TILE]
                 for j in range(W // HEAD_DIM)], axis=1).astype(BF16)


def _q_proj(h, norm_g, w_in, gate_b, q_norm_g, cos128, sin128, ts=512):
    B, S, D = h.shape
    NQ = N_HEADS * HEAD_DIM
    n_gate = 3 * N_HEADS
    w = jnp.pad(w_in, ((0, 0), (0, GATE_PAD - n_gate))).astype(BF16)
    gb = jnp.pad(gate_b, (0, GATE_PAD - n_gate)).reshape(1, GATE_PAD)
    W = KV_WIDTH
    qn = jnp.tile(q_norm_g, W // HEAD_DIM).reshape(1, W)
    row = lambda width: pl.BlockSpec((None, ts, width), lambda b, s: (b, s, 0))
    tab = pl.BlockSpec((ts, LANES), lambda b, s: (s, 0))
    return pl.pallas_call(
        _q_kernel,
        grid=(B, S // ts),
        in_specs=[row(D), _const_spec((1, D)), _const_spec((D, NQ + GATE_PAD)),
                  _const_spec((1, GATE_PAD)), _const_spec((1, W)), tab, tab, _const_spec((W, W))],
        out_specs=[pl.BlockSpec((None, N_KV_GROUPS, ts // Q_TILE, HEAD_DIM, HEADS_PER_GROUP * Q_TILE),
                                lambda b, s: (b, 0, s, 0, 0)),
                   pl.BlockSpec((None, n_gate, ts), lambda b, s: (b, 0, s))],
        out_shape=[jax.ShapeDtypeStruct((B, N_KV_GROUPS, S // Q_TILE, HEAD_DIM,
                                         HEADS_PER_GROUP * Q_TILE), BF16),
                   jax.ShapeDtypeStruct((B, n_gate, S), F32)],
        compiler_params=_cparams(2),
        name="nsa_q_proj",
    )(h, norm_g.reshape(1, D), w, gb, qn, cos128, sin128, _block_diag_ones(W))


N_SLC = 32
PS_PAD = 8
SCORE_PAIRS_AHEAD = 2


def _flash_step(slot, parts, v_t, m_scr, acc_scr):
    T = Q_TILE
    m_old = m_scr[slot]
    m_new, p_all = [], []
    for h in range(HEADS_PER_GROUP):
        hs = slice(h * T, (h + 1) * T)
        sm = [s[:, hs].astype(BF16) if bias is None else s[:, hs].astype(BF16) + bias
              for s, bias in parts]
        m_h = m_old[:, hs]
        for x in sm:
            m_h = jnp.maximum(m_h, jnp.max(x, axis=0, keepdims=True).astype(F32))
        m_new.append(m_h)
        p_all.append(jnp.concatenate([jnp.exp2(x - m_h.astype(BF16)) for x in sm], axis=0))
    m_new = jnp.concatenate(m_new, axis=1)
    alpha = jnp.exp2(m_old - m_new)
    m_scr[slot] = m_new
    acc_scr[slot] = alpha * acc_scr[slot] + _dot(v_t, jnp.concatenate(p_all, axis=1))


def _attn_kernel(q_ref, gt_ref, kc_ref, vct_ref, ks_ref, vst_ref, kw_ref, vwt_ref, o_ref,
                 sel_scr, ps_scr, m_scr, acc_scr, ot_scr):
    T = Q_TILE
    G = N_KV_GROUPS
    HG = HEADS_PER_GROUP
    NL = HG * T
    qi = pl.program_id(1)
    tpos = qi * T + lax.broadcasted_iota(jnp.int32, (1, T), 1)
    krow = lax.broadcasted_iota(jnp.int32, (K_TILE, T), 0)
    q_t = lambda g: q_ref[g]
    key_tile = lambda k_ref, g, kt: k_ref[g, pl.ds(pl.multiple_of(kt * K_TILE, K_TILE), K_TILE), :]

    def gate(branch, g):
        r0 = branch * N_HEADS + HG * g
        return jnp.concatenate([gt_ref[r0 + h:r0 + h + 1, :] for h in range(HG)], axis=1)

    def emit(g, o_t, first):
        for h in range(HG):
            rows = slice((HG * g + h) * HEAD_DIM, (HG * g + h + 1) * HEAD_DIM)
            piece = o_t[:, h * T:(h + 1) * T]
            ot_scr[rows, :] = piece if first else ot_scr[rows, :] + piece

    def reset_state():
        for g in range(G):
            m_scr[g] = jnp.full((1, NL), NEG, F32)
            acc_scr[g] = jnp.zeros((VT_ROWS, NL), F32)

    def emit_state(branch):
        for g in range(G):
            denom = acc_scr[g, HEAD_DIM:HEAD_DIM + 1, :]
            emit(g, acc_scr[g, 0:HEAD_DIM, :] * (gate(branch, g) / denom), first=False)

    def sweep(k_ref, vt_ref, tiles, biases, scores=None):
        pairs = [range(lo, min(lo + 2, len(tiles))) for lo in range(0, len(tiles), 2)]
        scores = dict(scores or {})

        def score_matmuls(pair, g):
            for i in pair:
                if (i, g) not in scores:
                    scores[i, g] = _dot(key_tile(k_ref, g, tiles[i]), q_t(g))

        for pair in pairs[:SCORE_PAIRS_AHEAD]:
            for g in range(G):
                score_matmuls(pair, g)
        for n, pair in enumerate(pairs):
            for g in range(G):
                parts = [(scores[i, g], None if biases[i] is None else biases[i](g)) for i in pair]
                v_t = jnp.concatenate([vt_ref[g, tiles[i]] for i in pair], axis=1)
                _flash_step(g, parts, v_t, m_scr, acc_scr)
                if n + SCORE_PAIRS_AHEAD < len(pairs):
                    score_matmuls(pairs[n + SCORE_PAIRS_AHEAD], g)

    def sweep_range(k_ref, vt_ref, n_tiles, bias_of):
        def run(tiles):
            sweep(k_ref, vt_ref, tiles,
                  [None if bias_of is None else functools.partial(bias_of, kt=kt) for kt in tiles])

        def quad(j, c):
            run([4 * j + i for i in range(4)])
            return c

        lax.fori_loop(0, n_tiles // 4, quad, 0)
        rem = n_tiles % 4
        base = n_tiles - rem

        @pl.when(rem >= 2)
        def _():
            run([base, base + 1])

        @pl.when(rem % 2 == 1)
        def _():
            run([n_tiles - 1])

    n_back = WINDOW // K_TILE

    def causal_bias(g):
        return jnp.where((qi * K_TILE + krow) <= tpos, 0.0, -jnp.inf).astype(BF16)

    def window_edge_bias(g):
        return jnp.where(((qi - n_back) * K_TILE + krow) > (tpos - WINDOW), 0.0,
                         -jnp.inf).astype(BF16)

    cmp_scores = [_dot(kc_ref[g], q_t(g)) for g in range(G)]
    win_diag_scores = {(0, g): _dot(key_tile(kw_ref, g, qi), q_t(g)) for g in range(G)}
    cvalid = ((krow * CMP_STRIDE + (CMP_BLOCK - 1)) <= tpos) & (krow < N_CHUNK - 1)
    cmp_probs = []
    for g in range(G):
        psum = jnp.zeros((N_CHUNK, T), F32)
        probs = []
        for h in range(HG):
            sm = jnp.where(cvalid, cmp_scores[g][:, h * T:(h + 1) * T], NEG)
            e = jnp.where(cvalid, jnp.exp2(sm - jnp.max(sm, axis=0, keepdims=True)), 0.0)
            den = jnp.sum(e, axis=0, keepdims=True)
            p = e / jnp.where(den > 0.0, den, 1.0)
            psum = psum + p
            probs.append(p.astype(BF16))
        cmp_probs.append(jnp.concatenate(probs, axis=1))
        ps_scr[g, 0:PS_PAD, :] = jnp.zeros((PS_PAD, T), F32)
        ps_scr[g, PS_PAD:PS_PAD + N_CHUNK, :] = psum
    for g in range(G):
        emit(g, gate(0, g) * _dot(vct_ref[g], cmp_probs[g]), first=True)

    reset_state()
    sweep(kw_ref, vwt_ref, [qi], [causal_bias], scores=win_diag_scores)

    @pl.when(qi >= n_back)
    def _():
        sweep(kw_ref, vwt_ref, [qi - 1 - i for i in range(n_back)],
              [None] * (n_back - 1) + [window_edge_bias])

    @pl.when(qi < n_back)
    def _():
        sweep_range(kw_ref, vwt_ref, qi, None)
    emit_state(2)

    all_causal_fit = (qi * T + T - 1) // SLC_BLOCK + 1 <= SLC_TOPK

    @pl.when(all_causal_fit)
    def _():
        for g in range(G):
            sel_scr[g] = jnp.zeros((N_SLC, T), F32)

    @pl.when(jnp.logical_not(all_causal_fit))
    def _():
        ROWS = 8
        jrow = lax.broadcasted_iota(jnp.int32, (N_SLC, T), 0)
        cur = tpos >> 6
        causal_blk = jrow <= cur
        forced = (jrow == 0) | (causal_blk & ((cur - jrow) < N_LOCAL_BLOCKS))
        for g in range(G):
            tap = lambda k: ps_scr[g, pl.ds(PS_PAD + k, N_SLC, stride=4), :]
            imp = 0.5 * tap(-1) + tap(0) + tap(1) + tap(2) + 0.5 * tap(3)
            score = jnp.where(forced, FORCE, jnp.where(causal_blk, imp, NEG))
            parts = [score[r:r + ROWS] for r in range(0, N_SLC, ROWS)]
            ranks = [jnp.zeros((ROWS, T), F32) for _ in parts]
            for j2 in range(N_SLC):
                other = score[j2:j2 + 1, :]
                for i, part in enumerate(parts):
                    r0 = i * ROWS
                    if r0 + ROWS - 1 < j2:
                        beats = other > part
                    elif r0 > j2:
                        beats = other >= part
                    else:
                        beats = (other > part) | ((other == part) & (jrow[r0:r0 + ROWS] > j2))
                    ranks[i] = ranks[i] + jnp.where(beats, 1.0, 0.0)
            rank = jnp.concatenate(ranks, axis=0)
            sel_scr[g] = jnp.where(rank < SLC_TOPK, 0.0, -jnp.inf)

    def slc_bias(g, kt):
        half = K_TILE // 2
        top = jnp.broadcast_to(sel_scr[g, pl.ds(2 * kt, 1), :], (half, T))
        bot = jnp.broadcast_to(sel_scr[g, pl.ds(2 * kt + 1, 1), :], (half, T))
        causal = jnp.where((kt * K_TILE + krow) <= tpos, 0.0, -jnp.inf)
        return (jnp.concatenate([top, bot], axis=0) + causal).astype(BF16)

    reset_state()
    sweep_range(ks_ref, vst_ref, qi + 1, slc_bias)
    emit_state(1)

    o_ref[...] = ot_scr[...].T.astype(BF16)


def _nsa_attention(q, gates_t, k_cmp, v_cmp, ks, vst, kw, vwt):
    B, _, S, _ = ks.shape
    G, T = N_KV_GROUPS, Q_TILE
    NQ = N_HEADS * HEAD_DIM
    NT = S // K_TILE
    kc = k_cmp.reshape(B, G, N_CHUNK, HEAD_DIM).astype(BF16)
    vct = v_cmp.reshape(B, G, N_CHUNK, HEAD_DIM).transpose(0, 1, 3, 2).astype(BF16)
    per_b = lambda *shape: pl.BlockSpec((None,) + shape, lambda b, i: (b,) + (0,) * len(shape))
    return pl.pallas_call(
        _attn_kernel,
        grid=(B, S // T),
        in_specs=[pl.BlockSpec((None, G, None, HEAD_DIM, HEADS_PER_GROUP * T),
                               lambda b, i: (b, 0, i, 0, 0)),
                  pl.BlockSpec((None, 3 * N_HEADS, T), lambda b, i: (b, 0, i)),
                  per_b(G, N_CHUNK, HEAD_DIM), per_b(G, HEAD_DIM, N_CHUNK),
                  per_b(G, S, HEAD_DIM), per_b(G, NT, VT_ROWS, K_TILE),
                  per_b(G, S, HEAD_DIM), per_b(G, NT, VT_ROWS, K_TILE)],
        out_specs=pl.BlockSpec((None, T, NQ), lambda b, i: (b, i, 0)),
        out_shape=jax.ShapeDtypeStruct((B, S, NQ), BF16),
        scratch_shapes=[pltpu.VMEM((G, N_SLC, T), F32), pltpu.VMEM((G, PS_PAD + N_CHUNK, T), F32),
                        pltpu.VMEM((G, 1, HEADS_PER_GROUP * T), F32),
                        pltpu.VMEM((G, VT_ROWS, HEADS_PER_GROUP * T), F32),
                        pltpu.VMEM((NQ, T), F32)],
        compiler_params=_cparams(2),
        name="nsa_attention",
    )(q, gates_t, kc, vct, ks, vst, kw, vwt)


def kernel(x, a_norm, a_w_in, a_conv_w, a_conv_b, a_gate_w, a_gate_b, a_lambda, a_w_out,
           kv_norm, kv_w, k_norm, cmp_pos, cmp_w1, cmp_b1, cmp_w2, cmp_b2,
           b_norm, b_w_in, b_gate_b, q_norm, b_w_out, f_norm, f_w_in, f_w_out):
    B, S, D = x.shape
    assert D == D_MODEL and S == N_SLC * SLC_BLOCK and S == N_CHUNK * CMP_STRIDE
    n_a = a_norm.shape[0]
    n_b = b_norm.shape[0]
    h = x
    for i in range(n_a):
        h = _recurrent_block(h, a_norm[i], a_w_in[i], a_conv_w[i], a_conv_b[i], a_gate_w[i],
                             a_gate_b[i], a_lambda[i], a_w_out[i])
        h = _swiglu(h, f_norm[i], f_w_in[i], f_w_out[i])

    cos_t, sin_t = _rope_tables(jnp.arange(S))
    cos128, sin128 = jnp.tile(cos_t, (1, 2)), jnp.tile(sin_t, (1, 2))
    kvc, ks, vs, kw, vw = _kv_proj(h, kv_norm, kv_w, k_norm, cos128, sin128)
    cmp = _compress(kvc, cmp_pos, cmp_w1, cmp_b1, cmp_w2, cmp_b2, k_norm[0])
    for j in range(n_b):
        q, gates = _q_proj(h, b_norm[j], b_w_in[j], b_gate_b[j], q_norm[j], cos128, sin128)
        o = _nsa_attention(q, gates, cmp[0], cmp[1], ks, vs, kw, vw)
        layer = n_a + j
        h = _swiglu(h, f_norm[layer], f_w_in[layer], f_w_out[layer], attn=o, w_o=b_w_out[j])
    return h
```

```python
import functools

import jax
import jax.numpy as jnp
from jax import lax
from jax.experimental import pallas as pl
from jax.experimental.pallas import tpu as pltpu

F32 = jnp.float32
BF16 = jnp.bfloat16

D_MODEL = 1024
LRU_WIDTH = D_MODEL
LRU_HEADS = 8
LRU_BLOCK = LRU_WIDTH // LRU_HEADS
CONV_WIDTH = 4
LRU_C = 8.0
HEAD_DIM = 64
N_HEADS = D_MODEL // HEAD_DIM
N_KV_GROUPS = 4
HEADS_PER_GROUP = N_HEADS // N_KV_GROUPS
CMP_BLOCK = 32
CMP_STRIDE = 16
CMP_HIDDEN = 256
SLC_BLOCK = 64
SLC_TOPK = 16
N_LOCAL_BLOCKS = 2
WINDOW = 512
ROPE_THETA = 10000.0
FFN_HIDDEN = 2816
EPS = 1e-6
NEG = -1e30
FORCE = 1e30

LANES = 128
KV_WIDTH = N_KV_GROUPS * HEAD_DIM
Q_TILE = 128
K_TILE = 128
VT_ROWS = HEAD_DIM + 16
VMEM_LIMIT = 56 * 1024 * 1024


def _cparams(n_axes):
    return pltpu.CompilerParams(dimension_semantics=("arbitrary",) * n_axes,
                                vmem_limit_bytes=VMEM_LIMIT)


def _const_spec(shape):
    nd = len(shape)
    return pl.BlockSpec(shape, lambda *_: (0,) * nd, pipeline_mode=pl.Buffered(1))


def _rms(x, g):
    ms = jnp.mean(x * x, axis=-1, keepdims=True)
    return x * lax.rsqrt(ms + EPS) * g


def _sigmoid(x):
    return 1.0 / (1.0 + jnp.exp(-x))


def _gelu_tanh(x):
    c = 0.7978845608028654
    return x * (0.5 * (1.0 + jnp.tanh(c * (x + 0.044715 * (x * x * x)))))


def _dot(a, b):
    return jnp.dot(a, b, preferred_element_type=F32)


def _head_mean_sq(x, ones_bd):
    sq = x * x
    hi = sq.astype(BF16)
    lo = (sq - hi.astype(F32)).astype(BF16)
    return (_dot(hi, ones_bd) + _dot(lo, ones_bd)) * (1.0 / HEAD_DIM)


def _rope_flat(x, cos_t, sin_t):
    width = x.shape[-1]
    lane = lax.broadcasted_iota(jnp.int32, x.shape, 1)
    upper = (lane & (HEAD_DIM // 2)) != 0
    partner = jnp.where(upper, pltpu.roll(x, HEAD_DIM // 2, 1),
                        pltpu.roll(x, width - HEAD_DIM // 2, 1))
    return x * cos_t + partner * sin_t


def _tile_lanes(t, width):
    reps = width // t.shape[-1]
    return t if reps == 1 else jnp.concatenate([t] * reps, axis=1)


SUBLANES = 8


def _segment_perm(ts):
    seg_len = ts // SUBLANES
    dst = jnp.arange(ts)
    src = (dst % SUBLANES) * seg_len + dst // SUBLANES
    return (src[:, None] == jnp.arange(ts)[None, :]).astype(BF16)


def _scan_segments(a, b, h_in):
    n_steps = a.shape[0] // SUBLANES
    vreg = lambda x, j: x[j * SUBLANES:(j + 1) * SUBLANES]
    h_loc, a_cum = [vreg(b, 0)], [vreg(a, 0)]
    for j in range(1, n_steps):
        h_loc.append(vreg(a, j) * h_loc[-1] + vreg(b, j))
        a_cum.append(vreg(a, j) * a_cum[-1])
    seg_a, seg_h = a_cum[-1], h_loc[-1]
    carry = [h_in]
    for s in range(SUBLANES):
        carry.append(seg_a[s:s + 1] * carry[-1] + seg_h[s:s + 1])
    enter = jnp.concatenate(carry[:SUBLANES], axis=0)
    h = jnp.concatenate([h_loc[j] + a_cum[j] * enter for j in range(n_steps)], axis=0)
    return h, carry[SUBLANES]


def _rec_kernel(x_ref, g_ref, perm_ref, perm_t_ref, win_ref, cw_ref, cb_ref, wg_ref, gb_ref,
                lam_ref, wout_ref, o_ref, tail, hcar):
    R = LRU_WIDTH
    sub_rows = perm_ref.shape[0]
    n_sub = x_ref.shape[0] // sub_rows
    n_steps = sub_rows // SUBLANES
    taps = CONV_WIDTH - 1

    @pl.when(pl.program_id(1) == 0)
    def _():
        tail[...] = jnp.zeros_like(tail)
        hcar[...] = jnp.zeros_like(hcar)

    x = x_ref[...]
    u = _rms(x, g_ref[...]).astype(BF16)
    def in_proj(t):
        ut = _dot(perm_ref[...], u[t * sub_rows:(t + 1) * sub_rows]).astype(BF16)
        return _dot(ut, win_ref[...])

    sub = lax.broadcasted_iota(jnp.int32, (SUBLANES, LRU_BLOCK), 0)
    softplus_neg = jnp.maximum(-lam_ref[...], 0.0) + jnp.log1p(jnp.exp(-jnp.abs(lam_ref[...])))
    log_a_scale = -LRU_C * softplus_neg
    prev_tail = tail[...]
    z_next = in_proj(0)
    for t in range(n_sub):
        z = z_next
        conv_out, gate_pre = [], []
        for hh in range(LRU_HEADS):
            cs = slice(hh * LRU_BLOCK, (hh + 1) * LRU_BLOCK)
            xc = z[:, R + hh * LRU_BLOCK:R + (hh + 1) * LRU_BLOCK]
            wrapped = []
            for k in range(taps):
                cur = xc[(n_steps - taps + k) * SUBLANES:(n_steps - taps + k + 1) * SUBLANES]
                prev = prev_tail[k * SUBLANES:(k + 1) * SUBLANES, cs]
                wrapped.append(pltpu.roll(jnp.where(sub == SUBLANES - 1, prev, cur), 1, 0))
            back = lambda d: jnp.concatenate(
                wrapped[taps - d:] + [xc[0:(n_steps - d) * SUBLANES]], axis=0)
            cw = cw_ref[:, cs]
            xr = cb_ref[:, cs] + back(3) * cw[0:1]
            xr = xr + back(2) * cw[1:2]
            xr = xr + back(1) * cw[2:3]
            xr = xr + xc * cw[3:4]
            conv_out.append(xr)
            gate_pre.append(_dot(xr.astype(BF16), wg_ref[hh]))
        if t + 1 < n_sub:
            z_next = in_proj(t + 1)
        gated = []
        for hh in range(LRU_HEADS):
            cs = slice(hh * LRU_BLOCK, (hh + 1) * LRU_BLOCK)
            xr, gates = conv_out[hh], gate_pre[hh]
            gb = gb_ref[:, cs]
            r = _sigmoid(gates[:, :LRU_BLOCK] + gb[0:1])
            i = _sigmoid(gates[:, LRU_BLOCK:] + gb[1:2])
            log_a = log_a_scale[:, cs] * r
            a = jnp.exp(log_a)
            bterm = jnp.sqrt(1.0 - a * a) * (i * xr)
            hs, hcar[:, cs] = _scan_segments(a, bterm, hcar[:, cs])
            gated.append((_gelu_tanh(z[:, cs]) * hs).astype(BF16))
        prev_tail = z[(n_steps - taps) * SUBLANES:, R:]
        yh = _dot(perm_t_ref[...], jnp.concatenate(gated, axis=1)).astype(BF16)
        rows = slice(t * sub_rows, (t + 1) * sub_rows)
        o_ref[rows, :] = x[rows] + _dot(yh, wout_ref[...])
    tail[...] = prev_tail


def _recurrent_block(h, norm_g, w_in, conv_w, conv_b, gate_w, gate_b, lam, w_out, ts=512,
                     sub_rows=128):
    B, S, D = h.shape
    R = LRU_WIDTH
    wg = jnp.concatenate([gate_w[0], gate_w[1]], axis=-1).astype(BF16)
    perm = _segment_perm(sub_rows)
    return pl.pallas_call(
        _rec_kernel,
        grid=(B, S // ts),
        in_specs=[
            pl.BlockSpec((None, ts, D), lambda b, s: (b, s, 0)),
            _const_spec((1, D)),
            _const_spec((sub_rows, sub_rows)),
            _const_spec((sub_rows, sub_rows)),
            _const_spec((D, 2 * R)),
            _const_spec((CONV_WIDTH, R)),
            _const_spec((1, R)),
            _const_spec((LRU_HEADS, LRU_BLOCK, 2 * LRU_BLOCK)),
            _const_spec((2, R)),
            _const_spec((1, R)),
            _const_spec((R, D)),
        ],
        out_specs=pl.BlockSpec((None, ts, D), lambda b, s: (b, s, 0)),
        out_shape=jax.ShapeDtypeStruct((B, S, D), F32),
        scratch_shapes=[pltpu.VMEM(((CONV_WIDTH - 1) * SUBLANES, R), F32), pltpu.VMEM((1, R), F32)],
        compiler_params=_cparams(2),
        name="rglru_block",
    )(h, norm_g.reshape(1, D), perm, perm.T, w_in.astype(BF16), conv_w, conv_b.reshape(1, R), wg,
      gate_b, lam.reshape(1, R), w_out.astype(BF16))


FFN_CHUNK = FFN_HIDDEN // 2


def _ffn_body(x, g_ref, win_ref, wout_ref, o_ref):
    u = _rms(x, g_ref[...]).astype(BF16)

    acc = x
    for c in range(FFN_HIDDEN // FFN_CHUNK):
        lo = c * FFN_CHUNK
        gate = _dot(u, win_ref[:, lo:lo + FFN_CHUNK])
        up = _dot(u, win_ref[:, FFN_HIDDEN + lo:FFN_HIDDEN + lo + FFN_CHUNK])
        act = ((gate * _sigmoid(gate)) * up).astype(BF16)
        acc = acc + _dot(act, wout_ref[lo:lo + FFN_CHUNK, :])
    o_ref[...] = acc


def _ffn_kernel(x_ref, g_ref, win_ref, wout_ref, o_ref):
    _ffn_body(x_ref[...], g_ref, win_ref, wout_ref, o_ref)


def _proj_ffn_kernel(x_ref, a_ref, wo_ref, g_ref, win_ref, wout_ref, o_ref):
    _ffn_body(x_ref[...] + _dot(a_ref[...], wo_ref[...]), g_ref, win_ref, wout_ref, o_ref)


def _swiglu(h, norm_g, w_in, w_out, attn=None, w_o=None, tm=512):
    B, S, D = h.shape
    M = B * S
    row_spec = pl.BlockSpec((tm, D), lambda i: (i, 0))
    w_specs = [_const_spec((1, D)), _const_spec((D, 2 * FFN_HIDDEN)), _const_spec((FFN_HIDDEN, D))]
    w_args = (norm_g.reshape(1, D), w_in.astype(BF16), w_out.astype(BF16))
    if attn is None:
        kern, specs, args = _ffn_kernel, [row_spec] + w_specs, (h.reshape(M, D),) + w_args
    else:
        kern = _proj_ffn_kernel
        specs = [row_spec, row_spec, _const_spec((D, D))] + w_specs
        args = (h.reshape(M, D), attn.reshape(M, D), w_o.astype(BF16)) + w_args
    out = pl.pallas_call(
        kern,
        grid=(M // tm,),
        in_specs=specs,
        out_specs=row_spec,
        out_shape=jax.ShapeDtypeStruct((M, D), F32),
        compiler_params=_cparams(1),
        name="swiglu_ffn",
    )(*args)
    return out.reshape(B, S, D)


def _kv_kernel(x_ref, g_ref, w_ref, kn_ref, cos_ref, sin_ref, ones_ref,
               kvc_ref, ks_ref, vs_ref, kw_ref, vw_ref):
    W = KV_WIDTH
    u = _rms(x_ref[...], g_ref[...]).astype(BF16)
    kv = _dot(u, w_ref[...])
    cos_t = _tile_lanes(cos_ref[...], W)
    sin_t = _tile_lanes(sin_ref[...], W)
    ones_bd = ones_ref[...]

    def put_key(j, gain, k_ref):
        k = kv[:, j * W:(j + 1) * W]
        k = k * lax.rsqrt(_head_mean_sq(k, ones_bd) + EPS) * gain
        k = _rope_flat(k, cos_t, sin_t).astype(BF16)
        for g in range(N_KV_GROUPS):
            k_ref[g] = k[:, g * HEAD_DIM:(g + 1) * HEAD_DIM]

    def put_value_t(j, vt_ref):
        v_t = kv[:, j * W:(j + 1) * W].T
        for g in range(N_KV_GROUPS):
            for t in range(v_t.shape[1] // K_TILE):
                vt_ref[g, t, 0:HEAD_DIM, :] = v_t[g * HEAD_DIM:(g + 1) * HEAD_DIM,
                                                  t * K_TILE:(t + 1) * K_TILE].astype(BF16)
                vt_ref[g, t, HEAD_DIM:VT_ROWS, :] = jnp.ones((VT_ROWS - HEAD_DIM, K_TILE), BF16)

    kvc_ref[...] = kv[:, 0:2 * W]
    put_key(2, kn_ref[1:2, :], ks_ref)
    put_value_t(3, vs_ref)
    put_key(4, kn_ref[2:3, :], kw_ref)
    put_value_t(5, vw_ref)


def _rope_tables(pos):
    half = HEAD_DIM // 2
    freqs = jnp.power(ROPE_THETA, -jnp.arange(half, dtype=F32) / half)
    ang = pos.astype(F32)[:, None] * freqs[None, :]
    cos, sin = jnp.cos(ang), jnp.sin(ang)
    cos_t = jnp.concatenate([cos, cos], axis=-1)
    sin_t = jnp.concatenate([-sin, sin], axis=-1)
    return cos_t, sin_t


def _block_diag_ones(width):
    seg = jnp.arange(width) // HEAD_DIM
    return (seg[:, None] == seg[None, :]).astype(BF16)


def _kv_proj(h, kv_norm, kv_w, k_norm, cos128, sin128, ts=512):
    B, S, D = h.shape
    W = KV_WIDTH
    kn = jnp.tile(k_norm, (1, N_KV_GROUPS))
    G = N_KV_GROUPS
    row = lambda width: pl.BlockSpec((None, ts, width), lambda b, s: (b, s, 0))
    tab = pl.BlockSpec((ts, LANES), lambda b, s: (s, 0))
    key_spec = pl.BlockSpec((None, G, ts, HEAD_DIM), lambda b, s: (b, 0, s, 0))
    val_spec = pl.BlockSpec((None, G, ts // K_TILE, VT_ROWS, K_TILE), lambda b, s: (b, 0, s, 0, 0))
    flat = jax.ShapeDtypeStruct((B, S, 2 * W), F32)
    keys = jax.ShapeDtypeStruct((B, G, S, HEAD_DIM), BF16)
    vals = jax.ShapeDtypeStruct((B, G, S // K_TILE, VT_ROWS, K_TILE), BF16)
    return pl.pallas_call(
        _kv_kernel,
        grid=(B, S // ts),
        in_specs=[row(D), _const_spec((1, D)), _const_spec((D, 6 * W)), _const_spec((3, W)),
                  tab, tab, _const_spec((W, W))],
        out_specs=[row(2 * W), key_spec, val_spec, key_spec, val_spec],
        out_shape=[flat, keys, vals, keys, vals],
        compiler_params=_cparams(2),
        name="shared_kv_proj",
    )(h, kv_norm.reshape(1, D), kv_w.astype(BF16), kn, cos128, sin128, _block_diag_ones(W))


N_CHUNK = 128
CHUNK_W = CMP_STRIDE * HEAD_DIM


def _cmp_kernel(xa_ref, xb_ref, pos_ref, w1_ref, b1_ref, w2_ref, b2_ref, kn_ref, cos_ref, sin_ref, o_ref):
    groups = []
    for x_ref in (xa_ref, xb_ref):
        nth = [x_ref[pl.ds(r, N_CHUNK, stride=CMP_STRIDE), :] for r in range(CMP_STRIDE)]
        for j in range(LANES // HEAD_DIM):
            groups.append(jnp.concatenate([t[:, j * HEAD_DIM:(j + 1) * HEAD_DIM] for t in nth], axis=1))
    x = jnp.concatenate(groups, axis=0)
    rows = x.shape[0]
    ya = _dot((x + pos_ref[0:1, :]).astype(BF16), w1_ref[0:CHUNK_W, :])
    yb = _dot((x + pos_ref[1:2, :]).astype(BF16), w1_ref[CHUNK_W:2 * CHUNK_W, :])
    hid = _gelu_tanh(ya + pltpu.roll(yb, rows - 1, 0) + b1_ref[...])
    out = _dot(hid.astype(BF16), w2_ref[...]) + b2_ref[...]

    @pl.when(pl.program_id(0) == 0)
    def _():
        k = _rms(out, kn_ref[...])
        half = HEAD_DIM // 2
        partner = jnp.concatenate([k[:, half:], k[:, :half]], axis=1)
        o_ref[...] = k * cos_ref[...] + partner * sin_ref[...]

    @pl.when(pl.program_id(0) == 1)
    def _():
        o_ref[...] = out


def _compress(kvc, cmp_pos, cmp_w1, cmp_b1, cmp_w2, cmp_b2, k_norm0):
    B, S, _ = kvc.shape
    G = N_KV_GROUPS
    rows = G * N_CHUNK
    pos = cmp_pos.reshape(2, 2, CHUNK_W)
    cmp_last = jnp.arange(N_CHUNK) * CMP_STRIDE + CMP_BLOCK - 1
    cos_t, sin_t = _rope_tables(cmp_last)
    cos_t, sin_t = jnp.tile(cos_t, (G, 1)), jnp.tile(sin_t, (G, 1))
    per_kv = lambda *shape: pl.BlockSpec((None,) + shape, lambda k, b: (k,) + (0,) * len(shape))
    return pl.pallas_call(
        _cmp_kernel,
        grid=(2, B),
        in_specs=[pl.BlockSpec((None, S, LANES), lambda k, b: (b, 0, 2 * k)),
                  pl.BlockSpec((None, S, LANES), lambda k, b: (b, 0, 2 * k + 1)),
                  per_kv(2, CHUNK_W), per_kv(2 * CHUNK_W, CMP_HIDDEN), per_kv(1, CMP_HIDDEN),
                  per_kv(CMP_HIDDEN, HEAD_DIM), per_kv(1, HEAD_DIM),
                  _const_spec((1, HEAD_DIM)), _const_spec((rows, HEAD_DIM)),
                  _const_spec((rows, HEAD_DIM))],
        out_specs=pl.BlockSpec((None, None, rows, HEAD_DIM), lambda k, b: (k, b, 0, 0)),
        out_shape=jax.ShapeDtypeStruct((2, B, rows, HEAD_DIM), F32),
        compiler_params=_cparams(2),
        name="kv_compress",
    )(kvc, kvc, pos, cmp_w1.astype(BF16), cmp_b1.reshape(2, 1, CMP_HIDDEN), cmp_w2.astype(BF16),
      cmp_b2.reshape(2, 1, HEAD_DIM), k_norm0.reshape(1, HEAD_DIM), cos_t, sin_t)


GATE_PAD = LANES
LOG2_E = 1.4426950408889634
Q_SCALE = HEAD_DIM ** -0.5 * LOG2_E


def _q_kernel(x_ref, g_ref, w_ref, gb_ref, qn_ref, cos_ref, sin_ref, ones_ref, q_ref, gate_ref):
    NQ = N_HEADS * HEAD_DIM
    u = _rms(x_ref[...], g_ref[...]).astype(BF16)
    z = _dot(u, w_ref[...])
    gate_ref[...] = _sigmoid(z[:, NQ:] + gb_ref[...]).T[0:3 * N_HEADS, :]
    ones_bd = ones_ref[...]
    W = ones_bd.shape[0]
    cos_t = _tile_lanes(cos_ref[...], W)
    sin_t = _tile_lanes(sin_ref[...], W)
    for c in range(NQ // W):
        q = z[:, c * W:(c + 1) * W]
        q = q * lax.rsqrt(_head_mean_sq(q, ones_bd) + EPS) * qn_ref[...]
        q_tr = (_rope_flat(q, cos_t, sin_t) * Q_SCALE).T
        for t in range(q_tr.shape[1] // Q_TILE):
            q_ref[c, t] = jnp.concatenate(
                [q_tr[j * HEAD_DIM:(j + 1) * HEAD_DIM, t * Q_TILE:(t + 1) * Q_TILE]
                 for j in range(W // HEAD_DIM)], axis=1).astype(BF16)


def _q_proj(h, norm_g, w_in, gate_b, q_norm_g, cos128, sin128, ts=512):
    B, S, D = h.shape
    NQ = N_HEADS * HEAD_DIM
    n_gate = 3 * N_HEADS
    w = jnp.pad(w_in, ((0, 0), (0, GATE_PAD - n_gate))).astype(BF16)
    gb = jnp.pad(gate_b, (0, GATE_PAD - n_gate)).reshape(1, GATE_PAD)
    W = KV_WIDTH
    qn = jnp.tile(q_norm_g, W // HEAD_DIM).reshape(1, W)
    row = lambda width: pl.BlockSpec((None, ts, width), lambda b, s: (b, s, 0))
    tab = pl.BlockSpec((ts, LANES), lambda b, s: (s, 0))
    return pl.pallas_call(
        _q_kernel,
        grid=(B, S // ts),
        in_specs=[row(D), _const_spec((1, D)), _const_spec((D, NQ + GATE_PAD)),
                  _const_spec((1, GATE_PAD)), _const_spec((1, W)), tab, tab, _const_spec((W, W))],
        out_specs=[pl.BlockSpec((None, N_KV_GROUPS, ts // Q_TILE, HEAD_DIM, HEADS_PER_GROUP * Q_TILE),
                                lambda b, s: (b, 0, s, 0, 0)),
                   pl.BlockSpec((None, n_gate, ts), lambda b, s: (b, 0, s))],
        out_shape=[jax.ShapeDtypeStruct((B, N_KV_GROUPS, S // Q_TILE, HEAD_DIM,
                                         HEADS_PER_GROUP * Q_TILE), BF16),
                   jax.ShapeDtypeStruct((B, n_gate, S), F32)],
        compiler_params=_cparams(2),
        name="nsa_q_proj",
    )(h, norm_g.reshape(1, D), w, gb, qn, cos128, sin128, _block_diag_ones(W))


N_SLC = 32
PS_PAD = 8
SCORE_PAIRS_AHEAD = 2


def _flash_step(slot, parts, v_t, m_scr, acc_scr):
    T = Q_TILE
    m_old = m_scr[slot]
    m_new, p_all = [], []
    for h in range(HEADS_PER_GROUP):
        hs = slice(h * T, (h + 1) * T)
        sm = [s[:, hs].astype(BF16) if bias is None else s[:, hs].astype(BF16) + bias
              for s, bias in parts]
        m_h = m_old[:, hs]
        for x in sm:
            m_h = jnp.maximum(m_h, jnp.max(x, axis=0, keepdims=True).astype(F32))
        m_new.append(m_h)
        p_all.append(jnp.concatenate([jnp.exp2(x - m_h.astype(BF16)) for x in sm], axis=0))
    m_new = jnp.concatenate(m_new, axis=1)
    alpha = jnp.exp2(m_old - m_new)
    m_scr[slot] = m_new
    acc_scr[slot] = alpha * acc_scr[slot] + _dot(v_t, jnp.concatenate(p_all, axis=1))


def _attn_kernel(q_ref, gt_ref, kc_ref, vct_ref, ks_ref, vst_ref, kw_ref, vwt_ref, o_ref,
                 sel_scr, ps_scr, m_scr, acc_scr, ot_scr):
    T = Q_TILE
    G = N_KV_GROUPS
    HG = HEADS_PER_GROUP
    NL = HG * T
    qi = pl.program_id(1)
    tpos = qi * T + lax.broadcasted_iota(jnp.int32, (1, T), 1)
    krow = lax.broadcasted_iota(jnp.int32, (K_TILE, T), 0)
    q_t = lambda g: q_ref[g]
    key_tile = lambda k_ref, g, kt: k_ref[g, pl.ds(pl.multiple_of(kt * K_TILE, K_TILE), K_TILE), :]

    def gate(branch, g):
        r0 = branch * N_HEADS + HG * g
        return jnp.concatenate([gt_ref[r0 + h:r0 + h + 1, :] for h in range(HG)], axis=1)

    def emit(g, o_t, first):
        for h in range(HG):
            rows = slice((HG * g + h) * HEAD_DIM, (HG * g + h + 1) * HEAD_DIM)
            piece = o_t[:, h * T:(h + 1) * T]
            ot_scr[rows, :] = piece if first else ot_scr[rows, :] + piece

    def reset_state():
        for g in range(G):
            m_scr[g] = jnp.full((1, NL), NEG, F32)
            acc_scr[g] = jnp.zeros((VT_ROWS, NL), F32)

    def emit_state(branch):
        for g in range(G):
            denom = acc_scr[g, HEAD_DIM:HEAD_DIM + 1, :]
            emit(g, acc_scr[g, 0:HEAD_DIM, :] * (gate(branch, g) / denom), first=False)

    def sweep(k_ref, vt_ref, tiles, biases, scores=None):
        pairs = [range(lo, min(lo + 2, len(tiles))) for lo in range(0, len(tiles), 2)]
        scores = dict(scores or {})

        def score_matmuls(pair, g):
            for i in pair:
                if (i, g) not in scores:
                    scores[i, g] = _dot(key_tile(k_ref, g, tiles[i]), q_t(g))

        for pair in pairs[:SCORE_PAIRS_AHEAD]:
            for g in range(G):
                score_matmuls(pair, g)
        for n, pair in enumerate(pairs):
            for g in range(G):
                parts = [(scores[i, g], None if biases[i] is None else biases[i](g)) for i in pair]
                v_t = jnp.concatenate([vt_ref[g, tiles[i]] for i in pair], axis=1)
                _flash_step(g, parts, v_t, m_scr, acc_scr)
                if n + SCORE_PAIRS_AHEAD < len(pairs):
                    score_matmuls(pairs[n + SCORE_PAIRS_AHEAD], g)

    def sweep_range(k_ref, vt_ref, n_tiles, bias_of):
        def run(tiles):
            sweep(k_ref, vt_ref, tiles,
                  [None if bias_of is None else functools.partial(bias_of, kt=kt) for kt in tiles])

        def quad(j, c):
            run([4 * j + i for i in range(4)])
            return c

        lax.fori_loop(0, n_tiles // 4, quad, 0)
        rem = n_tiles % 4
        base = n_tiles - rem

        @pl.when(rem >= 2)
        def _():
            run([base, base + 1])

        @pl.when(rem % 2 == 1)
        def _():
            run([n_tiles - 1])

    n_back = WINDOW // K_TILE

    def causal_bias(g):
        return jnp.where((qi * K_TILE + krow) <= tpos, 0.0, -jnp.inf).astype(BF16)

    def window_edge_bias(g):
        return jnp.where(((qi - n_back) * K_TILE + krow) > (tpos - WINDOW), 0.0,
                         -jnp.inf).astype(BF16)

    cmp_scores = [_dot(kc_ref[g], q_t(g)) for g in range(G)]
    win_diag_scores = {(0, g): _dot(key_tile(kw_ref, g, qi), q_t(g)) for g in range(G)}
    cvalid = ((krow * CMP_STRIDE + (CMP_BLOCK - 1)) <= tpos) & (krow < N_CHUNK - 1)
    cmp_probs = []
    for g in range(G):
        psum = jnp.zeros((N_CHUNK, T), F32)
        probs = []
        for h in range(HG):
            sm = jnp.where(cvalid, cmp_scores[g][:, h * T:(h + 1) * T], NEG)
            e = jnp.where(cvalid, jnp.exp2(sm - jnp.max(sm, axis=0, keepdims=True)), 0.0)
            den = jnp.sum(e, axis=0, keepdims=True)
            p = e / jnp.where(den > 0.0, den, 1.0)
            psum = psum + p
            probs.append(p.astype(BF16))
        cmp_probs.append(jnp.concatenate(probs, axis=1))
        ps_scr[g, 0:PS_PAD, :] = jnp.zeros((PS_PAD, T), F32)
        ps_scr[g, PS_PAD:PS_PAD + N_CHUNK, :] = psum
    for g in range(G):
        emit(g, gate(0, g) * _dot(vct_ref[g], cmp_probs[g]), first=True)

    reset_state()
    sweep(kw_ref, vwt_ref, [qi], [causal_bias], scores=win_diag_scores)

    @pl.when(qi >= n_back)
    def _():
        sweep(kw_ref, vwt_ref, [qi - 1 - i for i in range(n_back)],
              [None] * (n_back - 1) + [window_edge_bias])

    @pl.when(qi < n_back)
    def _():
        sweep_range(kw_ref, vwt_ref, qi, None)
    emit_state(2)

    all_causal_fit = (qi * T + T - 1) // SLC_BLOCK + 1 <= SLC_TOPK

    @pl.when(all_causal_fit)
    def _():
        for g in range(G):
            sel_scr[g] = jnp.zeros((N_SLC, T), F32)

    @pl.when(jnp.logical_not(all_causal_fit))
    def _():
        ROWS = 8
        jrow = lax.broadcasted_iota(jnp.int32, (N_SLC, T), 0)
        cur = tpos >> 6
        causal_blk = jrow <= cur
        forced = (jrow == 0) | (causal_blk & ((cur - jrow) < N_LOCAL_BLOCKS))
        for g in range(G):
            tap = lambda k: ps_scr[g, pl.ds(PS_PAD + k, N_SLC, stride=4), :]
            imp = 0.5 * tap(-1) + tap(0) + tap(1) + tap(2) + 0.5 * tap(3)
            score = jnp.where(forced, FORCE, jnp.where(causal_blk, imp, NEG))
            parts = [score[r:r + ROWS] for r in range(0, N_SLC, ROWS)]
            ranks = [jnp.zeros((ROWS, T), F32) for _ in parts]
            for j2 in range(N_SLC):
                other = score[j2:j2 + 1, :]
                for i, part in enumerate(parts):
                    r0 = i * ROWS
                    if r0 + ROWS - 1 < j2:
                        beats = other > part
                    elif r0 > j2:
                        beats = other >= part
                    else:
                        beats = (other > part) | ((other == part) & (jrow[r0:r0 + ROWS] > j2))
                    ranks[i] = ranks[i] + jnp.where(beats, 1.0, 0.0)
            rank = jnp.concatenate(ranks, axis=0)
            sel_scr[g] = jnp.where(rank < SLC_TOPK, 0.0, -jnp.inf)

    def slc_bias(g, kt):
        half = K_TILE // 2
        top = jnp.broadcast_to(sel_scr[g, pl.ds(2 * kt, 1), :], (half, T))
        bot = jnp.broadcast_to(sel_scr[g, pl.ds(2 * kt + 1, 1), :], (half, T))
        causal = jnp.where((kt * K_TILE + krow) <= tpos, 0.0, -jnp.inf)
        return (jnp.concatenate([top, bot], axis=0) + causal).astype(BF16)

    reset_state()
    sweep_range(ks_ref, vst_ref, qi + 1, slc_bias)
    emit_state(1)

    o_ref[...] = ot_scr[...].T.astype(BF16)


def _nsa_attention(q, gates_t, k_cmp, v_cmp, ks, vst, kw, vwt):
    B, _, S, _ = ks.shape
    G, T = N_KV_GROUPS, Q_TILE
    NQ = N_HEADS * HEAD_DIM
    NT = S // K_TILE
    kc = k_cmp.reshape(B, G, N_CHUNK, HEAD_DIM).astype(BF16)
    vct = v_cmp.reshape(B, G, N_CHUNK, HEAD_DIM).transpose(0, 1, 3, 2).astype(BF16)
    per_b = lambda *shape: pl.BlockSpec((None,) + shape, lambda b, i: (b,) + (0,) * len(shape))
    return pl.pallas_call(
        _attn_kernel,
        grid=(B, S // T),
        in_specs=[pl.BlockSpec((None, G, None, HEAD_DIM, HEADS_PER_GROUP * T),
                               lambda b, i: (b, 0, i, 0, 0)),
                  pl.BlockSpec((None, 3 * N_HEADS, T), lambda b, i: (b, 0, i)),
                  per_b(G, N_CHUNK, HEAD_DIM), per_b(G, HEAD_DIM, N_CHUNK),
                  per_b(G, S, HEAD_DIM), per_b(G, NT, VT_ROWS, K_TILE),
                  per_b(G, S, HEAD_DIM), per_b(G, NT, VT_ROWS, K_TILE)],
        out_specs=pl.BlockSpec((None, T, NQ), lambda b, i: (b, i, 0)),
        out_shape=jax.ShapeDtypeStruct((B, S, NQ), BF16),
        scratch_shapes=[pltpu.VMEM((G, N_SLC, T), F32), pltpu.VMEM((G, PS_PAD + N_CHUNK, T), F32),
                        pltpu.VMEM((G, 1, HEADS_PER_GROUP * T), F32),
                        pltpu.VMEM((G, VT_ROWS, HEADS_PER_GROUP * T), F32),
                        pltpu.VMEM((NQ, T), F32)],
        compiler_params=_cparams(2),
        name="nsa_attention",
    )(q, gates_t, kc, vct, ks, vst, kw, vwt)


def kernel(x, a_norm, a_w_in, a_conv_w, a_conv_b, a_gate_w, a_gate_b, a_lambda, a_w_out,
           kv_norm, kv_w, k_norm, cmp_pos, cmp_w1, cmp_b1, cmp_w2, cmp_b2,
           b_norm, b_w_in, b_gate_b, q_norm, b_w_out, f_norm, f_w_in, f_w_out):
    B, S, D = x.shape
    assert D == D_MODEL and S == N_SLC * SLC_BLOCK and S == N_CHUNK * CMP_STRIDE
    n_a = a_norm.shape[0]
    n_b = b_norm.shape[0]
    h = x
    for i in range(n_a):
        h = _recurrent_block(h, a_norm[i], a_w_in[i], a_conv_w[i], a_conv_b[i], a_gate_w[i],
                             a_gate_b[i], a_lambda[i], a_w_out[i])
        h = _swiglu(h, f_norm[i], f_w_in[i], f_w_out[i])

    cos_t, sin_t = _rope_tables(jnp.arange(S))
    cos128, sin128 = jnp.tile(cos_t, (1, 2)), jnp.tile(sin_t, (1, 2))
    kvc, ks, vs, kw, vw = _kv_proj(h, kv_norm, kv_w, k_norm, cos128, sin128)
    cmp = _compress(kvc, cmp_pos, cmp_w1, cmp_b1, cmp_w2, cmp_b2, k_norm[0])
    for j in range(n_b):
        q, gates = _q_proj(h, b_norm[j], b_w_in[j], b_gate_b[j], q_norm[j], cos128, sin128)
        o = _nsa_attention(q, gates, cmp[0], cmp[1], ks, vs, kw, vw)
        layer = n_a + j
        h = _swiglu(h, f_norm[layer], f_w_in[layer], f_w_out[layer], attn=o, w_o=b_w_out[j])
    return h
```

```python
import functools

import jax
import jax.numpy as jnp
from jax import lax
from jax.experimental import pallas as pl
from jax.experimental.pallas import tpu as pltpu

F32 = jnp.float32
BF16 = jnp.bfloat16

D_MODEL = 1024
LRU_WIDTH = D_MODEL
LRU_HEADS = 8
LRU_BLOCK = LRU_WIDTH // LRU_HEADS
CONV_WIDTH = 4
LRU_C = 8.0
HEAD_DIM = 64
N_HEADS = D_MODEL // HEAD_DIM
N_KV_GROUPS = 4
HEADS_PER_GROUP = N_HEADS // N_KV_GROUPS
CMP_BLOCK = 32
CMP_STRIDE = 16
CMP_HIDDEN = 256
SLC_BLOCK = 64
SLC_TOPK = 16
N_LOCAL_BLOCKS = 2
WINDOW = 512
ROPE_THETA = 10000.0
FFN_HIDDEN = 2816
EPS = 1e-6
NEG = -1e30
FORCE = 1e30

LANES = 128
KV_WIDTH = N_KV_GROUPS * HEAD_DIM
Q_TILE = 128
K_TILE = 128
VT_ROWS = HEAD_DIM + 16
VMEM_LIMIT = 56 * 1024 * 1024


def _cparams(n_axes):
    return pltpu.CompilerParams(dimension_semantics=("arbitrary",) * n_axes,
                                vmem_limit_bytes=VMEM_LIMIT)


def _const_spec(shape):
    nd = len(shape)
    return pl.BlockSpec(shape, lambda *_: (0,) * nd, pipeline_mode=pl.Buffered(1))


def _rms(x, g):
    ms = jnp.mean(x * x, axis=-1, keepdims=True)
    return x * lax.rsqrt(ms + EPS) * g


def _sigmoid(x):
    return 1.0 / (1.0 + jnp.exp(-x))


def _gelu_tanh(x):
    c = 0.7978845608028654
    return x * (0.5 * (1.0 + jnp.tanh(c * (x + 0.044715 * (x * x * x)))))


def _dot(a, b):
    return jnp.dot(a, b, preferred_element_type=F32)


def _head_mean_sq(x, ones_bd):
    sq = x * x
    hi = sq.astype(BF16)
    lo = (sq - hi.astype(F32)).astype(BF16)
    return (_dot(hi, ones_bd) + _dot(lo, ones_bd)) * (1.0 / HEAD_DIM)


def _rope_flat(x, cos_t, sin_t):
    width = x.shape[-1]
    lane = lax.broadcasted_iota(jnp.int32, x.shape, 1)
    upper = (lane & (HEAD_DIM // 2)) != 0
    partner = jnp.where(upper, pltpu.roll(x, HEAD_DIM // 2, 1),
                        pltpu.roll(x, width - HEAD_DIM // 2, 1))
    return x * cos_t + partner * sin_t


def _tile_lanes(t, width):
    reps = width // t.shape[-1]
    return t if reps == 1 else jnp.concatenate([t] * reps, axis=1)


SUBLANES = 8


def _segment_perm(ts):
    seg_len = ts // SUBLANES
    dst = jnp.arange(ts)
    src = (dst % SUBLANES) * seg_len + dst // SUBLANES
    return (src[:, None] == jnp.arange(ts)[None, :]).astype(BF16)


def _scan_segments(a, b, h_in):
    n_steps = a.shape[0] // SUBLANES
    vreg = lambda x, j: x[j * SUBLANES:(j + 1) * SUBLANES]
    h_loc, a_cum = [vreg(b, 0)], [vreg(a, 0)]
    for j in range(1, n_steps):
        h_loc.append(vreg(a, j) * h_loc[-1] + vreg(b, j))
        a_cum.append(vreg(a, j) * a_cum[-1])
    seg_a, seg_h = a_cum[-1], h_loc[-1]
    carry = [h_in]
    for s in range(SUBLANES):
        carry.append(seg_a[s:s + 1] * carry[-1] + seg_h[s:s + 1])
    enter = jnp.concatenate(carry[:SUBLANES], axis=0)
    h = jnp.concatenate([h_loc[j] + a_cum[j] * enter for j in range(n_steps)], axis=0)
    return h, carry[SUBLANES]


def _rec_kernel(x_ref, g_ref, perm_ref, perm_t_ref, win_ref, cw_ref, cb_ref, wg_ref, gb_ref,
                lam_ref, wout_ref, o_ref, tail, hcar):
    R = LRU_WIDTH
    sub_rows = perm_ref.shape[0]
    n_sub = x_ref.shape[0] // sub_rows
    n_steps = sub_rows // SUBLANES
    taps = CONV_WIDTH - 1

    @pl.when(pl.program_id(1) == 0)
    def _():
        tail[...] = jnp.zeros_like(tail)
        hcar[...] = jnp.zeros_like(hcar)

    x = x_ref[...]
    u = _rms(x, g_ref[...]).astype(BF16)
    def in_proj(t):
        ut = _dot(perm_ref[...], u[t * sub_rows:(t + 1) * sub_rows]).astype(BF16)
        return _dot(ut, win_ref[...])

    sub = lax.broadcasted_iota(jnp.int32, (SUBLANES, LRU_BLOCK), 0)
    softplus_neg = jnp.maximum(-lam_ref[...], 0.0) + jnp.log1p(jnp.exp(-jnp.abs(lam_ref[...])))
    log_a_scale = -LRU_C * softplus_neg
    prev_tail = tail[...]
    z_next = in_proj(0)
    for t in range(n_sub):
        z = z_next
        conv_out, gate_pre = [], []
        for hh in range(LRU_HEADS):
            cs = slice(hh * LRU_BLOCK, (hh + 1) * LRU_BLOCK)
            xc = z[:, R + hh * LRU_BLOCK:R + (hh + 1) * LRU_BLOCK]
            wrapped = []
            for k in range(taps):
                cur = xc[(n_steps - taps + k) * SUBLANES:(n_steps - taps + k + 1) * SUBLANES]
                prev = prev_tail[k * SUBLANES:(k + 1) * SUBLANES, cs]
                wrapped.append(pltpu.roll(jnp.where(sub == SUBLANES - 1, prev, cur), 1, 0))
            back = lambda d: jnp.concatenate(
                wrapped[taps - d:] + [xc[0:(n_steps - d) * SUBLANES]], axis=0)
            cw = cw_ref[:, cs]
            xr = cb_ref[:, cs] + back(3) * cw[0:1]
            xr = xr + back(2) * cw[1:2]
            xr = xr + back(1) * cw[2:3]
            xr = xr + xc * cw[3:4]
            conv_out.append(xr)
            gate_pre.append(_dot(xr.astype(BF16), wg_ref[hh]))
        if t + 1 < n_sub:
            z_next = in_proj(t + 1)
        gated = []
        for hh in range(LRU_HEADS):
            cs = slice(hh * LRU_BLOCK, (hh + 1) * LRU_BLOCK)
            xr, gates = conv_out[hh], gate_pre[hh]
            gb = gb_ref[:, cs]
            r = _sigmoid(gates[:, :LRU_BLOCK] + gb[0:1])
            i = _sigmoid(gates[:, LRU_BLOCK:] + gb[1:2])
            log_a = log_a_scale[:, cs] * r
            a = jnp.exp(log_a)
            bterm = jnp.sqrt(1.0 - a * a) * (i * xr)
            hs, hcar[:, cs] = _scan_segments(a, bterm, hcar[:, cs])
            gated.append((_gelu_tanh(z[:, cs]) * hs).astype(BF16))
        prev_tail = z[(n_steps - taps) * SUBLANES:, R:]
        yh = _dot(perm_t_ref[...], jnp.concatenate(gated, axis=1)).astype(BF16)
        rows = slice(t * sub_rows, (t + 1) * sub_rows)
        o_ref[rows, :] = x[rows] + _dot(yh, wout_ref[...])
    tail[...] = prev_tail


def _recurrent_block(h, norm_g, w_in, conv_w, conv_b, gate_w, gate_b, lam, w_out, ts=512,
                     sub_rows=128):
    B, S, D = h.shape
    R = LRU_WIDTH
    wg = jnp.concatenate([gate_w[0], gate_w[1]], axis=-1).astype(BF16)
    perm = _segment_perm(sub_rows)
    return pl.pallas_call(
        _rec_kernel,
        grid=(B, S // ts),
        in_specs=[
            pl.BlockSpec((None, ts, D), lambda b, s: (b, s, 0)),
            _const_spec((1, D)),
            _const_spec((sub_rows, sub_rows)),
            _const_spec((sub_rows, sub_rows)),
            _const_spec((D, 2 * R)),
            _const_spec((CONV_WIDTH, R)),
            _const_spec((1, R)),
            _const_spec((LRU_HEADS, LRU_BLOCK, 2 * LRU_BLOCK)),
            _const_spec((2, R)),
            _const_spec((1, R)),
            _const_spec((R, D)),
        ],
        out_specs=pl.BlockSpec((None, ts, D), lambda b, s: (b, s, 0)),
        out_shape=jax.ShapeDtypeStruct((B, S, D), F32),
        scratch_shapes=[pltpu.VMEM(((CONV_WIDTH - 1) * SUBLANES, R), F32), pltpu.VMEM((1, R), F32)],
        compiler_params=_cparams(2),
        name="rglru_block",
    )(h, norm_g.reshape(1, D), perm, perm.T, w_in.astype(BF16), conv_w, conv_b.reshape(1, R), wg,
      gate_b, lam.reshape(1, R), w_out.astype(BF16))


FFN_CHUNK = FFN_HIDDEN // 2


def _ffn_body(x, g_ref, win_ref, wout_ref, o_ref):
    u = _rms(x, g_ref[...]).astype(BF16)

    acc = x
    for c in range(FFN_HIDDEN // FFN_CHUNK):
        lo = c * FFN_CHUNK
        gate = _dot(u, win_ref[:, lo:lo + FFN_CHUNK])
        up = _dot(u, win_ref[:, FFN_HIDDEN + lo:FFN_HIDDEN + lo + FFN_CHUNK])
        act = ((gate * _sigmoid(gate)) * up).astype(BF16)
        acc = acc + _dot(act, wout_ref[lo:lo + FFN_CHUNK, :])
    o_ref[...] = acc


def _ffn_kernel(x_ref, g_ref, win_ref, wout_ref, o_ref):
    _ffn_body(x_ref[...], g_ref, win_ref, wout_ref, o_ref)


def _proj_ffn_kernel(x_ref, a_ref, wo_ref, g_ref, win_ref, wout_ref, o_ref):
    _ffn_body(x_ref[...] + _dot(a_ref[...], wo_ref[...]), g_ref, win_ref, wout_ref, o_ref)


def _swiglu(h, norm_g, w_in, w_out, attn=None, w_o=None, tm=512):
    B, S, D = h.shape
    M = B * S
    row_spec = pl.BlockSpec((tm, D), lambda i: (i, 0))
    w_specs = [_const_spec((1, D)), _const_spec((D, 2 * FFN_HIDDEN)), _const_spec((FFN_HIDDEN, D))]
    w_args = (norm_g.reshape(1, D), w_in.astype(BF16), w_out.astype(BF16))
    if attn is None:
        kern, specs, args = _ffn_kernel, [row_spec] + w_specs, (h.reshape(M, D),) + w_args
    else:
        kern = _proj_ffn_kernel
        specs = [row_spec, row_spec, _const_spec((D, D))] + w_specs
        args = (h.reshape(M, D), attn.reshape(M, D), w_o.astype(BF16)) + w_args
    out = pl.pallas_call(
        kern,
        grid=(M // tm,),
        in_specs=specs,
        out_specs=row_spec,
        out_shape=jax.ShapeDtypeStruct((M, D), F32),
        compiler_params=_cparams(1),
        name="swiglu_ffn",
    )(*args)
    return out.reshape(B, S, D)


def _kv_kernel(x_ref, g_ref, w_ref, kn_ref, cos_ref, sin_ref, ones_ref,
               kvc_ref, ks_ref, vs_ref, kw_ref, vw_ref):
    W = KV_WIDTH
    T = K_TILE
    n_sub = x_ref.shape[0] // T
    u = _rms(x_ref[...], g_ref[...]).astype(BF16)
    ones_bd = ones_ref[...]
    project = lambda t: _dot(u[t * T:(t + 1) * T], w_ref[...])
    kv_next = project(0)
    for t in range(n_sub):
        kv = kv_next
        rows = slice(t * T, (t + 1) * T)
        part = lambda j: kv[:, j * W:(j + 1) * W]
        mean_sq = {j: _head_mean_sq(part(j), ones_bd) for j in (2, 4)}
        if t + 1 < n_sub:
            kv_next = project(t + 1)
        cos_t = _tile_lanes(cos_ref[rows, :], W)
        sin_t = _tile_lanes(sin_ref[rows, :], W)
        kvc_ref[rows, :] = kv[:, 0:2 * W]
        for j, gain, k_ref in ((2, kn_ref[1:2, :], ks_ref), (4, kn_ref[2:3, :], kw_ref)):
            k = part(j) * lax.rsqrt(mean_sq[j] + EPS) * gain
            k = _rope_flat(k, cos_t, sin_t).astype(BF16)
            for g in range(N_KV_GROUPS):
                k_ref[g, rows, :] = k[:, g * HEAD_DIM:(g + 1) * HEAD_DIM]
        for j, vt_ref in ((3, vs_ref), (5, vw_ref)):
            v_t = part(j).T
            for g in range(N_KV_GROUPS):
                vt_ref[g, t, 0:HEAD_DIM, :] = v_t[g * HEAD_DIM:(g + 1) * HEAD_DIM, :].astype(BF16)
                vt_ref[g, t, HEAD_DIM:VT_ROWS, :] = jnp.ones((VT_ROWS - HEAD_DIM, T), BF16)


def _rope_tables(pos):
    half = HEAD_DIM // 2
    freqs = jnp.power(ROPE_THETA, -jnp.arange(half, dtype=F32) / half)
    ang = pos.astype(F32)[:, None] * freqs[None, :]
    cos, sin = jnp.cos(ang), jnp.sin(ang)
    cos_t = jnp.concatenate([cos, cos], axis=-1)
    sin_t = jnp.concatenate([-sin, sin], axis=-1)
    return cos_t, sin_t


def _block_diag_ones(width):
    seg = jnp.arange(width) // HEAD_DIM
    return (seg[:, None] == seg[None, :]).astype(BF16)


def _kv_proj(h, kv_norm, kv_w, k_norm, cos128, sin128, ts=512):
    B, S, D = h.shape
    W = KV_WIDTH
    kn = jnp.tile(k_norm, (1, N_KV_GROUPS))
    G = N_KV_GROUPS
    row = lambda width: pl.BlockSpec((None, ts, width), lambda b, s: (b, s, 0))
    tab = pl.BlockSpec((ts, LANES), lambda b, s: (s, 0))
    key_spec = pl.BlockSpec((None, G, ts, HEAD_DIM), lambda b, s: (b, 0, s, 0))
    val_spec = pl.BlockSpec((None, G, ts // K_TILE, VT_ROWS, K_TILE), lambda b, s: (b, 0, s, 0, 0))
    flat = jax.ShapeDtypeStruct((B, S, 2 * W), F32)
    keys = jax.ShapeDtypeStruct((B, G, S, HEAD_DIM), BF16)
    vals = jax.ShapeDtypeStruct((B, G, S // K_TILE, VT_ROWS, K_TILE), BF16)
    return pl.pallas_call(
        _kv_kernel,
        grid=(B, S // ts),
        in_specs=[row(D), _const_spec((1, D)), _const_spec((D, 6 * W)), _const_spec((3, W)),
                  tab, tab, _const_spec((W, W))],
        out_specs=[row(2 * W), key_spec, val_spec, key_spec, val_spec],
        out_shape=[flat, keys, vals, keys, vals],
        compiler_params=_cparams(2),
        name="shared_kv_proj",
    )(h, kv_norm.reshape(1, D), kv_w.astype(BF16), kn, cos128, sin128, _block_diag_ones(W))


N_CHUNK = 128
CHUNK_W = CMP_STRIDE * HEAD_DIM


def _cmp_kernel(xa_ref, xb_ref, pos_ref, w1_ref, b1_ref, w2_ref, b2_ref, kn_ref, cos_ref, sin_ref, o_ref):
    groups = []
    for x_ref in (xa_ref, xb_ref):
        nth = [x_ref[pl.ds(r, N_CHUNK, stride=CMP_STRIDE), :] for r in range(CMP_STRIDE)]
        for j in range(LANES // HEAD_DIM):
            groups.append(jnp.concatenate([t[:, j * HEAD_DIM:(j + 1) * HEAD_DIM] for t in nth], axis=1))
    x = jnp.concatenate(groups, axis=0)
    rows = x.shape[0]
    ya = _dot((x + pos_ref[0:1, :]).astype(BF16), w1_ref[0:CHUNK_W, :])
    yb = _dot((x + pos_ref[1:2, :]).astype(BF16), w1_ref[CHUNK_W:2 * CHUNK_W, :])
    hid = _gelu_tanh(ya + pltpu.roll(yb, rows - 1, 0) + b1_ref[...])
    out = _dot(hid.astype(BF16), w2_ref[...]) + b2_ref[...]

    @pl.when(pl.program_id(0) == 0)
    def _():
        k = _rms(out, kn_ref[...])
        half = HEAD_DIM // 2
        partner = jnp.concatenate([k[:, half:], k[:, :half]], axis=1)
        o_ref[...] = k * cos_ref[...] + partner * sin_ref[...]

    @pl.when(pl.program_id(0) == 1)
    def _():
        o_ref[...] = out


def _compress(kvc, cmp_pos, cmp_w1, cmp_b1, cmp_w2, cmp_b2, k_norm0):
    B, S, _ = kvc.shape
    G = N_KV_GROUPS
    rows = G * N_CHUNK
    pos = cmp_pos.reshape(2, 2, CHUNK_W)
    cmp_last = jnp.arange(N_CHUNK) * CMP_STRIDE + CMP_BLOCK - 1
    cos_t, sin_t = _rope_tables(cmp_last)
    cos_t, sin_t = jnp.tile(cos_t, (G, 1)), jnp.tile(sin_t, (G, 1))
    per_kv = lambda *shape: pl.BlockSpec((None,) + shape, lambda k, b: (k,) + (0,) * len(shape))
    return pl.pallas_call(
        _cmp_kernel,
        grid=(2, B),
        in_specs=[pl.BlockSpec((None, S, LANES), lambda k, b: (b, 0, 2 * k)),
                  pl.BlockSpec((None, S, LANES), lambda k, b: (b, 0, 2 * k + 1)),
                  per_kv(2, CHUNK_W), per_kv(2 * CHUNK_W, CMP_HIDDEN), per_kv(1, CMP_HIDDEN),
                  per_kv(CMP_HIDDEN, HEAD_DIM), per_kv(1, HEAD_DIM),
                  _const_spec((1, HEAD_DIM)), _const_spec((rows, HEAD_DIM)),
                  _const_spec((rows, HEAD_DIM))],
        out_specs=pl.BlockSpec((None, None, rows, HEAD_DIM), lambda k, b: (k, b, 0, 0)),
        out_shape=jax.ShapeDtypeStruct((2, B, rows, HEAD_DIM), F32),
        compiler_params=_cparams(2),
        name="kv_compress",
    )(kvc, kvc, pos, cmp_w1.astype(BF16), cmp_b1.reshape(2, 1, CMP_HIDDEN), cmp_w2.astype(BF16),
      cmp_b2.reshape(2, 1, HEAD_DIM), k_norm0.reshape(1, HEAD_DIM), cos_t, sin_t)


GATE_PAD = LANES
LOG2_E = 1.4426950408889634
Q_SCALE = HEAD_DIM ** -0.5 * LOG2_E


def _q_kernel(x_ref, g_ref, w_ref, gb_ref, qn_ref, cos_ref, sin_ref, ones_ref, q_ref, gate_ref):
    NQ = N_HEADS * HEAD_DIM
    T = Q_TILE
    n_sub = x_ref.shape[0] // T
    u = _rms(x_ref[...], g_ref[...]).astype(BF16)
    ones_bd = ones_ref[...]
    W = ones_bd.shape[0]
    project = lambda t: _dot(u[t * T:(t + 1) * T], w_ref[...])
    z_next = project(0)
    for t in range(n_sub):
        z = z_next
        rows = slice(t * T, (t + 1) * T)
        chunks = [z[:, c * W:(c + 1) * W] for c in range(NQ // W)]
        mean_sq = [_head_mean_sq(q, ones_bd) for q in chunks]
        if t + 1 < n_sub:
            z_next = project(t + 1)
        gate_ref[:, rows] = _sigmoid(z[:, NQ:] + gb_ref[...]).T[0:3 * N_HEADS, :]
        cos_t = _tile_lanes(cos_ref[rows, :], W)
        sin_t = _tile_lanes(sin_ref[rows, :], W)
        for c, q in enumerate(chunks):
            q = q * lax.rsqrt(mean_sq[c] + EPS) * qn_ref[...]
            q_tr = (_rope_flat(q, cos_t, sin_t) * Q_SCALE).T
            q_ref[c, t] = jnp.concatenate(
                [q_tr[j * HEAD_DIM:(j + 1) * HEAD_DIM, :] for j in range(W // HEAD_DIM)],
                axis=1).astype(BF16)


def _q_proj(h, norm_g, w_in, gate_b, q_norm_g, cos128, sin128, ts=512):
    B, S, D = h.shape
    NQ = N_HEADS * HEAD_DIM
    n_gate = 3 * N_HEADS
    w = jnp.pad(w_in, ((0, 0), (0, GATE_PAD - n_gate))).astype(BF16)
    gb = jnp.pad(gate_b, (0, GATE_PAD - n_gate)).reshape(1, GATE_PAD)
    W = KV_WIDTH
    qn = jnp.tile(q_norm_g, W // HEAD_DIM).reshape(1, W)
    row = lambda width: pl.BlockSpec((None, ts, width), lambda b, s: (b, s, 0))
    tab = pl.BlockSpec((ts, LANES), lambda b, s: (s, 0))
    return pl.pallas_call(
        _q_kernel,
        grid=(B, S // ts),
        in_specs=[row(D), _const_spec((1, D)), _const_spec((D, NQ + GATE_PAD)),
                  _const_spec((1, GATE_PAD)), _const_spec((1, W)), tab, tab, _const_spec((W, W))],
        out_specs=[pl.BlockSpec((None, N_KV_GROUPS, ts // Q_TILE, HEAD_DIM, HEADS_PER_GROUP * Q_TILE),
                                lambda b, s: (b, 0, s, 0, 0)),
                   pl.BlockSpec((None, n_gate, ts), lambda b, s: (b, 0, s))],
        out_shape=[jax.ShapeDtypeStruct((B, N_KV_GROUPS, S // Q_TILE, HEAD_DIM,
                                         HEADS_PER_GROUP * Q_TILE), BF16),
                   jax.ShapeDtypeStruct((B, n_gate, S), F32)],
        compiler_params=_cparams(2),
        name="nsa_q_proj",
    )(h, norm_g.reshape(1, D), w, gb, qn, cos128, sin128, _block_diag_ones(W))


N_SLC = 32
PS_PAD = 8
SCORE_PAIRS_AHEAD = 2


def _flash_step(slot, parts, v_t, m_scr, acc_scr):
    T = Q_TILE
    m_old = m_scr[slot]
    m_new, p_all = [], []
    for h in range(HEADS_PER_GROUP):
        hs = slice(h * T, (h + 1) * T)
        sm = [s[:, hs].astype(BF16) if bias is None else s[:, hs].astype(BF16) + bias
              for s, bias in parts]
        m_h = m_old[:, hs]
        for x in sm:
            m_h = jnp.maximum(m_h, jnp.max(x, axis=0, keepdims=True).astype(F32))
        m_new.append(m_h)
        p_all.append(jnp.concatenate([jnp.exp2(x - m_h.astype(BF16)) for x in sm], axis=0))
    m_new = jnp.concatenate(m_new, axis=1)
    alpha = jnp.exp2(m_old - m_new)
    m_scr[slot] = m_new
    acc_scr[slot] = alpha * acc_scr[slot] + _dot(v_t, jnp.concatenate(p_all, axis=1))


def _attn_kernel(q_ref, gt_ref, kc_ref, vct_ref, ks_ref, vst_ref, kw_ref, vwt_ref, o_ref,
                 sel_scr, ps_scr, m_scr, acc_scr, ot_scr):
    T = Q_TILE
    G = N_KV_GROUPS
    HG = HEADS_PER_GROUP
    NL = HG * T
    qi = pl.program_id(1)
    tpos = qi * T + lax.broadcasted_iota(jnp.int32, (1, T), 1)
    krow = lax.broadcasted_iota(jnp.int32, (K_TILE, T), 0)
    q_t = lambda g: q_ref[g]
    key_tile = lambda k_ref, g, kt: k_ref[g, pl.ds(pl.multiple_of(kt * K_TILE, K_TILE), K_TILE), :]

    def gate(branch, g):
        r0 = branch * N_HEADS + HG * g
        return jnp.concatenate([gt_ref[r0 + h:r0 + h + 1, :] for h in range(HG)], axis=1)

    def emit(g, o_t, first):
        for h in range(HG):
            rows = slice((HG * g + h) * HEAD_DIM, (HG * g + h + 1) * HEAD_DIM)
            piece = o_t[:, h * T:(h + 1) * T]
            ot_scr[rows, :] = piece if first else ot_scr[rows, :] + piece

    def reset_state():
        for g in range(G):
            m_scr[g] = jnp.full((1, NL), NEG, F32)
            acc_scr[g] = jnp.zeros((VT_ROWS, NL), F32)

    def emit_state(branch):
        for g in range(G):
            denom = acc_scr[g, HEAD_DIM:HEAD_DIM + 1, :]
            emit(g, acc_scr[g, 0:HEAD_DIM, :] * (gate(branch, g) / denom), first=False)

    def sweep(k_ref, vt_ref, tiles, biases, scores=None):
        pairs = [range(lo, min(lo + 2, len(tiles))) for lo in range(0, len(tiles), 2)]
        scores = dict(scores or {})

        def score_matmuls(pair, g):
            for i in pair:
                if (i, g) not in scores:
                    scores[i, g] = _dot(key_tile(k_ref, g, tiles[i]), q_t(g))

        for pair in pairs[:SCORE_PAIRS_AHEAD]:
            for g in range(G):
                score_matmuls(pair, g)
        for n, pair in enumerate(pairs):
            for g in range(G):
                parts = [(scores[i, g], None if biases[i] is None else biases[i](g)) for i in pair]
                v_t = jnp.concatenate([vt_ref[g, tiles[i]] for i in pair], axis=1)
                _flash_step(g, parts, v_t, m_scr, acc_scr)
                if n + SCORE_PAIRS_AHEAD < len(pairs):
                    score_matmuls(pairs[n + SCORE_PAIRS_AHEAD], g)

    def sweep_range(k_ref, vt_ref, n_tiles, bias_of):
        def run(tiles):
            sweep(k_ref, vt_ref, tiles,
                  [None if bias_of is None else functools.partial(bias_of, kt=kt) for kt in tiles])

        def quad(j, c):
            run([4 * j + i for i in range(4)])
            return c

        lax.fori_loop(0, n_tiles // 4, quad, 0)
        rem = n_tiles % 4
        base = n_tiles - rem

        @pl.when(rem >= 2)
        def _():
            run([base, base + 1])

        @pl.when(rem % 2 == 1)
        def _():
            run([n_tiles - 1])

    n_back = WINDOW // K_TILE

    def causal_bias(g):
        return jnp.where((qi * K_TILE + krow) <= tpos, 0.0, -jnp.inf).astype(BF16)

    def window_edge_bias(g):
        return jnp.where(((qi - n_back) * K_TILE + krow) > (tpos - WINDOW), 0.0,
                         -jnp.inf).astype(BF16)

    cmp_scores = [_dot(kc_ref[g], q_t(g)) for g in range(G)]
    win_diag_scores = {(0, g): _dot(key_tile(kw_ref, g, qi), q_t(g)) for g in range(G)}
    cvalid = ((krow * CMP_STRIDE + (CMP_BLOCK - 1)) <= tpos) & (krow < N_CHUNK - 1)
    cmp_bias = jnp.where(cvalid, 0.0, -jnp.inf)
    cmp_probs = []
    for g in range(G):
        psum = jnp.zeros((N_CHUNK, T), F32)
        probs = []
        for h in range(HG):
            sm = cmp_scores[g][:, h * T:(h + 1) * T] + cmp_bias
            e = jnp.exp2(sm - jnp.maximum(jnp.max(sm, axis=0, keepdims=True), NEG))
            den = jnp.sum(e, axis=0, keepdims=True)
            p = e / jnp.where(den > 0.0, den, 1.0)
            psum = psum + p
            probs.append(p.astype(BF16))
        cmp_probs.append(jnp.concatenate(probs, axis=1))
        ps_scr[g, 0:PS_PAD, :] = jnp.zeros((PS_PAD, T), F32)
        ps_scr[g, PS_PAD:PS_PAD + N_CHUNK, :] = psum
    for g in range(G):
        emit(g, gate(0, g) * _dot(vct_ref[g], cmp_probs[g]), first=True)

    reset_state()
    sweep(kw_ref, vwt_ref, [qi], [causal_bias], scores=win_diag_scores)

    @pl.when(qi >= n_back)
    def _():
        sweep(kw_ref, vwt_ref, [qi - 1 - i for i in range(n_back)],
              [None] * (n_back - 1) + [window_edge_bias])

    @pl.when(qi < n_back)
    def _():
        sweep_range(kw_ref, vwt_ref, qi, None)
    emit_state(2)

    all_causal_fit = (qi * T + T - 1) // SLC_BLOCK + 1 <= SLC_TOPK

    @pl.when(all_causal_fit)
    def _():
        for g in range(G):
            sel_scr[g] = jnp.zeros((N_SLC, T), F32)

    @pl.when(jnp.logical_not(all_causal_fit))
    def _():
        ROWS = 8
        jrow = lax.broadcasted_iota(jnp.int32, (N_SLC, T), 0)
        cur = tpos >> 6
        causal_blk = jrow <= cur
        forced = (jrow == 0) | (causal_blk & ((cur - jrow) < N_LOCAL_BLOCKS))
        for g in range(G):
            tap = lambda k: ps_scr[g, pl.ds(PS_PAD + k, N_SLC, stride=4), :]
            imp = 0.5 * tap(-1) + tap(0) + tap(1) + tap(2) + 0.5 * tap(3)
            score = jnp.where(forced, FORCE, jnp.where(causal_blk, imp, NEG))
            parts = [score[r:r + ROWS] for r in range(0, N_SLC, ROWS)]
            ranks = [jnp.zeros((ROWS, T), F32) for _ in parts]
            for j2 in range(N_SLC):
                other = score[j2:j2 + 1, :]
                for i, part in enumerate(parts):
                    r0 = i * ROWS
                    if r0 + ROWS - 1 < j2:
                        beats = other > part
                    elif r0 > j2:
                        beats = other >= part
                    else:
                        beats = (other > part) | ((other == part) & (jrow[r0:r0 + ROWS] > j2))
                    ranks[i] = ranks[i] + jnp.where(beats, 1.0, 0.0)
            rank = jnp.concatenate(ranks, axis=0)
            sel_scr[g] = jnp.where(rank < SLC_TOPK, 0.0, -jnp.inf)

    def slc_bias(g, kt):
        half = K_TILE // 2
        top = jnp.broadcast_to(sel_scr[g, pl.ds(2 * kt, 1), :], (half, T))
        bot = jnp.broadcast_to(sel_scr[g, pl.ds(2 * kt + 1, 1), :], (half, T))
        causal = jnp.where((kt * K_TILE + krow) <= tpos, 0.0, -jnp.inf)
        return (jnp.concatenate([top, bot], axis=0) + causal).astype(BF16)

    reset_state()
    sweep_range(ks_ref, vst_ref, qi + 1, slc_bias)
    emit_state(1)

    o_ref[...] = ot_scr[...].T.astype(BF16)


def _nsa_attention(q, gates_t, k_cmp, v_cmp, ks, vst, kw, vwt):
    B, _, S, _ = ks.shape
    G, T = N_KV_GROUPS, Q_TILE
    NQ = N_HEADS * HEAD_DIM
    NT = S // K_TILE
    kc = k_cmp.reshape(B, G, N_CHUNK, HEAD_DIM).astype(BF16)
    vct = v_cmp.reshape(B, G, N_CHUNK, HEAD_DIM).transpose(0, 1, 3, 2).astype(BF16)
    per_b = lambda *shape: pl.BlockSpec((None,) + shape, lambda b, i: (b,) + (0,) * len(shape))
    return pl.pallas_call(
        _attn_kernel,
        grid=(B, S // T),
        in_specs=[pl.BlockSpec((None, G, None, HEAD_DIM, HEADS_PER_GROUP * T),
                               lambda b, i: (b, 0, i, 0, 0)),
                  pl.BlockSpec((None, 3 * N_HEADS, T), lambda b, i: (b, 0, i)),
                  per_b(G, N_CHUNK, HEAD_DIM), per_b(G, HEAD_DIM, N_CHUNK),
                  per_b(G, S, HEAD_DIM), per_b(G, NT, VT_ROWS, K_TILE),
                  per_b(G, S, HEAD_DIM), per_b(G, NT, VT_ROWS, K_TILE)],
        out_specs=pl.BlockSpec((None, T, NQ), lambda b, i: (b, i, 0)),
        out_shape=jax.ShapeDtypeStruct((B, S, NQ), BF16),
        scratch_shapes=[pltpu.VMEM((G, N_SLC, T), F32), pltpu.VMEM((G, PS_PAD + N_CHUNK, T), F32),
                        pltpu.VMEM((G, 1, HEADS_PER_GROUP * T), F32),
                        pltpu.VMEM((G, VT_ROWS, HEADS_PER_GROUP * T), F32),
                        pltpu.VMEM((NQ, T), F32)],
        compiler_params=_cparams(2),
        name="nsa_attention",
    )(q, gates_t, kc, vct, ks, vst, kw, vwt)


def kernel(x, a_norm, a_w_in, a_conv_w, a_conv_b, a_gate_w, a_gate_b, a_lambda, a_w_out,
           kv_norm, kv_w, k_norm, cmp_pos, cmp_w1, cmp_b1, cmp_w2, cmp_b2,
           b_norm, b_w_in, b_gate_b, q_norm, b_w_out, f_norm, f_w_in, f_w_out):
    B, S, D = x.shape
    assert D == D_MODEL and S == N_SLC * SLC_BLOCK and S == N_CHUNK * CMP_STRIDE
    n_a = a_norm.shape[0]
    n_b = b_norm.shape[0]
    h = x
    for i in range(n_a):
        h = _recurrent_block(h, a_norm[i], a_w_in[i], a_conv_w[i], a_conv_b[i], a_gate_w[i],
                             a_gate_b[i], a_lambda[i], a_w_out[i])
        h = _swiglu(h, f_norm[i], f_w_in[i], f_w_out[i])

    cos_t, sin_t = _rope_tables(jnp.arange(S))
    cos128, sin128 = jnp.tile(cos_t, (1, 2)), jnp.tile(sin_t, (1, 2))
    kvc, ks, vs, kw, vw = _kv_proj(h, kv_norm, kv_w, k_norm, cos128, sin128)
    cmp = _compress(kvc, cmp_pos, cmp_w1, cmp_b1, cmp_w2, cmp_b2, k_norm[0])
    for j in range(n_b):
        q, gates = _q_proj(h, b_norm[j], b_w_in[j], b_gate_b[j], q_norm[j], cos128, sin128)
        o = _nsa_attention(q, gates, cmp[0], cmp[1], ks, vs, kw, vw)
        layer = n_a + j
        h = _swiglu(h, f_norm[layer], f_w_in[layer], f_w_out[layer], attn=o, w_o=b_w_out[j])
    return h
```

```python
import functools

import jax
import jax.numpy as jnp
from jax import lax
from jax.experimental import pallas as pl
from jax.experimental.pallas import tpu as pltpu

F32 = jnp.float32
BF16 = jnp.bfloat16

D_MODEL = 1024
LRU_WIDTH = D_MODEL
LRU_HEADS = 8
LRU_BLOCK = LRU_WIDTH // LRU_HEADS
CONV_WIDTH = 4
LRU_C = 8.0
HEAD_DIM = 64
N_HEADS = D_MODEL // HEAD_DIM
N_KV_GROUPS = 4
HEADS_PER_GROUP = N_HEADS // N_KV_GROUPS
CMP_BLOCK = 32
CMP_STRIDE = 16
CMP_HIDDEN = 256
SLC_BLOCK = 64
SLC_TOPK = 16
N_LOCAL_BLOCKS = 2
WINDOW = 512
ROPE_THETA = 10000.0
FFN_HIDDEN = 2816
EPS = 1e-6
NEG = -1e30
FORCE = 1e30

LANES = 128
KV_WIDTH = N_KV_GROUPS * HEAD_DIM
Q_TILE = 128
K_TILE = 128
VT_ROWS = HEAD_DIM + 16
PROJ_ROWS = 256
VMEM_LIMIT = 56 * 1024 * 1024


def _cparams(n_axes):
    return pltpu.CompilerParams(dimension_semantics=("arbitrary",) * n_axes,
                                vmem_limit_bytes=VMEM_LIMIT)


def _const_spec(shape):
    nd = len(shape)
    return pl.BlockSpec(shape, lambda *_: (0,) * nd, pipeline_mode=pl.Buffered(1))


def _rms(x, g):
    ms = jnp.mean(x * x, axis=-1, keepdims=True)
    return x * lax.rsqrt(ms + EPS) * g


def _sigmoid(x):
    return 1.0 / (1.0 + jnp.exp(-x))


def _gelu_tanh(x):
    c = 0.7978845608028654
    return x * (0.5 * (1.0 + jnp.tanh(c * (x + 0.044715 * (x * x * x)))))


def _dot(a, b):
    return jnp.dot(a, b, preferred_element_type=F32)


def _head_mean_sq(x, ones_bd):
    sq = x * x
    hi = sq.astype(BF16)
    lo = (sq - hi.astype(F32)).astype(BF16)
    return (_dot(hi, ones_bd) + _dot(lo, ones_bd)) * (1.0 / HEAD_DIM)


def _rope_flat(x, cos_t, sin_t):
    width = x.shape[-1]
    lane = lax.broadcasted_iota(jnp.int32, x.shape, 1)
    upper = (lane & (HEAD_DIM // 2)) != 0
    partner = jnp.where(upper, pltpu.roll(x, HEAD_DIM // 2, 1),
                        pltpu.roll(x, width - HEAD_DIM // 2, 1))
    return x * cos_t + partner * sin_t


def _tile_lanes(t, width):
    reps = width // t.shape[-1]
    return t if reps == 1 else jnp.concatenate([t] * reps, axis=1)


SUBLANES = 8


def _segment_perm(ts):
    seg_len = ts // SUBLANES
    dst = jnp.arange(ts)
    src = (dst % SUBLANES) * seg_len + dst // SUBLANES
    return (src[:, None] == jnp.arange(ts)[None, :]).astype(BF16)


def _scan_segments(a, b, h_in):
    n_steps = a.shape[0] // SUBLANES
    vreg = lambda x, j: x[j * SUBLANES:(j + 1) * SUBLANES]
    h_loc, a_cum = [vreg(b, 0)], [vreg(a, 0)]
    for j in range(1, n_steps):
        h_loc.append(vreg(a, j) * h_loc[-1] + vreg(b, j))
        a_cum.append(vreg(a, j) * a_cum[-1])
    seg_a, seg_h = a_cum[-1], h_loc[-1]
    carry = [h_in]
    for s in range(SUBLANES):
        carry.append(seg_a[s:s + 1] * carry[-1] + seg_h[s:s + 1])
    enter = jnp.concatenate(carry[:SUBLANES], axis=0)
    h = jnp.concatenate([h_loc[j] + a_cum[j] * enter for j in range(n_steps)], axis=0)
    return h, carry[SUBLANES]


def _rec_kernel(x_ref, g_ref, perm_ref, perm_t_ref, win_ref, cw_ref, cb_ref, wg_ref, gb_ref,
                lam_ref, wout_ref, o_ref, tail, hcar):
    R = LRU_WIDTH
    sub_rows = perm_ref.shape[0]
    n_sub = x_ref.shape[0] // sub_rows
    n_steps = sub_rows // SUBLANES
    taps = CONV_WIDTH - 1

    @pl.when(pl.program_id(1) == 0)
    def _():
        tail[...] = jnp.zeros_like(tail)
        hcar[...] = jnp.zeros_like(hcar)

    x = x_ref[...]
    u = _rms(x, g_ref[...]).astype(BF16)
    def in_proj(t):
        ut = _dot(perm_ref[...], u[t * sub_rows:(t + 1) * sub_rows]).astype(BF16)
        return _dot(ut, win_ref[...])

    sub = lax.broadcasted_iota(jnp.int32, (SUBLANES, LRU_BLOCK), 0)
    softplus_neg = jnp.maximum(-lam_ref[...], 0.0) + jnp.log1p(jnp.exp(-jnp.abs(lam_ref[...])))
    log_a_scale = -LRU_C * softplus_neg
    prev_tail = tail[...]
    z_next = in_proj(0)
    for t in range(n_sub):
        z = z_next
        conv_out, gate_pre = [], []
        for hh in range(LRU_HEADS):
            cs = slice(hh * LRU_BLOCK, (hh + 1) * LRU_BLOCK)
            xc = z[:, R + hh * LRU_BLOCK:R + (hh + 1) * LRU_BLOCK]
            wrapped = []
            for k in range(taps):
                cur = xc[(n_steps - taps + k) * SUBLANES:(n_steps - taps + k + 1) * SUBLANES]
                prev = prev_tail[k * SUBLANES:(k + 1) * SUBLANES, cs]
                wrapped.append(pltpu.roll(jnp.where(sub == SUBLANES - 1, prev, cur), 1, 0))
            back = lambda d: jnp.concatenate(
                wrapped[taps - d:] + [xc[0:(n_steps - d) * SUBLANES]], axis=0)
            cw = cw_ref[:, cs]
            xr = cb_ref[:, cs] + back(3) * cw[0:1]
            xr = xr + back(2) * cw[1:2]
            xr = xr + back(1) * cw[2:3]
            xr = xr + xc * cw[3:4]
            conv_out.append(xr)
            gate_pre.append(_dot(xr.astype(BF16), wg_ref[hh]))
        if t + 1 < n_sub:
            z_next = in_proj(t + 1)
        gated = []
        for hh in range(LRU_HEADS):
            cs = slice(hh * LRU_BLOCK, (hh + 1) * LRU_BLOCK)
            xr, gates = conv_out[hh], gate_pre[hh]
            gb = gb_ref[:, cs]
            r = _sigmoid(gates[:, :LRU_BLOCK] + gb[0:1])
            i = _sigmoid(gates[:, LRU_BLOCK:] + gb[1:2])
            log_a = log_a_scale[:, cs] * r
            a = jnp.exp(log_a)
            bterm = jnp.sqrt(1.0 - a * a) * (i * xr)
            hs, hcar[:, cs] = _scan_segments(a, bterm, hcar[:, cs])
            gated.append((_gelu_tanh(z[:, cs]) * hs).astype(BF16))
        prev_tail = z[(n_steps - taps) * SUBLANES:, R:]
        yh = _dot(perm_t_ref[...], jnp.concatenate(gated, axis=1)).astype(BF16)
        rows = slice(t * sub_rows, (t + 1) * sub_rows)
        o_ref[rows, :] = x[rows] + _dot(yh, wout_ref[...])
    tail[...] = prev_tail


def _recurrent_block(h, norm_g, w_in, conv_w, conv_b, gate_w, gate_b, lam, w_out, ts=1024,
                     sub_rows=256):
    B, S, D = h.shape
    R = LRU_WIDTH
    wg = jnp.concatenate([gate_w[0], gate_w[1]], axis=-1).astype(BF16)
    perm = _segment_perm(sub_rows)
    return pl.pallas_call(
        _rec_kernel,
        grid=(B, S // ts),
        in_specs=[
            pl.BlockSpec((None, ts, D), lambda b, s: (b, s, 0)),
            _const_spec((1, D)),
            _const_spec((sub_rows, sub_rows)),
            _const_spec((sub_rows, sub_rows)),
            _const_spec((D, 2 * R)),
            _const_spec((CONV_WIDTH, R)),
            _const_spec((1, R)),
            _const_spec((LRU_HEADS, LRU_BLOCK, 2 * LRU_BLOCK)),
            _const_spec((2, R)),
            _const_spec((1, R)),
            _const_spec((R, D)),
        ],
        out_specs=pl.BlockSpec((None, ts, D), lambda b, s: (b, s, 0)),
        out_shape=jax.ShapeDtypeStruct((B, S, D), F32),
        scratch_shapes=[pltpu.VMEM(((CONV_WIDTH - 1) * SUBLANES, R), F32), pltpu.VMEM((1, R), F32)],
        compiler_params=_cparams(2),
        name="rglru_block",
    )(h, norm_g.reshape(1, D), perm, perm.T, w_in.astype(BF16), conv_w, conv_b.reshape(1, R), wg,
      gate_b, lam.reshape(1, R), w_out.astype(BF16))


FFN_CHUNK = FFN_HIDDEN // 2


def _ffn_body(x, g_ref, win_ref, wout_ref, o_ref):
    u = _rms(x, g_ref[...]).astype(BF16)

    acc = x
    for c in range(FFN_HIDDEN // FFN_CHUNK):
        lo = c * FFN_CHUNK
        gate = _dot(u, win_ref[:, lo:lo + FFN_CHUNK])
        up = _dot(u, win_ref[:, FFN_HIDDEN + lo:FFN_HIDDEN + lo + FFN_CHUNK])
        act = ((gate * _sigmoid(gate)) * up).astype(BF16)
        acc = acc + _dot(act, wout_ref[lo:lo + FFN_CHUNK, :])
    o_ref[...] = acc


def _ffn_kernel(x_ref, g_ref, win_ref, wout_ref, o_ref):
    _ffn_body(x_ref[...], g_ref, win_ref, wout_ref, o_ref)


def _proj_ffn_kernel(x_ref, a_ref, wo_ref, g_ref, win_ref, wout_ref, o_ref):
    _ffn_body(x_ref[...] + _dot(a_ref[...], wo_ref[...]), g_ref, win_ref, wout_ref, o_ref)


def _swiglu(h, norm_g, w_in, w_out, attn=None, w_o=None, tm=512):
    B, S, D = h.shape
    M = B * S
    row_spec = pl.BlockSpec((tm, D), lambda i: (i, 0))
    w_specs = [_const_spec((1, D)), _const_spec((D, 2 * FFN_HIDDEN)), _const_spec((FFN_HIDDEN, D))]
    w_args = (norm_g.reshape(1, D), w_in.astype(BF16), w_out.astype(BF16))
    if attn is None:
        kern, specs, args = _ffn_kernel, [row_spec] + w_specs, (h.reshape(M, D),) + w_args
    else:
        kern = _proj_ffn_kernel
        specs = [row_spec, row_spec, _const_spec((D, D))] + w_specs
        args = (h.reshape(M, D), attn.reshape(M, D), w_o.astype(BF16)) + w_args
    out = pl.pallas_call(
        kern,
        grid=(M // tm,),
        in_specs=specs,
        out_specs=row_spec,
        out_shape=jax.ShapeDtypeStruct((M, D), F32),
        compiler_params=_cparams(1),
        name="swiglu_ffn",
    )(*args)
    return out.reshape(B, S, D)


def _kv_kernel(x_ref, g_ref, w_ref, kn_ref, cos_ref, sin_ref, ones_ref,
               kvc_ref, ks_ref, vs_ref, kw_ref, vw_ref):
    W = KV_WIDTH
    T = PROJ_ROWS
    n_sub = x_ref.shape[0] // T
    u = _rms(x_ref[...], g_ref[...]).astype(BF16)
    ones_bd = ones_ref[...]
    project = lambda t: _dot(u[t * T:(t + 1) * T], w_ref[...])
    kv_next = project(0)
    for t in range(n_sub):
        kv = kv_next
        rows = slice(t * T, (t + 1) * T)
        part = lambda j: kv[:, j * W:(j + 1) * W]
        mean_sq = {j: _head_mean_sq(part(j), ones_bd) for j in (2, 4)}
        if t + 1 < n_sub:
            kv_next = project(t + 1)
        cos_t = _tile_lanes(cos_ref[rows, :], W)
        sin_t = _tile_lanes(sin_ref[rows, :], W)
        kvc_ref[rows, :] = kv[:, 0:2 * W]
        for j, gain, k_ref in ((2, kn_ref[1:2, :], ks_ref), (4, kn_ref[2:3, :], kw_ref)):
            k = part(j) * lax.rsqrt(mean_sq[j] + EPS) * gain
            k = _rope_flat(k, cos_t, sin_t).astype(BF16)
            for g in range(N_KV_GROUPS):
                k_ref[g, rows, :] = k[:, g * HEAD_DIM:(g + 1) * HEAD_DIM]
        for j, vt_ref in ((3, vs_ref), (5, vw_ref)):
            v_t = part(j).T
            for g in range(N_KV_GROUPS):
                for i in range(T // K_TILE):
                    tile = t * (T // K_TILE) + i
                    vt_ref[g, tile, 0:HEAD_DIM, :] = v_t[g * HEAD_DIM:(g + 1) * HEAD_DIM,
                                                         i * K_TILE:(i + 1) * K_TILE].astype(BF16)
                    vt_ref[g, tile, HEAD_DIM:VT_ROWS, :] = jnp.ones((VT_ROWS - HEAD_DIM, K_TILE), BF16)


def _rope_tables(pos):
    half = HEAD_DIM // 2
    freqs = jnp.power(ROPE_THETA, -jnp.arange(half, dtype=F32) / half)
    ang = pos.astype(F32)[:, None] * freqs[None, :]
    cos, sin = jnp.cos(ang), jnp.sin(ang)
    cos_t = jnp.concatenate([cos, cos], axis=-1)
    sin_t = jnp.concatenate([-sin, sin], axis=-1)
    return cos_t, sin_t


def _block_diag_ones(width):
    seg = jnp.arange(width) // HEAD_DIM
    return (seg[:, None] == seg[None, :]).astype(BF16)


def _kv_proj(h, kv_norm, kv_w, k_norm, cos128, sin128, ts=512):
    B, S, D = h.shape
    W = KV_WIDTH
    kn = jnp.tile(k_norm, (1, N_KV_GROUPS))
    G = N_KV_GROUPS
    row = lambda width: pl.BlockSpec((None, ts, width), lambda b, s: (b, s, 0))
    tab = pl.BlockSpec((ts, LANES), lambda b, s: (s, 0))
    key_spec = pl.BlockSpec((None, G, ts, HEAD_DIM), lambda b, s: (b, 0, s, 0))
    val_spec = pl.BlockSpec((None, G, ts // K_TILE, VT_ROWS, K_TILE), lambda b, s: (b, 0, s, 0, 0))
    flat = jax.ShapeDtypeStruct((B, S, 2 * W), F32)
    keys = jax.ShapeDtypeStruct((B, G, S, HEAD_DIM), BF16)
    vals = jax.ShapeDtypeStruct((B, G, S // K_TILE, VT_ROWS, K_TILE), BF16)
    return pl.pallas_call(
        _kv_kernel,
        grid=(B, S // ts),
        in_specs=[row(D), _const_spec((1, D)), _const_spec((D, 6 * W)), _const_spec((3, W)),
                  tab, tab, _const_spec((W, W))],
        out_specs=[row(2 * W), key_spec, val_spec, key_spec, val_spec],
        out_shape=[flat, keys, vals, keys, vals],
        compiler_params=_cparams(2),
        name="shared_kv_proj",
    )(h, kv_norm.reshape(1, D), kv_w.astype(BF16), kn, cos128, sin128, _block_diag_ones(W))


N_CHUNK = 128
CHUNK_W = CMP_STRIDE * HEAD_DIM


def _cmp_kernel(xa_ref, xb_ref, pos_ref, w1_ref, b1_ref, w2_ref, b2_ref, kn_ref, cos_ref, sin_ref, o_ref):
    groups = []
    for x_ref in (xa_ref, xb_ref):
        nth = [x_ref[pl.ds(r, N_CHUNK, stride=CMP_STRIDE), :] for r in range(CMP_STRIDE)]
        for j in range(LANES // HEAD_DIM):
            groups.append(jnp.concatenate([t[:, j * HEAD_DIM:(j + 1) * HEAD_DIM] for t in nth], axis=1))
    x = jnp.concatenate(groups, axis=0)
    rows = x.shape[0]
    ya = _dot((x + pos_ref[0:1, :]).astype(BF16), w1_ref[0:CHUNK_W, :])
    yb = _dot((x + pos_ref[1:2, :]).astype(BF16), w1_ref[CHUNK_W:2 * CHUNK_W, :])
    hid = _gelu_tanh(ya + pltpu.roll(yb, rows - 1, 0) + b1_ref[...])
    out = _dot(hid.astype(BF16), w2_ref[...]) + b2_ref[...]

    @pl.when(pl.program_id(0) == 0)
    def _():
        k = _rms(out, kn_ref[...])
        half = HEAD_DIM // 2
        partner = jnp.concatenate([k[:, half:], k[:, :half]], axis=1)
        o_ref[...] = k * cos_ref[...] + partner * sin_ref[...]

    @pl.when(pl.program_id(0) == 1)
    def _():
        o_ref[...] = out


def _compress(kvc, cmp_pos, cmp_w1, cmp_b1, cmp_w2, cmp_b2, k_norm0):
    B, S, _ = kvc.shape
    G = N_KV_GROUPS
    rows = G * N_CHUNK
    pos = cmp_pos.reshape(2, 2, CHUNK_W)
    cmp_last = jnp.arange(N_CHUNK) * CMP_STRIDE + CMP_BLOCK - 1
    cos_t, sin_t = _rope_tables(cmp_last)
    cos_t, sin_t = jnp.tile(cos_t, (G, 1)), jnp.tile(sin_t, (G, 1))
    per_kv = lambda *shape: pl.BlockSpec((None,) + shape, lambda k, b: (k,) + (0,) * len(shape))
    return pl.pallas_call(
        _cmp_kernel,
        grid=(2, B),
        in_specs=[pl.BlockSpec((None, S, LANES), lambda k, b: (b, 0, 2 * k)),
                  pl.BlockSpec((None, S, LANES), lambda k, b: (b, 0, 2 * k + 1)),
                  per_kv(2, CHUNK_W), per_kv(2 * CHUNK_W, CMP_HIDDEN), per_kv(1, CMP_HIDDEN),
                  per_kv(CMP_HIDDEN, HEAD_DIM), per_kv(1, HEAD_DIM),
                  _const_spec((1, HEAD_DIM)), _const_spec((rows, HEAD_DIM)),
                  _const_spec((rows, HEAD_DIM))],
        out_specs=pl.BlockSpec((None, None, rows, HEAD_DIM), lambda k, b: (k, b, 0, 0)),
        out_shape=jax.ShapeDtypeStruct((2, B, rows, HEAD_DIM), F32),
        compiler_params=_cparams(2),
        name="kv_compress",
    )(kvc, kvc, pos, cmp_w1.astype(BF16), cmp_b1.reshape(2, 1, CMP_HIDDEN), cmp_w2.astype(BF16),
      cmp_b2.reshape(2, 1, HEAD_DIM), k_norm0.reshape(1, HEAD_DIM), cos_t, sin_t)


GATE_PAD = LANES
LOG2_E = 1.4426950408889634
Q_SCALE = HEAD_DIM ** -0.5 * LOG2_E


def _q_kernel(x_ref, g_ref, w_ref, gb_ref, qn_ref, cos_ref, sin_ref, ones_ref, q_ref, gate_ref):
    NQ = N_HEADS * HEAD_DIM
    T = PROJ_ROWS
    n_sub = x_ref.shape[0] // T
    u = _rms(x_ref[...], g_ref[...]).astype(BF16)
    ones_bd = ones_ref[...]
    W = ones_bd.shape[0]
    project = lambda t: _dot(u[t * T:(t + 1) * T], w_ref[...])
    z_next = project(0)
    for t in range(n_sub):
        z = z_next
        rows = slice(t * T, (t + 1) * T)
        chunks = [z[:, c * W:(c + 1) * W] for c in range(NQ // W)]
        mean_sq = [_head_mean_sq(q, ones_bd) for q in chunks]
        if t + 1 < n_sub:
            z_next = project(t + 1)
        gate_ref[:, rows] = _sigmoid(z[:, NQ:] + gb_ref[...]).T[0:3 * N_HEADS, :]
        cos_t = _tile_lanes(cos_ref[rows, :], W)
        sin_t = _tile_lanes(sin_ref[rows, :], W)
        for c, q in enumerate(chunks):
            q = q * lax.rsqrt(mean_sq[c] + EPS) * qn_ref[...]
            q_tr = (_rope_flat(q, cos_t, sin_t) * Q_SCALE).T
            for i in range(T // Q_TILE):
                q_ref[c, t * (T // Q_TILE) + i] = jnp.concatenate(
                    [q_tr[j * HEAD_DIM:(j + 1) * HEAD_DIM, i * Q_TILE:(i + 1) * Q_TILE]
                     for j in range(W // HEAD_DIM)], axis=1).astype(BF16)


def _q_proj(h, norm_g, w_in, gate_b, q_norm_g, cos128, sin128, ts=512):
    B, S, D = h.shape
    NQ = N_HEADS * HEAD_DIM
    n_gate = 3 * N_HEADS
    w = jnp.pad(w_in, ((0, 0), (0, GATE_PAD - n_gate))).astype(BF16)
    gb = jnp.pad(gate_b, (0, GATE_PAD - n_gate)).reshape(1, GATE_PAD)
    W = KV_WIDTH
    qn = jnp.tile(q_norm_g, W // HEAD_DIM).reshape(1, W)
    row = lambda width: pl.BlockSpec((None, ts, width), lambda b, s: (b, s, 0))
    tab = pl.BlockSpec((ts, LANES), lambda b, s: (s, 0))
    return pl.pallas_call(
        _q_kernel,
        grid=(B, S // ts),
        in_specs=[row(D), _const_spec((1, D)), _const_spec((D, NQ + GATE_PAD)),
                  _const_spec((1, GATE_PAD)), _const_spec((1, W)), tab, tab, _const_spec((W, W))],
        out_specs=[pl.BlockSpec((None, N_KV_GROUPS, ts // Q_TILE, HEAD_DIM, HEADS_PER_GROUP * Q_TILE),
                                lambda b, s: (b, 0, s, 0, 0)),
                   pl.BlockSpec((None, n_gate, ts), lambda b, s: (b, 0, s))],
        out_shape=[jax.ShapeDtypeStruct((B, N_KV_GROUPS, S // Q_TILE, HEAD_DIM,
                                         HEADS_PER_GROUP * Q_TILE), BF16),
                   jax.ShapeDtypeStruct((B, n_gate, S), F32)],
        compiler_params=_cparams(2),
        name="nsa_q_proj",
    )(h, norm_g.reshape(1, D), w, gb, qn, cos128, sin128, _block_diag_ones(W))


N_SLC = 32
PS_PAD = 8
SCORE_PAIRS_AHEAD = 2


def _flash_step(slot, parts, v_t, m_scr, acc_scr):
    T = Q_TILE
    m_old = m_scr[slot]
    m_new, p_all = [], []
    for h in range(HEADS_PER_GROUP):
        hs = slice(h * T, (h + 1) * T)
        sm = [s[:, hs].astype(BF16) if bias is None else s[:, hs].astype(BF16) + bias
              for s, bias in parts]
        m_h = m_old[:, hs]
        for x in sm:
            m_h = jnp.maximum(m_h, jnp.max(x, axis=0, keepdims=True).astype(F32))
        m_new.append(m_h)
        p_all.append(jnp.concatenate([jnp.exp2(x - m_h.astype(BF16)) for x in sm], axis=0))
    m_new = jnp.concatenate(m_new, axis=1)
    alpha = jnp.exp2(m_old - m_new)
    m_scr[slot] = m_new
    acc_scr[slot] = alpha * acc_scr[slot] + _dot(v_t, jnp.concatenate(p_all, axis=1))


def _attn_kernel(q_ref, gt_ref, kc_ref, vct_ref, ks_ref, vst_ref, kw_ref, vwt_ref, o_ref,
                 sel_scr, ps_scr, m_scr, acc_scr, ot_scr):
    T = Q_TILE
    G = N_KV_GROUPS
    HG = HEADS_PER_GROUP
    NL = HG * T
    qi = pl.program_id(1)
    tpos = qi * T + lax.broadcasted_iota(jnp.int32, (1, T), 1)
    krow = lax.broadcasted_iota(jnp.int32, (K_TILE, T), 0)
    q_t = lambda g: q_ref[g]
    key_tile = lambda k_ref, g, kt: k_ref[g, pl.ds(pl.multiple_of(kt * K_TILE, K_TILE), K_TILE), :]

    def gate(branch, g):
        r0 = branch * N_HEADS + HG * g
        return jnp.concatenate([gt_ref[r0 + h:r0 + h + 1, :] for h in range(HG)], axis=1)

    def emit(g, o_t, first):
        for h in range(HG):
            rows = slice((HG * g + h) * HEAD_DIM, (HG * g + h + 1) * HEAD_DIM)
            piece = o_t[:, h * T:(h + 1) * T]
            ot_scr[rows, :] = piece if first else ot_scr[rows, :] + piece

    def reset_state():
        for g in range(G):
            m_scr[g] = jnp.full((1, NL), NEG, F32)
            acc_scr[g] = jnp.zeros((VT_ROWS, NL), F32)

    def emit_state(branch):
        for g in range(G):
            denom = acc_scr[g, HEAD_DIM:HEAD_DIM + 1, :]
            emit(g, acc_scr[g, 0:HEAD_DIM, :] * (gate(branch, g) / denom), first=False)

    def sweep(k_ref, vt_ref, tiles, biases, scores=None):
        pairs = [range(lo, min(lo + 2, len(tiles))) for lo in range(0, len(tiles), 2)]
        scores = dict(scores or {})

        def score_matmuls(pair, g):
            for i in pair:
                if (i, g) not in scores:
                    scores[i, g] = _dot(key_tile(k_ref, g, tiles[i]), q_t(g))

        for pair in pairs[:SCORE_PAIRS_AHEAD]:
            for g in range(G):
                score_matmuls(pair, g)
        for n, pair in enumerate(pairs):
            for g in range(G):
                parts = [(scores[i, g], None if biases[i] is None else biases[i](g)) for i in pair]
                v_t = jnp.concatenate([vt_ref[g, tiles[i]] for i in pair], axis=1)
                _flash_step(g, parts, v_t, m_scr, acc_scr)
                if n + SCORE_PAIRS_AHEAD < len(pairs):
                    score_matmuls(pairs[n + SCORE_PAIRS_AHEAD], g)

    def sweep_range(k_ref, vt_ref, n_tiles, bias_of):
        def run(tiles):
            sweep(k_ref, vt_ref, tiles,
                  [None if bias_of is None else functools.partial(bias_of, kt=kt) for kt in tiles])

        def quad(j, c):
            run([4 * j + i for i in range(4)])
            return c

        lax.fori_loop(0, n_tiles // 4, quad, 0)
        rem = n_tiles % 4
        base = n_tiles - rem

        @pl.when(rem >= 2)
        def _():
            run([base, base + 1])

        @pl.when(rem % 2 == 1)
        def _():
            run([n_tiles - 1])

    n_back = WINDOW // K_TILE

    def causal_bias(g):
        return jnp.where((qi * K_TILE + krow) <= tpos, 0.0, -jnp.inf).astype(BF16)

    def window_edge_bias(g):
        return jnp.where(((qi - n_back) * K_TILE + krow) > (tpos - WINDOW), 0.0,
                         -jnp.inf).astype(BF16)

    cmp_scores = [_dot(kc_ref[g], q_t(g)) for g in range(G)]
    win_diag_scores = {(0, g): _dot(key_tile(kw_ref, g, qi), q_t(g)) for g in range(G)}
    cvalid = ((krow * CMP_STRIDE + (CMP_BLOCK - 1)) <= tpos) & (krow < N_CHUNK - 1)
    cmp_bias = jnp.where(cvalid, 0.0, -jnp.inf)
    cmp_probs = []
    for g in range(G):
        psum = jnp.zeros((N_CHUNK, T), F32)
        probs = []
        for h in range(HG):
            sm = cmp_scores[g][:, h * T:(h + 1) * T] + cmp_bias
            e = jnp.exp2(sm - jnp.maximum(jnp.max(sm, axis=0, keepdims=True), NEG))
            den = jnp.sum(e, axis=0, keepdims=True)
            p = e / jnp.where(den > 0.0, den, 1.0)
            psum = psum + p
            probs.append(p.astype(BF16))
        cmp_probs.append(jnp.concatenate(probs, axis=1))
        ps_scr[g, 0:PS_PAD, :] = jnp.zeros((PS_PAD, T), F32)
        ps_scr[g, PS_PAD:PS_PAD + N_CHUNK, :] = psum
    for g in range(G):
        emit(g, gate(0, g) * _dot(vct_ref[g], cmp_probs[g]), first=True)

    reset_state()
    sweep(kw_ref, vwt_ref, [qi], [causal_bias], scores=win_diag_scores)

    @pl.when(qi >= n_back)
    def _():
        sweep(kw_ref, vwt_ref, [qi - 1 - i for i in range(n_back)],
              [None] * (n_back - 1) + [window_edge_bias])

    @pl.when(qi < n_back)
    def _():
        sweep_range(kw_ref, vwt_ref, qi, None)
    emit_state(2)

    all_causal_fit = (qi * T + T - 1) // SLC_BLOCK + 1 <= SLC_TOPK

    @pl.when(all_causal_fit)
    def _():
        for g in range(G):
            sel_scr[g] = jnp.zeros((N_SLC, T), F32)

    @pl.when(jnp.logical_not(all_causal_fit))
    def _():
        ROWS = 8
        jrow = lax.broadcasted_iota(jnp.int32, (N_SLC, T), 0)
        cur = tpos >> 6
        causal_blk = jrow <= cur
        forced = (jrow == 0) | (causal_blk & ((cur - jrow) < N_LOCAL_BLOCKS))
        for g in range(G):
            tap = lambda k: ps_scr[g, pl.ds(PS_PAD + k, N_SLC, stride=4), :]
            imp = 0.5 * tap(-1) + tap(0) + tap(1) + tap(2) + 0.5 * tap(3)
            score = jnp.where(forced, FORCE, jnp.where(causal_blk, imp, NEG))
            parts = [score[r:r + ROWS] for r in range(0, N_SLC, ROWS)]
            ranks = [jnp.zeros((ROWS, T), F32) for _ in parts]
            for j2 in range(N_SLC):
                other = score[j2:j2 + 1, :]
                for i, part in enumerate(parts):
                    r0 = i * ROWS
                    if r0 + ROWS - 1 < j2:
                        beats = other > part
                    elif r0 > j2:
                        beats = other >= part
                    else:
                        beats = (other > part) | ((other == part) & (jrow[r0:r0 + ROWS] > j2))
                    ranks[i] = ranks[i] + jnp.where(beats, 1.0, 0.0)
            rank = jnp.concatenate(ranks, axis=0)
            sel_scr[g] = jnp.where(rank < SLC_TOPK, 0.0, -jnp.inf)

    def slc_bias(g, kt):
        half = K_TILE // 2
        top = jnp.broadcast_to(sel_scr[g, pl.ds(2 * kt, 1), :], (half, T))
        bot = jnp.broadcast_to(sel_scr[g, pl.ds(2 * kt + 1, 1), :], (half, T))
        causal = jnp.where((kt * K_TILE + krow) <= tpos, 0.0, -jnp.inf)
        return (jnp.concatenate([top, bot], axis=0) + causal).astype(BF16)

    reset_state()
    sweep_range(ks_ref, vst_ref, qi + 1, slc_bias)
    emit_state(1)

    o_ref[...] = ot_scr[...].T.astype(BF16)


def _nsa_attention(q, gates_t, k_cmp, v_cmp, ks, vst, kw, vwt):
    B, _, S, _ = ks.shape
    G, T = N_KV_GROUPS, Q_TILE
    NQ = N_HEADS * HEAD_DIM
    NT = S // K_TILE
    kc = k_cmp.reshape(B, G, N_CHUNK, HEAD_DIM).astype(BF16)
    vct = v_cmp.reshape(B, G, N_CHUNK, HEAD_DIM).transpose(0, 1, 3, 2).astype(BF16)
    per_b = lambda *shape: pl.BlockSpec((None,) + shape, lambda b, i: (b,) + (0,) * len(shape))
    return pl.pallas_call(
        _attn_kernel,
        grid=(B, S // T),
        in_specs=[pl.BlockSpec((None, G, None, HEAD_DIM, HEADS_PER_GROUP * T),
                               lambda b, i: (b, 0, i, 0, 0)),
                  pl.BlockSpec((None, 3 * N_HEADS, T), lambda b, i: (b, 0, i)),
                  per_b(G, N_CHUNK, HEAD_DIM), per_b(G, HEAD_DIM, N_CHUNK),
                  per_b(G, S, HEAD_DIM), per_b(G, NT, VT_ROWS, K_TILE),
                  per_b(G, S, HEAD_DIM), per_b(G, NT, VT_ROWS, K_TILE)],
        out_specs=pl.BlockSpec((None, T, NQ), lambda b, i: (b, i, 0)),
        out_shape=jax.ShapeDtypeStruct((B, S, NQ), BF16),
        scratch_shapes=[pltpu.VMEM((G, N_SLC, T), F32), pltpu.VMEM((G, PS_PAD + N_CHUNK, T), F32),
                        pltpu.VMEM((G, 1, HEADS_PER_GROUP * T), F32),
                        pltpu.VMEM((G, VT_ROWS, HEADS_PER_GROUP * T), F32),
                        pltpu.VMEM((NQ, T), F32)],
        compiler_params=_cparams(2),
        name="nsa_attention",
    )(q, gates_t, kc, vct, ks, vst, kw, vwt)


def kernel(x, a_norm, a_w_in, a_conv_w, a_conv_b, a_gate_w, a_gate_b, a_lambda, a_w_out,
           kv_norm, kv_w, k_norm, cmp_pos, cmp_w1, cmp_b1, cmp_w2, cmp_b2,
           b_norm, b_w_in, b_gate_b, q_norm, b_w_out, f_norm, f_w_in, f_w_out):
    B, S, D = x.shape
    assert D == D_MODEL and S == N_SLC * SLC_BLOCK and S == N_CHUNK * CMP_STRIDE
    n_a = a_norm.shape[0]
    n_b = b_norm.shape[0]
    h = x
    for i in range(n_a):
        h = _recurrent_block(h, a_norm[i], a_w_in[i], a_conv_w[i], a_conv_b[i], a_gate_w[i],
                             a_gate_b[i], a_lambda[i], a_w_out[i])
        h = _swiglu(h, f_norm[i], f_w_in[i], f_w_out[i])

    cos_t, sin_t = _rope_tables(jnp.arange(S))
    cos128, sin128 = jnp.tile(cos_t, (1, 2)), jnp.tile(sin_t, (1, 2))
    kvc, ks, vs, kw, vw = _kv_proj(h, kv_norm, kv_w, k_norm, cos128, sin128)
    cmp = _compress(kvc, cmp_pos, cmp_w1, cmp_b1, cmp_w2, cmp_b2, k_norm[0])
    for j in range(n_b):
        q, gates = _q_proj(h, b_norm[j], b_w_in[j], b_gate_b[j], q_norm[j], cos128, sin128)
        o = _nsa_attention(q, gates, cmp[0], cmp[1], ks, vs, kw, vw)
        layer = n_a + j
        h = _swiglu(h, f_norm[layer], f_w_in[layer], f_w_out[layer], attn=o, w_o=b_w_out[j])
    return h
```

```python
import functools

import jax
import jax.numpy as jnp
from jax import lax
from jax.experimental import pallas as pl
from jax.experimental.pallas import tpu as pltpu

F32 = jnp.float32
BF16 = jnp.bfloat16

D_MODEL = 1024
LRU_WIDTH = D_MODEL
LRU_HEADS = 8
LRU_BLOCK = LRU_WIDTH // LRU_HEADS
CONV_WIDTH = 4
LRU_C = 8.0
HEAD_DIM = 64
N_HEADS = D_MODEL // HEAD_DIM
N_KV_GROUPS = 4
HEADS_PER_GROUP = N_HEADS // N_KV_GROUPS
CMP_BLOCK = 32
CMP_STRIDE = 16
CMP_HIDDEN = 256
SLC_BLOCK = 64
SLC_TOPK = 16
N_LOCAL_BLOCKS = 2
WINDOW = 512
ROPE_THETA = 10000.0
FFN_HIDDEN = 2816
EPS = 1e-6
NEG = -1e30
FORCE = 1e30

LANES = 128
KV_WIDTH = N_KV_GROUPS * HEAD_DIM
Q_TILE = 128
K_TILE = 128
VT_ROWS = HEAD_DIM + 16
PROJ_ROWS = 128
VMEM_LIMIT = 56 * 1024 * 1024


def _cparams(n_axes):
    return pltpu.CompilerParams(dimension_semantics=("arbitrary",) * n_axes,
                                vmem_limit_bytes=VMEM_LIMIT)


def _const_spec(shape):
    nd = len(shape)
    return pl.BlockSpec(shape, lambda *_: (0,) * nd, pipeline_mode=pl.Buffered(1))


def _rms(x, g):
    ms = jnp.mean(x * x, axis=-1, keepdims=True)
    return x * lax.rsqrt(ms + EPS) * g


def _sigmoid(x):
    return 1.0 / (1.0 + jnp.exp(-x))


def _gelu_tanh(x):
    c = 0.7978845608028654
    return x * (0.5 * (1.0 + jnp.tanh(c * (x + 0.044715 * (x * x * x)))))


def _dot(a, b):
    return jnp.dot(a, b, preferred_element_type=F32)


def _head_mean_sq(x, ones_bd):
    sq = x * x
    hi = sq.astype(BF16)
    lo = (sq - hi.astype(F32)).astype(BF16)
    return (_dot(hi, ones_bd) + _dot(lo, ones_bd)) * (1.0 / HEAD_DIM)


def _rope_flat(x, cos_t, sin_t):
    width = x.shape[-1]
    lane = lax.broadcasted_iota(jnp.int32, x.shape, 1)
    upper = (lane & (HEAD_DIM // 2)) != 0
    partner = jnp.where(upper, pltpu.roll(x, HEAD_DIM // 2, 1),
                        pltpu.roll(x, width - HEAD_DIM // 2, 1))
    return x * cos_t + partner * sin_t


def _tile_lanes(t, width):
    reps = width // t.shape[-1]
    return t if reps == 1 else jnp.concatenate([t] * reps, axis=1)


SUBLANES = 8


def _segment_perm(ts):
    seg_len = ts // SUBLANES
    dst = jnp.arange(ts)
    src = (dst % SUBLANES) * seg_len + dst // SUBLANES
    return (src[:, None] == jnp.arange(ts)[None, :]).astype(BF16)


def _scan_segments(a, b, h_in):
    n_steps = a.shape[0] // SUBLANES
    vreg = lambda x, j: x[j * SUBLANES:(j + 1) * SUBLANES]
    h_loc, a_cum = [vreg(b, 0)], [vreg(a, 0)]
    for j in range(1, n_steps):
        h_loc.append(vreg(a, j) * h_loc[-1] + vreg(b, j))
        a_cum.append(vreg(a, j) * a_cum[-1])
    seg_a, seg_h = a_cum[-1], h_loc[-1]
    carry = [h_in]
    for s in range(SUBLANES):
        carry.append(seg_a[s:s + 1] * carry[-1] + seg_h[s:s + 1])
    enter = jnp.concatenate(carry[:SUBLANES], axis=0)
    h = jnp.concatenate([h_loc[j] + a_cum[j] * enter for j in range(n_steps)], axis=0)
    return h, carry[SUBLANES]


def _rec_kernel(x_ref, g_ref, perm_ref, perm_t_ref, win_ref, cw_ref, cb_ref, wg_ref, gb_ref,
                lam_ref, wout_ref, o_ref, tail, hcar):
    R = LRU_WIDTH
    sub_rows = perm_ref.shape[0]
    n_sub = x_ref.shape[0] // sub_rows
    n_steps = sub_rows // SUBLANES
    taps = CONV_WIDTH - 1

    @pl.when(pl.program_id(1) == 0)
    def _():
        tail[...] = jnp.zeros_like(tail)
        hcar[...] = jnp.zeros_like(hcar)

    x = x_ref[...]
    u = _rms(x, g_ref[...]).astype(BF16)
    def in_proj(t):
        ut = _dot(perm_ref[...], u[t * sub_rows:(t + 1) * sub_rows]).astype(BF16)
        return _dot(ut, win_ref[...])

    sub = lax.broadcasted_iota(jnp.int32, (SUBLANES, LRU_BLOCK), 0)
    softplus_neg = jnp.maximum(-lam_ref[...], 0.0) + jnp.log1p(jnp.exp(-jnp.abs(lam_ref[...])))
    log_a_scale = -LRU_C * softplus_neg
    prev_tail = tail[...]
    z_next = in_proj(0)
    for t in range(n_sub):
        z = z_next
        conv_out, gate_pre = [], []
        for hh in range(LRU_HEADS):
            cs = slice(hh * LRU_BLOCK, (hh + 1) * LRU_BLOCK)
            xc = z[:, R + hh * LRU_BLOCK:R + (hh + 1) * LRU_BLOCK]
            wrapped = []
            for k in range(taps):
                cur = xc[(n_steps - taps + k) * SUBLANES:(n_steps - taps + k + 1) * SUBLANES]
                prev = prev_tail[k * SUBLANES:(k + 1) * SUBLANES, cs]
                wrapped.append(pltpu.roll(jnp.where(sub == SUBLANES - 1, prev, cur), 1, 0))
            back = lambda d: jnp.concatenate(
                wrapped[taps - d:] + [xc[0:(n_steps - d) * SUBLANES]], axis=0)
            cw = cw_ref[:, cs]
            xr = cb_ref[:, cs] + back(3) * cw[0:1]
            xr = xr + back(2) * cw[1:2]
            xr = xr + back(1) * cw[2:3]
            xr = xr + xc * cw[3:4]
            conv_out.append(xr)
            gate_pre.append(_dot(xr.astype(BF16), wg_ref[hh]))
        if t + 1 < n_sub:
            z_next = in_proj(t + 1)
        gated = []
        for hh in range(LRU_HEADS):
            cs = slice(hh * LRU_BLOCK, (hh + 1) * LRU_BLOCK)
            xr, gates = conv_out[hh], gate_pre[hh]
            gb = gb_ref[:, cs]
            r = _sigmoid(gates[:, :LRU_BLOCK] + gb[0:1])
            i = _sigmoid(gates[:, LRU_BLOCK:] + gb[1:2])
            log_a = log_a_scale[:, cs] * r
            a = jnp.exp(log_a)
            bterm = jnp.sqrt(1.0 - a * a) * (i * xr)
            hs, hcar[:, cs] = _scan_segments(a, bterm, hcar[:, cs])
            gated.append((_gelu_tanh(z[:, cs]) * hs).astype(BF16))
        prev_tail = z[(n_steps - taps) * SUBLANES:, R:]
        yh = _dot(perm_t_ref[...], jnp.concatenate(gated, axis=1)).astype(BF16)
        rows = slice(t * sub_rows, (t + 1) * sub_rows)
        o_ref[rows, :] = x[rows] + _dot(yh, wout_ref[...])
    tail[...] = prev_tail


def _recurrent_block(h, norm_g, w_in, conv_w, conv_b, gate_w, gate_b, lam, w_out, ts=1024,
                     sub_rows=256):
    B, S, D = h.shape
    R = LRU_WIDTH
    wg = jnp.concatenate([gate_w[0], gate_w[1]], axis=-1).astype(BF16)
    perm = _segment_perm(sub_rows)
    return pl.pallas_call(
        _rec_kernel,
        grid=(B, S // ts),
        in_specs=[
            pl.BlockSpec((None, ts, D), lambda b, s: (b, s, 0)),
            _const_spec((1, D)),
            _const_spec((sub_rows, sub_rows)),
            _const_spec((sub_rows, sub_rows)),
            _const_spec((D, 2 * R)),
            _const_spec((CONV_WIDTH, R)),
            _const_spec((1, R)),
            _const_spec((LRU_HEADS, LRU_BLOCK, 2 * LRU_BLOCK)),
            _const_spec((2, R)),
            _const_spec((1, R)),
            _const_spec((R, D)),
        ],
        out_specs=pl.BlockSpec((None, ts, D), lambda b, s: (b, s, 0)),
        out_shape=jax.ShapeDtypeStruct((B, S, D), F32),
        scratch_shapes=[pltpu.VMEM(((CONV_WIDTH - 1) * SUBLANES, R), F32), pltpu.VMEM((1, R), F32)],
        compiler_params=_cparams(2),
        name="rglru_block",
    )(h, norm_g.reshape(1, D), perm, perm.T, w_in.astype(BF16), conv_w, conv_b.reshape(1, R), wg,
      gate_b, lam.reshape(1, R), w_out.astype(BF16))


FFN_CHUNK = FFN_HIDDEN // 2


def _ffn_body(x, g_ref, win_ref, wout_ref, o_ref):
    u = _rms(x, g_ref[...]).astype(BF16)

    acc = x
    for c in range(FFN_HIDDEN // FFN_CHUNK):
        lo = c * FFN_CHUNK
        gate = _dot(u, win_ref[:, lo:lo + FFN_CHUNK])
        up = _dot(u, win_ref[:, FFN_HIDDEN + lo:FFN_HIDDEN + lo + FFN_CHUNK])
        act = ((gate * _sigmoid(gate)) * up).astype(BF16)
        acc = acc + _dot(act, wout_ref[lo:lo + FFN_CHUNK, :])
    o_ref[...] = acc


def _ffn_kernel(x_ref, g_ref, win_ref, wout_ref, o_ref):
    _ffn_body(x_ref[...], g_ref, win_ref, wout_ref, o_ref)


def _proj_ffn_kernel(x_ref, a_ref, wo_ref, g_ref, win_ref, wout_ref, o_ref):
    _ffn_body(x_ref[...] + _dot(a_ref[...], wo_ref[...]), g_ref, win_ref, wout_ref, o_ref)


def _swiglu(h, norm_g, w_in, w_out, attn=None, w_o=None, tm=512):
    B, S, D = h.shape
    M = B * S
    row_spec = pl.BlockSpec((tm, D), lambda i: (i, 0))
    w_specs = [_const_spec((1, D)), _const_spec((D, 2 * FFN_HIDDEN)), _const_spec((FFN_HIDDEN, D))]
    w_args = (norm_g.reshape(1, D), w_in.astype(BF16), w_out.astype(BF16))
    if attn is None:
        kern, specs, args = _ffn_kernel, [row_spec] + w_specs, (h.reshape(M, D),) + w_args
    else:
        kern = _proj_ffn_kernel
        specs = [row_spec, row_spec, _const_spec((D, D))] + w_specs
        args = (h.reshape(M, D), attn.reshape(M, D), w_o.astype(BF16)) + w_args
    out = pl.pallas_call(
        kern,
        grid=(M // tm,),
        in_specs=specs,
        out_specs=row_spec,
        out_shape=jax.ShapeDtypeStruct((M, D), F32),
        compiler_params=_cparams(1),
        name="swiglu_ffn",
    )(*args)
    return out.reshape(B, S, D)


def _kv_kernel(x_ref, g_ref, w_ref, kn_ref, cos_ref, sin_ref, ones_ref,
               kvc_ref, ks_ref, vs_ref, kw_ref, vw_ref):
    W = KV_WIDTH
    T = PROJ_ROWS
    n_sub = x_ref.shape[0] // T
    u = _rms(x_ref[...], g_ref[...]).astype(BF16)
    ones_bd = ones_ref[...]
    project = lambda t: _dot(u[t * T:(t + 1) * T], w_ref[...])
    kv_next = project(0)
    for t in range(n_sub):
        kv = kv_next
        rows = slice(t * T, (t + 1) * T)
        part = lambda j: kv[:, j * W:(j + 1) * W]
        mean_sq = {j: _head_mean_sq(part(j), ones_bd) for j in (2, 4)}
        if t + 1 < n_sub:
            kv_next = project(t + 1)
        cos_t = _tile_lanes(cos_ref[rows, :], W)
        sin_t = _tile_lanes(sin_ref[rows, :], W)
        kvc_ref[rows, :] = kv[:, 0:2 * W]
        for j, gain, k_ref in ((2, kn_ref[1:2, :], ks_ref), (4, kn_ref[2:3, :], kw_ref)):
            k = part(j) * lax.rsqrt(mean_sq[j] + EPS) * gain
            k = _rope_flat(k, cos_t, sin_t).astype(BF16)
            for g in range(N_KV_GROUPS):
                k_ref[g, rows, :] = k[:, g * HEAD_DIM:(g + 1) * HEAD_DIM]
        for j, vt_ref in ((3, vs_ref), (5, vw_ref)):
            v_t = part(j).T
            for g in range(N_KV_GROUPS):
                for i in range(T // K_TILE):
                    tile = t * (T // K_TILE) + i
                    vt_ref[g, tile, 0:HEAD_DIM, :] = v_t[g * HEAD_DIM:(g + 1) * HEAD_DIM,
                                                         i * K_TILE:(i + 1) * K_TILE].astype(BF16)
                    vt_ref[g, tile, HEAD_DIM:VT_ROWS, :] = jnp.ones((VT_ROWS - HEAD_DIM, K_TILE), BF16)


def _rope_tables(pos):
    half = HEAD_DIM // 2
    freqs = jnp.power(ROPE_THETA, -jnp.arange(half, dtype=F32) / half)
    ang = pos.astype(F32)[:, None] * freqs[None, :]
    cos, sin = jnp.cos(ang), jnp.sin(ang)
    cos_t = jnp.concatenate([cos, cos], axis=-1)
    sin_t = jnp.concatenate([-sin, sin], axis=-1)
    return cos_t, sin_t


def _block_diag_ones(width):
    seg = jnp.arange(width) // HEAD_DIM
    return (seg[:, None] == seg[None, :]).astype(BF16)


def _kv_proj(h, kv_norm, kv_w, k_norm, cos128, sin128, ts=512):
    B, S, D = h.shape
    W = KV_WIDTH
    kn = jnp.tile(k_norm, (1, N_KV_GROUPS))
    G = N_KV_GROUPS
    row = lambda width: pl.BlockSpec((None, ts, width), lambda b, s: (b, s, 0))
    tab = pl.BlockSpec((ts, LANES), lambda b, s: (s, 0))
    key_spec = pl.BlockSpec((None, G, ts, HEAD_DIM), lambda b, s: (b, 0, s, 0))
    val_spec = pl.BlockSpec((None, G, ts // K_TILE, VT_ROWS, K_TILE), lambda b, s: (b, 0, s, 0, 0))
    flat = jax.ShapeDtypeStruct((B, S, 2 * W), F32)
    keys = jax.ShapeDtypeStruct((B, G, S, HEAD_DIM), BF16)
    vals = jax.ShapeDtypeStruct((B, G, S // K_TILE, VT_ROWS, K_TILE), BF16)
    return pl.pallas_call(
        _kv_kernel,
        grid=(B, S // ts),
        in_specs=[row(D), _const_spec((1, D)), _const_spec((D, 6 * W)), _const_spec((3, W)),
                  tab, tab, _const_spec((W, W))],
        out_specs=[row(2 * W), key_spec, val_spec, key_spec, val_spec],
        out_shape=[flat, keys, vals, keys, vals],
        compiler_params=_cparams(2),
        name="shared_kv_proj",
    )(h, kv_norm.reshape(1, D), kv_w.astype(BF16), kn, cos128, sin128, _block_diag_ones(W))


N_CHUNK = 128
CHUNK_W = CMP_STRIDE * HEAD_DIM


def _cmp_kernel(xa_ref, xb_ref, pos_ref, w1_ref, b1_ref, w2_ref, b2_ref, kn_ref, cos_ref, sin_ref, o_ref):
    groups = []
    for x_ref in (xa_ref, xb_ref):
        nth = [x_ref[pl.ds(r, N_CHUNK, stride=CMP_STRIDE), :] for r in range(CMP_STRIDE)]
        for j in range(LANES // HEAD_DIM):
            groups.append(jnp.concatenate([t[:, j * HEAD_DIM:(j + 1) * HEAD_DIM] for t in nth], axis=1))
    x = jnp.concatenate(groups, axis=0)
    rows = x.shape[0]
    ya = _dot((x + pos_ref[0:1, :]).astype(BF16), w1_ref[0:CHUNK_W, :])
    yb = _dot((x + pos_ref[1:2, :]).astype(BF16), w1_ref[CHUNK_W:2 * CHUNK_W, :])
    hid = _gelu_tanh(ya + pltpu.roll(yb, rows - 1, 0) + b1_ref[...])
    out = _dot(hid.astype(BF16), w2_ref[...]) + b2_ref[...]

    @pl.when(pl.program_id(0) == 0)
    def _():
        k = _rms(out, kn_ref[...])
        half = HEAD_DIM // 2
        partner = jnp.concatenate([k[:, half:], k[:, :half]], axis=1)
        o_ref[...] = k * cos_ref[...] + partner * sin_ref[...]

    @pl.when(pl.program_id(0) == 1)
    def _():
        o_ref[...] = out


def _compress(kvc, cmp_pos, cmp_w1, cmp_b1, cmp_w2, cmp_b2, k_norm0):
    B, S, _ = kvc.shape
    G = N_KV_GROUPS
    rows = G * N_CHUNK
    pos = cmp_pos.reshape(2, 2, CHUNK_W)
    cmp_last = jnp.arange(N_CHUNK) * CMP_STRIDE + CMP_BLOCK - 1
    cos_t, sin_t = _rope_tables(cmp_last)
    cos_t, sin_t = jnp.tile(cos_t, (G, 1)), jnp.tile(sin_t, (G, 1))
    per_kv = lambda *shape: pl.BlockSpec((None,) + shape, lambda k, b: (k,) + (0,) * len(shape))
    return pl.pallas_call(
        _cmp_kernel,
        grid=(2, B),
        in_specs=[pl.BlockSpec((None, S, LANES), lambda k, b: (b, 0, 2 * k)),
                  pl.BlockSpec((None, S, LANES), lambda k, b: (b, 0, 2 * k + 1)),
                  per_kv(2, CHUNK_W), per_kv(2 * CHUNK_W, CMP_HIDDEN), per_kv(1, CMP_HIDDEN),
                  per_kv(CMP_HIDDEN, HEAD_DIM), per_kv(1, HEAD_DIM),
                  _const_spec((1, HEAD_DIM)), _const_spec((rows, HEAD_DIM)),
                  _const_spec((rows, HEAD_DIM))],
        out_specs=pl.BlockSpec((None, None, rows, HEAD_DIM), lambda k, b: (k, b, 0, 0)),
        out_shape=jax.ShapeDtypeStruct((2, B, rows, HEAD_DIM), F32),
        compiler_params=_cparams(2),
        name="kv_compress",
    )(kvc, kvc, pos, cmp_w1.astype(BF16), cmp_b1.reshape(2, 1, CMP_HIDDEN), cmp_w2.astype(BF16),
      cmp_b2.reshape(2, 1, HEAD_DIM), k_norm0.reshape(1, HEAD_DIM), cos_t, sin_t)


GATE_PAD = LANES
LOG2_E = 1.4426950408889634
Q_SCALE = HEAD_DIM ** -0.5 * LOG2_E


def _q_kernel(x_ref, g_ref, w_ref, gb_ref, qn_ref, cos_ref, sin_ref, ones_ref, q_ref, gate_ref):
    NQ = N_HEADS * HEAD_DIM
    T = PROJ_ROWS
    n_sub = x_ref.shape[0] // T
    u = _rms(x_ref[...], g_ref[...]).astype(BF16)
    ones_bd = ones_ref[...]
    W = ones_bd.shape[0]
    project = lambda t: _dot(u[t * T:(t + 1) * T], w_ref[...])
    z_next = project(0)
    for t in range(n_sub):
        z = z_next
        rows = slice(t * T, (t + 1) * T)
        chunks = [z[:, c * W:(c + 1) * W] for c in range(NQ // W)]
        mean_sq = [_head_mean_sq(q, ones_bd) for q in chunks]
        if t + 1 < n_sub:
            z_next = project(t + 1)
        gate_ref[:, rows] = _sigmoid(z[:, NQ:] + gb_ref[...]).T[0:3 * N_HEADS, :]
        cos_t = _tile_lanes(cos_ref[rows, :], W)
        sin_t = _tile_lanes(sin_ref[rows, :], W)
        for c, q in enumerate(chunks):
            q = q * lax.rsqrt(mean_sq[c] + EPS) * qn_ref[...]
            q_tr = (_rope_flat(q, cos_t, sin_t) * Q_SCALE).T
            for i in range(T // Q_TILE):
                q_ref[c, t * (T // Q_TILE) + i] = jnp.concatenate(
                    [q_tr[j * HEAD_DIM:(j + 1) * HEAD_DIM, i * Q_TILE:(i + 1) * Q_TILE]
                     for j in range(W // HEAD_DIM)], axis=1).astype(BF16)


def _q_proj(h, norm_g, w_in, gate_b, q_norm_g, cos128, sin128, ts=512):
    B, S, D = h.shape
    NQ = N_HEADS * HEAD_DIM
    n_gate = 3 * N_HEADS
    w = jnp.pad(w_in, ((0, 0), (0, GATE_PAD - n_gate))).astype(BF16)
    gb = jnp.pad(gate_b, (0, GATE_PAD - n_gate)).reshape(1, GATE_PAD)
    W = KV_WIDTH
    qn = jnp.tile(q_norm_g, W // HEAD_DIM).reshape(1, W)
    row = lambda width: pl.BlockSpec((None, ts, width), lambda b, s: (b, s, 0))
    tab = pl.BlockSpec((ts, LANES), lambda b, s: (s, 0))
    return pl.pallas_call(
        _q_kernel,
        grid=(B, S // ts),
        in_specs=[row(D), _const_spec((1, D)), _const_spec((D, NQ + GATE_PAD)),
                  _const_spec((1, GATE_PAD)), _const_spec((1, W)), tab, tab, _const_spec((W, W))],
        out_specs=[pl.BlockSpec((None, N_KV_GROUPS, ts // Q_TILE, HEAD_DIM, HEADS_PER_GROUP * Q_TILE),
                                lambda b, s: (b, 0, s, 0, 0)),
                   pl.BlockSpec((None, n_gate, ts), lambda b, s: (b, 0, s))],
        out_shape=[jax.ShapeDtypeStruct((B, N_KV_GROUPS, S // Q_TILE, HEAD_DIM,
                                         HEADS_PER_GROUP * Q_TILE), BF16),
                   jax.ShapeDtypeStruct((B, n_gate, S), F32)],
        compiler_params=_cparams(2),
        name="nsa_q_proj",
    )(h, norm_g.reshape(1, D), w, gb, qn, cos128, sin128, _block_diag_ones(W))


N_SLC = 32
PS_PAD = 8
SCORE_PAIRS_AHEAD = 2


def _flash_step(slot, parts, v_t, m_scr, acc_scr):
    T = Q_TILE
    m_old = m_scr[slot]
    m_new, p_all = [], []
    for h in range(HEADS_PER_GROUP):
        hs = slice(h * T, (h + 1) * T)
        sm = [s[:, hs].astype(BF16) if bias is None else s[:, hs].astype(BF16) + bias
              for s, bias in parts]
        m_h = m_old[:, hs]
        for x in sm:
            m_h = jnp.maximum(m_h, jnp.max(x, axis=0, keepdims=True).astype(F32))
        m_new.append(m_h)
        p_all.append(jnp.concatenate([jnp.exp2(x - m_h.astype(BF16)) for x in sm], axis=0))
    m_new = jnp.concatenate(m_new, axis=1)
    alpha = jnp.exp2(m_old - m_new)
    m_scr[slot] = m_new
    acc_scr[slot] = alpha * acc_scr[slot] + _dot(v_t, jnp.concatenate(p_all, axis=1))


def _attn_kernel(q_ref, gt_ref, kc_ref, vct_ref, ks_ref, vst_ref, kw_ref, vwt_ref, o_ref,
                 sel_scr, ps_scr, m_scr, acc_scr, ot_scr):
    T = Q_TILE
    G = N_KV_GROUPS
    HG = HEADS_PER_GROUP
    NL = HG * T
    qi = pl.program_id(1)
    tpos = qi * T + lax.broadcasted_iota(jnp.int32, (1, T), 1)
    krow = lax.broadcasted_iota(jnp.int32, (K_TILE, T), 0)
    q_t = lambda g: q_ref[g]
    key_tile = lambda k_ref, g, kt: k_ref[g, pl.ds(pl.multiple_of(kt * K_TILE, K_TILE), K_TILE), :]

    def gate(branch, g):
        r0 = branch * N_HEADS + HG * g
        return jnp.concatenate([gt_ref[r0 + h:r0 + h + 1, :] for h in range(HG)], axis=1)

    def emit(g, o_t, first):
        for h in range(HG):
            rows = slice((HG * g + h) * HEAD_DIM, (HG * g + h + 1) * HEAD_DIM)
            piece = o_t[:, h * T:(h + 1) * T]
            ot_scr[rows, :] = piece if first else ot_scr[rows, :] + piece

    def reset_state():
        for g in range(G):
            m_scr[g] = jnp.full((1, NL), NEG, F32)
            acc_scr[g] = jnp.zeros((VT_ROWS, NL), F32)

    def emit_state(branch):
        for g in range(G):
            denom = acc_scr[g, HEAD_DIM:HEAD_DIM + 1, :]
            emit(g, acc_scr[g, 0:HEAD_DIM, :] * (gate(branch, g) / denom), first=False)

    def sweep(k_ref, vt_ref, tiles, biases, scores=None):
        pairs = [range(lo, min(lo + 2, len(tiles))) for lo in range(0, len(tiles), 2)]
        scores = dict(scores or {})

        def score_matmuls(pair, g):
            for i in pair:
                if (i, g) not in scores:
                    scores[i, g] = _dot(key_tile(k_ref, g, tiles[i]), q_t(g))

        for pair in pairs[:SCORE_PAIRS_AHEAD]:
            for g in range(G):
                score_matmuls(pair, g)
        for n, pair in enumerate(pairs):
            for g in range(G):
                parts = [(scores[i, g], None if biases[i] is None else biases[i](g)) for i in pair]
                v_t = jnp.concatenate([vt_ref[g, tiles[i]] for i in pair], axis=1)
                _flash_step(g, parts, v_t, m_scr, acc_scr)
                if n + SCORE_PAIRS_AHEAD < len(pairs):
                    score_matmuls(pairs[n + SCORE_PAIRS_AHEAD], g)

    def sweep_range(k_ref, vt_ref, n_tiles, bias_of):
        def run(tiles):
            sweep(k_ref, vt_ref, tiles,
                  [None if bias_of is None else functools.partial(bias_of, kt=kt) for kt in tiles])

        def quad(j, c):
            run([4 * j + i for i in range(4)])
            return c

        lax.fori_loop(0, n_tiles // 4, quad, 0)
        rem = n_tiles % 4
        base = n_tiles - rem

        @pl.when(rem >= 2)
        def _():
            run([base, base + 1])

        @pl.when(rem % 2 == 1)
        def _():
            run([n_tiles - 1])

    n_back = WINDOW // K_TILE

    def causal_bias(g):
        return jnp.where((qi * K_TILE + krow) <= tpos, 0.0, -jnp.inf).astype(BF16)

    def window_edge_bias(g):
        return jnp.where(((qi - n_back) * K_TILE + krow) > (tpos - WINDOW), 0.0,
                         -jnp.inf).astype(BF16)

    def compressed_and_window(tiles, biases):
        cmp_scores = [_dot(kc_ref[g], q_t(g)) for g in range(G)]
        ahead = {(i, g): _dot(key_tile(kw_ref, g, tiles[i]), q_t(g))
                 for i in range(min(len(tiles), 2 * SCORE_PAIRS_AHEAD)) for g in range(G)}
        cvalid = ((krow * CMP_STRIDE + (CMP_BLOCK - 1)) <= tpos) & (krow < N_CHUNK - 1)
        cmp_bias = jnp.where(cvalid, 0.0, -jnp.inf)
        cmp_probs = []
        for g in range(G):
            psum = jnp.zeros((N_CHUNK, T), F32)
            probs = []
            for h in range(HG):
                sm = cmp_scores[g][:, h * T:(h + 1) * T] + cmp_bias
                e = jnp.exp2(sm - jnp.maximum(jnp.max(sm, axis=0, keepdims=True), NEG))
                den = jnp.sum(e, axis=0, keepdims=True)
                p = e / jnp.where(den > 0.0, den, 1.0)
                psum = psum + p
                probs.append(p.astype(BF16))
            cmp_probs.append(jnp.concatenate(probs, axis=1))
            ps_scr[g, 0:PS_PAD, :] = jnp.zeros((PS_PAD, T), F32)
            ps_scr[g, PS_PAD:PS_PAD + N_CHUNK, :] = psum
        for g in range(G):
            emit(g, gate(0, g) * _dot(vct_ref[g], cmp_probs[g]), first=True)
        reset_state()
        sweep(kw_ref, vwt_ref, tiles, biases, scores=ahead)

    @pl.when(qi >= n_back)
    def _():
        compressed_and_window([qi - i for i in range(n_back + 1)],
                              [causal_bias] + [None] * (n_back - 1) + [window_edge_bias])

    @pl.when(qi < n_back)
    def _():
        compressed_and_window([qi], [causal_bias])
        sweep_range(kw_ref, vwt_ref, qi, None)
    emit_state(2)

    all_causal_fit = (qi * T + T - 1) // SLC_BLOCK + 1 <= SLC_TOPK

    @pl.when(all_causal_fit)
    def _():
        for g in range(G):
            sel_scr[g] = jnp.zeros((N_SLC, T), F32)

    @pl.when(jnp.logical_not(all_causal_fit))
    def _():
        ROWS = 8
        jrow = lax.broadcasted_iota(jnp.int32, (N_SLC, T), 0)
        cur = tpos // SLC_BLOCK
        causal_blk = jrow <= cur
        forced = (jrow == 0) | (causal_blk & ((cur - jrow) < N_LOCAL_BLOCKS))
        for g in range(G):
            tap = lambda k: ps_scr[g, pl.ds(PS_PAD + k, N_SLC, stride=4), :]
            imp = 0.5 * tap(-1) + tap(0) + tap(1) + tap(2) + 0.5 * tap(3)
            score = jnp.where(forced, FORCE, jnp.where(causal_blk, imp, NEG))
            parts = [score[r:r + ROWS] for r in range(0, N_SLC, ROWS)]
            ranks = [jnp.zeros((ROWS, T), F32) for _ in parts]
            for j2 in range(N_SLC):
                other = score[j2:j2 + 1, :]
                for i, part in enumerate(parts):
                    r0 = i * ROWS
                    if r0 + ROWS - 1 < j2:
                        beats = other > part
                    elif r0 > j2:
                        beats = other >= part
                    else:
                        beats = (other > part) | ((other == part) & (jrow[r0:r0 + ROWS] > j2))
                    ranks[i] = ranks[i] + jnp.where(beats, 1.0, 0.0)
            rank = jnp.concatenate(ranks, axis=0)
            sel_scr[g] = jnp.where(rank < SLC_TOPK, 0.0, -jnp.inf)

    def slc_bias(g, kt):
        half = K_TILE // 2
        top = jnp.broadcast_to(sel_scr[g, pl.ds(2 * kt, 1), :], (half, T))
        bot = jnp.broadcast_to(sel_scr[g, pl.ds(2 * kt + 1, 1), :], (half, T))
        causal = jnp.where((kt * K_TILE + krow) <= tpos, 0.0, -jnp.inf)
        return (jnp.concatenate([top, bot], axis=0) + causal).astype(BF16)

    reset_state()
    sweep_range(ks_ref, vst_ref, qi + 1, slc_bias)
    emit_state(1)

    o_ref[...] = ot_scr[...].T.astype(BF16)


def _nsa_attention(q, gates_t, k_cmp, v_cmp, ks, vst, kw, vwt):
    B, _, S, _ = ks.shape
    G, T = N_KV_GROUPS, Q_TILE
    NQ = N_HEADS * HEAD_DIM
    NT = S // K_TILE
    kc = k_cmp.reshape(B, G, N_CHUNK, HEAD_DIM).astype(BF16)
    vct = v_cmp.reshape(B, G, N_CHUNK, HEAD_DIM).transpose(0, 1, 3, 2).astype(BF16)
    per_b = lambda *shape: pl.BlockSpec((None,) + shape, lambda b, i: (b,) + (0,) * len(shape))
    return pl.pallas_call(
        _attn_kernel,
        grid=(B, S // T),
        in_specs=[pl.BlockSpec((None, G, None, HEAD_DIM, HEADS_PER_GROUP * T),
                               lambda b, i: (b, 0, i, 0, 0)),
                  pl.BlockSpec((None, 3 * N_HEADS, T), lambda b, i: (b, 0, i)),
                  per_b(G, N_CHUNK, HEAD_DIM), per_b(G, HEAD_DIM, N_CHUNK),
                  per_b(G, S, HEAD_DIM), per_b(G, NT, VT_ROWS, K_TILE),
                  per_b(G, S, HEAD_DIM), per_b(G, NT, VT_ROWS, K_TILE)],
        out_specs=pl.BlockSpec((None, T, NQ), lambda b, i: (b, i, 0)),
        out_shape=jax.ShapeDtypeStruct((B, S, NQ), BF16),
        scratch_shapes=[pltpu.VMEM((G, N_SLC, T), F32), pltpu.VMEM((G, PS_PAD + N_CHUNK, T), F32),
                        pltpu.VMEM((G, 1, HEADS_PER_GROUP * T), F32),
                        pltpu.VMEM((G, VT_ROWS, HEADS_PER_GROUP * T), F32),
                        pltpu.VMEM((NQ, T), F32)],
        compiler_params=_cparams(2),
        name="nsa_attention",
    )(q, gates_t, kc, vct, ks, vst, kw, vwt)


def kernel(x, a_norm, a_w_in, a_conv_w, a_conv_b, a_gate_w, a_gate_b, a_lambda, a_w_out,
           kv_norm, kv_w, k_norm, cmp_pos, cmp_w1, cmp_b1, cmp_w2, cmp_b2,
           b_norm, b_w_in, b_gate_b, q_norm, b_w_out, f_norm, f_w_in, f_w_out):
    B, S, D = x.shape
    assert D == D_MODEL and S == N_SLC * SLC_BLOCK and S == N_CHUNK * CMP_STRIDE
    n_a = a_norm.shape[0]
    n_b = b_norm.shape[0]
    h = x
    for i in range(n_a):
        h = _recurrent_block(h, a_norm[i], a_w_in[i], a_conv_w[i], a_conv_b[i], a_gate_w[i],
                             a_gate_b[i], a_lambda[i], a_w_out[i])
        h = _swiglu(h, f_norm[i], f_w_in[i], f_w_out[i])

    cos_t, sin_t = _rope_tables(jnp.arange(S))
    cos128, sin128 = jnp.tile(cos_t, (1, 2)), jnp.tile(sin_t, (1, 2))
    kvc, ks, vs, kw, vw = _kv_proj(h, kv_norm, kv_w, k_norm, cos128, sin128)
    cmp = _compress(kvc, cmp_pos, cmp_w1, cmp_b1, cmp_w2, cmp_b2, k_norm[0])
    for j in range(n_b):
        q, gates = _q_proj(h, b_norm[j], b_w_in[j], b_gate_b[j], q_norm[j], cos128, sin128)
        o = _nsa_attention(q, gates, cmp[0], cmp[1], ks, vs, kw, vw)
        layer = n_a + j
        h = _swiglu(h, f_norm[layer], f_w_in[layer], f_w_out[layer], attn=o, w_o=b_w_out[j])
    return h
```

```python
import functools

import jax
import jax.numpy as jnp
from jax import lax
from jax.experimental import pallas as pl
from jax.experimental.pallas import tpu as pltpu

F32 = jnp.float32
BF16 = jnp.bfloat16

D_MODEL = 1024
LRU_WIDTH = D_MODEL
LRU_HEADS = 8
LRU_BLOCK = LRU_WIDTH // LRU_HEADS
CONV_WIDTH = 4
LRU_C = 8.0
HEAD_DIM = 64
N_HEADS = D_MODEL // HEAD_DIM
N_KV_GROUPS = 4
HEADS_PER_GROUP = N_HEADS // N_KV_GROUPS
CMP_BLOCK = 32
CMP_STRIDE = 16
CMP_HIDDEN = 256
SLC_BLOCK = 64
SLC_TOPK = 16
N_LOCAL_BLOCKS = 2
WINDOW = 512
ROPE_THETA = 10000.0
FFN_HIDDEN = 2816
EPS = 1e-6
NEG = -1e30
FORCE = 1e30

LANES = 128
KV_WIDTH = N_KV_GROUPS * HEAD_DIM
Q_TILE = 128
K_TILE = 128
VT_ROWS = HEAD_DIM + 16
PROJ_ROWS = 128
VMEM_LIMIT = 56 * 1024 * 1024


def _cparams(n_axes):
    return pltpu.CompilerParams(dimension_semantics=("arbitrary",) * n_axes,
                                vmem_limit_bytes=VMEM_LIMIT)


def _const_spec(shape):
    nd = len(shape)
    return pl.BlockSpec(shape, lambda *_: (0,) * nd, pipeline_mode=pl.Buffered(1))


def _rms(x, g):
    ms = jnp.mean(x * x, axis=-1, keepdims=True)
    return x * lax.rsqrt(ms + EPS) * g


def _sigmoid(x):
    return 1.0 / (1.0 + jnp.exp(-x))


def _gelu_tanh(x):
    c = 0.7978845608028654
    return x * (0.5 * (1.0 + jnp.tanh(c * (x + 0.044715 * (x * x * x)))))


def _dot(a, b):
    return jnp.dot(a, b, preferred_element_type=F32)


def _head_mean_sq(x, ones_bd):
    sq = x * x
    hi = sq.astype(BF16)
    lo = (sq - hi.astype(F32)).astype(BF16)
    return (_dot(hi, ones_bd) + _dot(lo, ones_bd)) * (1.0 / HEAD_DIM)


def _rope_flat(x, cos_t, sin_t):
    width = x.shape[-1]
    lane = lax.broadcasted_iota(jnp.int32, x.shape, 1)
    upper = (lane & (HEAD_DIM // 2)) != 0
    partner = jnp.where(upper, pltpu.roll(x, HEAD_DIM // 2, 1),
                        pltpu.roll(x, width - HEAD_DIM // 2, 1))
    return x * cos_t + partner * sin_t


def _tile_lanes(t, width):
    reps = width // t.shape[-1]
    return t if reps == 1 else jnp.concatenate([t] * reps, axis=1)


SUBLANES = 8


def _segment_perm(ts):
    seg_len = ts // SUBLANES
    dst = jnp.arange(ts)
    src = (dst % SUBLANES) * seg_len + dst // SUBLANES
    return (src[:, None] == jnp.arange(ts)[None, :]).astype(BF16)


def _scan_segments(a, b, h_in):
    n_steps = a.shape[0] // SUBLANES
    vreg = lambda x, j: x[j * SUBLANES:(j + 1) * SUBLANES]
    h_loc, a_cum = [vreg(b, 0)], [vreg(a, 0)]
    for j in range(1, n_steps):
        h_loc.append(vreg(a, j) * h_loc[-1] + vreg(b, j))
        a_cum.append(vreg(a, j) * a_cum[-1])
    seg_a, seg_h = a_cum[-1], h_loc[-1]
    carry = [h_in]
    for s in range(SUBLANES):
        carry.append(seg_a[s:s + 1] * carry[-1] + seg_h[s:s + 1])
    enter = jnp.concatenate(carry[:SUBLANES], axis=0)
    h = jnp.concatenate([h_loc[j] + a_cum[j] * enter for j in range(n_steps)], axis=0)
    return h, carry[SUBLANES]


def _rec_kernel(x_ref, g_ref, perm_ref, perm_t_ref, win_ref, cw_ref, cb_ref, wg_ref, gb_ref,
                lam_ref, wout_ref, o_ref, tail, hcar):
    R = LRU_WIDTH
    sub_rows = perm_ref.shape[0]
    n_sub = x_ref.shape[0] // sub_rows
    n_steps = sub_rows // SUBLANES
    taps = CONV_WIDTH - 1

    @pl.when(pl.program_id(1) == 0)
    def _():
        tail[...] = jnp.zeros_like(tail)
        hcar[...] = jnp.zeros_like(hcar)

    x = x_ref[...]
    u = _rms(x, g_ref[...]).astype(BF16)
    def in_proj(t):
        ut = _dot(perm_ref[...], u[t * sub_rows:(t + 1) * sub_rows]).astype(BF16)
        return _dot(ut, win_ref[...])

    sub = lax.broadcasted_iota(jnp.int32, (SUBLANES, LRU_BLOCK), 0)
    softplus_neg = jnp.maximum(-lam_ref[...], 0.0) + jnp.log1p(jnp.exp(-jnp.abs(lam_ref[...])))
    log_a_scale = -LRU_C * softplus_neg

    def conv_and_gates(z, prev_tail):
        conv_out, gate_pre = [], []
        for hh in range(LRU_HEADS):
            cs = slice(hh * LRU_BLOCK, (hh + 1) * LRU_BLOCK)
            xc = z[:, R + hh * LRU_BLOCK:R + (hh + 1) * LRU_BLOCK]
            wrapped = []
            for k in range(taps):
                cur = xc[(n_steps - taps + k) * SUBLANES:(n_steps - taps + k + 1) * SUBLANES]
                prev = prev_tail[k * SUBLANES:(k + 1) * SUBLANES, cs]
                wrapped.append(pltpu.roll(jnp.where(sub == SUBLANES - 1, prev, cur), 1, 0))
            back = lambda d: jnp.concatenate(
                wrapped[taps - d:] + [xc[0:(n_steps - d) * SUBLANES]], axis=0)
            cw = cw_ref[:, cs]
            xr = cb_ref[:, cs] + back(3) * cw[0:1]
            xr = xr + back(2) * cw[1:2]
            xr = xr + back(1) * cw[2:3]
            xr = xr + xc * cw[3:4]
            conv_out.append(xr)
            gate_pre.append(_dot(xr.astype(BF16), wg_ref[hh]))
        return conv_out, gate_pre

    def recur_and_gate(z, conv_out, gate_pre):
        gated = []
        for hh in range(LRU_HEADS):
            cs = slice(hh * LRU_BLOCK, (hh + 1) * LRU_BLOCK)
            xr, gates = conv_out[hh], gate_pre[hh]
            gb = gb_ref[:, cs]
            r = _sigmoid(gates[:, :LRU_BLOCK] + gb[0:1])
            i = _sigmoid(gates[:, LRU_BLOCK:] + gb[1:2])
            log_a = log_a_scale[:, cs] * r
            a = jnp.exp(log_a)
            bterm = jnp.sqrt(1.0 - a * a) * (i * xr)
            hs, hcar[:, cs] = _scan_segments(a, bterm, hcar[:, cs])
            gated.append((_gelu_tanh(z[:, cs]) * hs).astype(BF16))
        return jnp.concatenate(gated, axis=1)

    last_rows = slice((n_steps - taps) * SUBLANES, None)
    zs = {0: in_proj(0)}
    if n_sub > 1:
        zs[1] = in_proj(1)
    staged = conv_and_gates(zs[0], tail[...])
    for t in range(n_sub):
        gated = recur_and_gate(zs[t], *staged)
        if t + 1 < n_sub:
            staged = conv_and_gates(zs[t + 1], zs[t][last_rows, R:])
        yh = _dot(perm_t_ref[...], gated).astype(BF16)
        rows = slice(t * sub_rows, (t + 1) * sub_rows)
        o_ref[rows, :] = x[rows] + _dot(yh, wout_ref[...])
        if t + 2 < n_sub:
            zs[t + 2] = in_proj(t + 2)
    tail[...] = zs[n_sub - 1][last_rows, R:]


def _recurrent_block(h, norm_g, w_in, conv_w, conv_b, gate_w, gate_b, lam, w_out, ts=1024,
                     sub_rows=256):
    B, S, D = h.shape
    R = LRU_WIDTH
    wg = jnp.concatenate([gate_w[0], gate_w[1]], axis=-1).astype(BF16)
    perm = _segment_perm(sub_rows)
    return pl.pallas_call(
        _rec_kernel,
        grid=(B, S // ts),
        in_specs=[
            pl.BlockSpec((None, ts, D), lambda b, s: (b, s, 0)),
            _const_spec((1, D)),
            _const_spec((sub_rows, sub_rows)),
            _const_spec((sub_rows, sub_rows)),
            _const_spec((D, 2 * R)),
            _const_spec((CONV_WIDTH, R)),
            _const_spec((1, R)),
            _const_spec((LRU_HEADS, LRU_BLOCK, 2 * LRU_BLOCK)),
            _const_spec((2, R)),
            _const_spec((1, R)),
            _const_spec((R, D)),
        ],
        out_specs=pl.BlockSpec((None, ts, D), lambda b, s: (b, s, 0)),
        out_shape=jax.ShapeDtypeStruct((B, S, D), F32),
        scratch_shapes=[pltpu.VMEM(((CONV_WIDTH - 1) * SUBLANES, R), F32), pltpu.VMEM((1, R), F32)],
        compiler_params=_cparams(2),
        name="rglru_block",
    )(h, norm_g.reshape(1, D), perm, perm.T, w_in.astype(BF16), conv_w, conv_b.reshape(1, R), wg,
      gate_b, lam.reshape(1, R), w_out.astype(BF16))


FFN_CHUNK = FFN_HIDDEN // 2


def _ffn_body(x, g_ref, win_ref, wout_ref, o_ref):
    u = _rms(x, g_ref[...]).astype(BF16)

    acc = x
    for c in range(FFN_HIDDEN // FFN_CHUNK):
        lo = c * FFN_CHUNK
        gate = _dot(u, win_ref[:, lo:lo + FFN_CHUNK])
        up = _dot(u, win_ref[:, FFN_HIDDEN + lo:FFN_HIDDEN + lo + FFN_CHUNK])
        act = ((gate * _sigmoid(gate)) * up).astype(BF16)
        acc = acc + _dot(act, wout_ref[lo:lo + FFN_CHUNK, :])
    o_ref[...] = acc


def _ffn_kernel(x_ref, g_ref, win_ref, wout_ref, o_ref):
    _ffn_body(x_ref[...], g_ref, win_ref, wout_ref, o_ref)


def _proj_ffn_kernel(x_ref, a_ref, wo_ref, g_ref, win_ref, wout_ref, o_ref):
    _ffn_body(x_ref[...] + _dot(a_ref[...], wo_ref[...]), g_ref, win_ref, wout_ref, o_ref)


def _swiglu(h, norm_g, w_in, w_out, attn=None, w_o=None, tm=512):
    B, S, D = h.shape
    M = B * S
    row_spec = pl.BlockSpec((tm, D), lambda i: (i, 0))
    w_specs = [_const_spec((1, D)), _const_spec((D, 2 * FFN_HIDDEN)), _const_spec((FFN_HIDDEN, D))]
    w_args = (norm_g.reshape(1, D), w_in.astype(BF16), w_out.astype(BF16))
    if attn is None:
        kern, specs, args = _ffn_kernel, [row_spec] + w_specs, (h.reshape(M, D),) + w_args
    else:
        kern = _proj_ffn_kernel
        specs = [row_spec, row_spec, _const_spec((D, D))] + w_specs
        args = (h.reshape(M, D), attn.reshape(M, D), w_o.astype(BF16)) + w_args
    out = pl.pallas_call(
        kern,
        grid=(M // tm,),
        in_specs=specs,
        out_specs=row_spec,
        out_shape=jax.ShapeDtypeStruct((M, D), F32),
        compiler_params=_cparams(1),
        name="swiglu_ffn",
    )(*args)
    return out.reshape(B, S, D)


def _kv_kernel(x_ref, g_ref, w_ref, kn_ref, cos_ref, sin_ref, ones_ref,
               kvc_ref, ks_ref, vs_ref, kw_ref, vw_ref):
    W = KV_WIDTH
    T = PROJ_ROWS
    n_sub = x_ref.shape[0] // T
    u = _rms(x_ref[...], g_ref[...]).astype(BF16)
    ones_bd = ones_ref[...]
    project = lambda t: _dot(u[t * T:(t + 1) * T], w_ref[...])
    kv_next = project(0)
    for t in range(n_sub):
        kv = kv_next
        rows = slice(t * T, (t + 1) * T)
        part = lambda j: kv[:, j * W:(j + 1) * W]
        mean_sq = {j: _head_mean_sq(part(j), ones_bd) for j in (2, 4)}
        if t + 1 < n_sub:
            kv_next = project(t + 1)
        cos_t = _tile_lanes(cos_ref[rows, :], W)
        sin_t = _tile_lanes(sin_ref[rows, :], W)
        kvc_ref[rows, :] = kv[:, 0:2 * W]
        for j, gain, k_ref in ((2, kn_ref[1:2, :], ks_ref), (4, kn_ref[2:3, :], kw_ref)):
            k = part(j) * lax.rsqrt(mean_sq[j] + EPS) * gain
            k = _rope_flat(k, cos_t, sin_t).astype(BF16)
            for g in range(N_KV_GROUPS):
                k_ref[g, rows, :] = k[:, g * HEAD_DIM:(g + 1) * HEAD_DIM]
        for j, vt_ref in ((3, vs_ref), (5, vw_ref)):
            v_t = part(j).T
            for g in range(N_KV_GROUPS):
                for i in range(T // K_TILE):
                    tile = t * (T // K_TILE) + i
                    vt_ref[g, tile, 0:HEAD_DIM, :] = v_t[g * HEAD_DIM:(g + 1) * HEAD_DIM,
                                                         i * K_TILE:(i + 1) * K_TILE].astype(BF16)
                    vt_ref[g, tile, HEAD_DIM:VT_ROWS, :] = jnp.ones((VT_ROWS - HEAD_DIM, K_TILE), BF16)


def _rope_tables(pos):
    half = HEAD_DIM // 2
    freqs = jnp.power(ROPE_THETA, -jnp.arange(half, dtype=F32) / half)
    ang = pos.astype(F32)[:, None] * freqs[None, :]
    cos, sin = jnp.cos(ang), jnp.sin(ang)
    cos_t = jnp.concatenate([cos, cos], axis=-1)
    sin_t = jnp.concatenate([-sin, sin], axis=-1)
    return cos_t, sin_t


def _block_diag_ones(width):
    seg = jnp.arange(width) // HEAD_DIM
    return (seg[:, None] == seg[None, :]).astype(BF16)


def _kv_proj(h, kv_norm, kv_w, k_norm, cos128, sin128, ts=512):
    B, S, D = h.shape
    W = KV_WIDTH
    kn = jnp.tile(k_norm, (1, N_KV_GROUPS))
    G = N_KV_GROUPS
    row = lambda width: pl.BlockSpec((None, ts, width), lambda b, s: (b, s, 0))
    tab = pl.BlockSpec((ts, LANES), lambda b, s: (s, 0))
    key_spec = pl.BlockSpec((None, G, ts, HEAD_DIM), lambda b, s: (b, 0, s, 0))
    val_spec = pl.BlockSpec((None, G, ts // K_TILE, VT_ROWS, K_TILE), lambda b, s: (b, 0, s, 0, 0))
    flat = jax.ShapeDtypeStruct((B, S, 2 * W), F32)
    keys = jax.ShapeDtypeStruct((B, G, S, HEAD_DIM), BF16)
    vals = jax.ShapeDtypeStruct((B, G, S // K_TILE, VT_ROWS, K_TILE), BF16)
    return pl.pallas_call(
        _kv_kernel,
        grid=(B, S // ts),
        in_specs=[row(D), _const_spec((1, D)), _const_spec((D, 6 * W)), _const_spec((3, W)),
                  tab, tab, _const_spec((W, W))],
        out_specs=[row(2 * W), key_spec, val_spec, key_spec, val_spec],
        out_shape=[flat, keys, vals, keys, vals],
        compiler_params=_cparams(2),
        name="shared_kv_proj",
    )(h, kv_norm.reshape(1, D), kv_w.astype(BF16), kn, cos128, sin128, _block_diag_ones(W))


N_CHUNK = 128
CHUNK_W = CMP_STRIDE * HEAD_DIM


def _cmp_kernel(xa_ref, xb_ref, pos_ref, w1_ref, b1_ref, w2_ref, b2_ref, kn_ref, cos_ref, sin_ref, o_ref):
    groups = []
    for x_ref in (xa_ref, xb_ref):
        nth = [x_ref[pl.ds(r, N_CHUNK, stride=CMP_STRIDE), :] for r in range(CMP_STRIDE)]
        for j in range(LANES // HEAD_DIM):
            groups.append(jnp.concatenate([t[:, j * HEAD_DIM:(j + 1) * HEAD_DIM] for t in nth], axis=1))
    x = jnp.concatenate(groups, axis=0)
    rows = x.shape[0]
    ya = _dot((x + pos_ref[0:1, :]).astype(BF16), w1_ref[0:CHUNK_W, :])
    yb = _dot((x + pos_ref[1:2, :]).astype(BF16), w1_ref[CHUNK_W:2 * CHUNK_W, :])
    hid = _gelu_tanh(ya + pltpu.roll(yb, rows - 1, 0) + b1_ref[...])
    out = _dot(hid.astype(BF16), w2_ref[...]) + b2_ref[...]

    @pl.when(pl.program_id(0) == 0)
    def _():
        k = _rms(out, kn_ref[...])
        half = HEAD_DIM // 2
        partner = jnp.concatenate([k[:, half:], k[:, :half]], axis=1)
        o_ref[...] = k * cos_ref[...] + partner * sin_ref[...]

    @pl.when(pl.program_id(0) == 1)
    def _():
        o_ref[...] = out


def _compress(kvc, cmp_pos, cmp_w1, cmp_b1, cmp_w2, cmp_b2, k_norm0):
    B, S, _ = kvc.shape
    G = N_KV_GROUPS
    rows = G * N_CHUNK
    pos = cmp_pos.reshape(2, 2, CHUNK_W)
    cmp_last = jnp.arange(N_CHUNK) * CMP_STRIDE + CMP_BLOCK - 1
    cos_t, sin_t = _rope_tables(cmp_last)
    cos_t, sin_t = jnp.tile(cos_t, (G, 1)), jnp.tile(sin_t, (G, 1))
    per_kv = lambda *shape: pl.BlockSpec((None,) + shape, lambda k, b: (k,) + (0,) * len(shape))
    return pl.pallas_call(
        _cmp_kernel,
        grid=(2, B),
        in_specs=[pl.BlockSpec((None, S, LANES), lambda k, b: (b, 0, 2 * k)),
                  pl.BlockSpec((None, S, LANES), lambda k, b: (b, 0, 2 * k + 1)),
                  per_kv(2, CHUNK_W), per_kv(2 * CHUNK_W, CMP_HIDDEN), per_kv(1, CMP_HIDDEN),
                  per_kv(CMP_HIDDEN, HEAD_DIM), per_kv(1, HEAD_DIM),
                  _const_spec((1, HEAD_DIM)), _const_spec((rows, HEAD_DIM)),
                  _const_spec((rows, HEAD_DIM))],
        out_specs=pl.BlockSpec((None, None, rows, HEAD_DIM), lambda k, b: (k, b, 0, 0)),
        out_shape=jax.ShapeDtypeStruct((2, B, rows, HEAD_DIM), F32),
        compiler_params=_cparams(2),
        name="kv_compress",
    )(kvc, kvc, pos, cmp_w1.astype(BF16), cmp_b1.reshape(2, 1, CMP_HIDDEN), cmp_w2.astype(BF16),
      cmp_b2.reshape(2, 1, HEAD_DIM), k_norm0.reshape(1, HEAD_DIM), cos_t, sin_t)


GATE_PAD = LANES
LOG2_E = 1.4426950408889634
Q_SCALE = HEAD_DIM ** -0.5 * LOG2_E


def _q_kernel(x_ref, g_ref, w_ref, gb_ref, qn_ref, cos_ref, sin_ref, ones_ref, q_ref, gate_ref):
    NQ = N_HEADS * HEAD_DIM
    T = PROJ_ROWS
    n_sub = x_ref.shape[0] // T
    u = _rms(x_ref[...], g_ref[...]).astype(BF16)
    ones_bd = ones_ref[...]
    W = ones_bd.shape[0]
    project = lambda t: _dot(u[t * T:(t + 1) * T], w_ref[...])
    z_next = project(0)
    for t in range(n_sub):
        z = z_next
        rows = slice(t * T, (t + 1) * T)
        chunks = [z[:, c * W:(c + 1) * W] for c in range(NQ // W)]
        mean_sq = [_head_mean_sq(q, ones_bd) for q in chunks]
        if t + 1 < n_sub:
            z_next = project(t + 1)
        gate_ref[:, rows] = _sigmoid(z[:, NQ:] + gb_ref[...]).T[0:3 * N_HEADS, :]
        cos_t = _tile_lanes(cos_ref[rows, :], W)
        sin_t = _tile_lanes(sin_ref[rows, :], W)
        for c, q in enumerate(chunks):
            q = q * lax.rsqrt(mean_sq[c] + EPS) * qn_ref[...]
            q_tr = (_rope_flat(q, cos_t, sin_t) * Q_SCALE).T
            for i in range(T // Q_TILE):
                q_ref[c, t * (T // Q_TILE) + i] = jnp.concatenate(
                    [q_tr[j * HEAD_DIM:(j + 1) * HEAD_DIM, i * Q_TILE:(i + 1) * Q_TILE]
                     for j in range(W // HEAD_DIM)], axis=1).astype(BF16)


def _q_proj(h, norm_g, w_in, gate_b, q_norm_g, cos128, sin128, ts=512):
    B, S, D = h.shape
    NQ = N_HEADS * HEAD_DIM
    n_gate = 3 * N_HEADS
    w = jnp.pad(w_in, ((0, 0), (0, GATE_PAD - n_gate))).astype(BF16)
    gb = jnp.pad(gate_b, (0, GATE_PAD - n_gate)).reshape(1, GATE_PAD)
    W = KV_WIDTH
    qn = jnp.tile(q_norm_g, W // HEAD_DIM).reshape(1, W)
    row = lambda width: pl.BlockSpec((None, ts, width), lambda b, s: (b, s, 0))
    tab = pl.BlockSpec((ts, LANES), lambda b, s: (s, 0))
    return pl.pallas_call(
        _q_kernel,
        grid=(B, S // ts),
        in_specs=[row(D), _const_spec((1, D)), _const_spec((D, NQ + GATE_PAD)),
                  _const_spec((1, GATE_PAD)), _const_spec((1, W)), tab, tab, _const_spec((W, W))],
        out_specs=[pl.BlockSpec((None, N_KV_GROUPS, ts // Q_TILE, HEAD_DIM, HEADS_PER_GROUP * Q_TILE),
                                lambda b, s: (b, 0, s, 0, 0)),
                   pl.BlockSpec((None, n_gate, ts), lambda b, s: (b, 0, s))],
        out_shape=[jax.ShapeDtypeStruct((B, N_KV_GROUPS, S // Q_TILE, HEAD_DIM,
                                         HEADS_PER_GROUP * Q_TILE), BF16),
                   jax.ShapeDtypeStruct((B, n_gate, S), F32)],
        compiler_params=_cparams(2),
        name="nsa_q_proj",
    )(h, norm_g.reshape(1, D), w, gb, qn, cos128, sin128, _block_diag_ones(W))


N_SLC = 32
PS_PAD = 8
SCORE_PAIRS_AHEAD = 2


def _flash_step(slot, parts, v_t, m_scr, acc_scr):
    T = Q_TILE
    m_old = m_scr[slot]
    m_new, p_all = [], []
    for h in range(HEADS_PER_GROUP):
        hs = slice(h * T, (h + 1) * T)
        sm = [s[:, hs].astype(BF16) if bias is None else s[:, hs].astype(BF16) + bias
              for s, bias in parts]
        m_h = m_old[:, hs]
        for x in sm:
            m_h = jnp.maximum(m_h, jnp.max(x, axis=0, keepdims=True).astype(F32))
        m_new.append(m_h)
        p_all.append(jnp.concatenate([jnp.exp2(x - m_h.astype(BF16)) for x in sm], axis=0))
    m_new = jnp.concatenate(m_new, axis=1)
    alpha = jnp.exp2(m_old - m_new)
    m_scr[slot] = m_new
    acc_scr[slot] = alpha * acc_scr[slot] + _dot(v_t, jnp.concatenate(p_all, axis=1))


def _attn_kernel(q_ref, gt_ref, kc_ref, vct_ref, ks_ref, vst_ref, kw_ref, vwt_ref, o_ref,
                 sel_scr, ps_scr, m_scr, acc_scr, ot_scr):
    T = Q_TILE
    G = N_KV_GROUPS
    HG = HEADS_PER_GROUP
    NL = HG * T
    qi = pl.program_id(1)
    tpos = qi * T + lax.broadcasted_iota(jnp.int32, (1, T), 1)
    krow = lax.broadcasted_iota(jnp.int32, (K_TILE, T), 0)
    q_t = lambda g: q_ref[g]
    key_tile = lambda k_ref, g, kt: k_ref[g, pl.ds(pl.multiple_of(kt * K_TILE, K_TILE), K_TILE), :]

    def gate(branch, g):
        r0 = branch * N_HEADS + HG * g
        return jnp.concatenate([gt_ref[r0 + h:r0 + h + 1, :] for h in range(HG)], axis=1)

    def emit(g, o_t, first):
        for h in range(HG):
            rows = slice((HG * g + h) * HEAD_DIM, (HG * g + h + 1) * HEAD_DIM)
            piece = o_t[:, h * T:(h + 1) * T]
            ot_scr[rows, :] = piece if first else ot_scr[rows, :] + piece

    def reset_state():
        for g in range(G):
            m_scr[g] = jnp.full((1, NL), NEG, F32)
            acc_scr[g] = jnp.zeros((VT_ROWS, NL), F32)

    def emit_state(branch):
        for g in range(G):
            denom = acc_scr[g, HEAD_DIM:HEAD_DIM + 1, :]
            emit(g, acc_scr[g, 0:HEAD_DIM, :] * (gate(branch, g) / denom), first=False)

    def sweep(k_ref, vt_ref, tiles, biases, scores=None):
        pairs = [range(lo, min(lo + 2, len(tiles))) for lo in range(0, len(tiles), 2)]
        scores = dict(scores or {})

        def score_matmuls(pair, g):
            for i in pair:
                if (i, g) not in scores:
                    scores[i, g] = _dot(key_tile(k_ref, g, tiles[i]), q_t(g))

        for pair in pairs[:SCORE_PAIRS_AHEAD]:
            for g in range(G):
                score_matmuls(pair, g)
        for n, pair in enumerate(pairs):
            for g in range(G):
                parts = [(scores[i, g], None if biases[i] is None else biases[i](g)) for i in pair]
                v_t = jnp.concatenate([vt_ref[g, tiles[i]] for i in pair], axis=1)
                _flash_step(g, parts, v_t, m_scr, acc_scr)
                if n + SCORE_PAIRS_AHEAD < len(pairs):
                    score_matmuls(pairs[n + SCORE_PAIRS_AHEAD], g)

    def sweep_range(k_ref, vt_ref, n_tiles, bias_of):
        def run(tiles):
            sweep(k_ref, vt_ref, tiles,
                  [None if bias_of is None else functools.partial(bias_of, kt=kt) for kt in tiles])

        def quad(j, c):
            run([4 * j + i for i in range(4)])
            return c

        lax.fori_loop(0, n_tiles // 4, quad, 0)
        rem = n_tiles % 4
        base = n_tiles - rem

        @pl.when(rem >= 2)
        def _():
            run([base, base + 1])

        @pl.when(rem % 2 == 1)
        def _():
            run([n_tiles - 1])

    n_back = WINDOW // K_TILE

    def causal_bias(g):
        return jnp.where((qi * K_TILE + krow) <= tpos, 0.0, -jnp.inf).astype(BF16)

    def window_edge_bias(g):
        return jnp.where(((qi - n_back) * K_TILE + krow) > (tpos - WINDOW), 0.0,
                         -jnp.inf).astype(BF16)

    def compressed_and_window(tiles, biases):
        cmp_scores = [_dot(kc_ref[g], q_t(g)) for g in range(G)]
        ahead = {(i, g): _dot(key_tile(kw_ref, g, tiles[i]), q_t(g))
                 for i in range(min(len(tiles), 2 * SCORE_PAIRS_AHEAD)) for g in range(G)}
        cvalid = ((krow * CMP_STRIDE + (CMP_BLOCK - 1)) <= tpos) & (krow < N_CHUNK - 1)
        cmp_bias = jnp.where(cvalid, 0.0, -jnp.inf)
        cmp_probs = []
        for g in range(G):
            psum = jnp.zeros((N_CHUNK, T), F32)
            probs = []
            for h in range(HG):
                sm = cmp_scores[g][:, h * T:(h + 1) * T] + cmp_bias
                e = jnp.exp2(sm - jnp.maximum(jnp.max(sm, axis=0, keepdims=True), NEG))
                den = jnp.sum(e, axis=0, keepdims=True)
                p = e / jnp.where(den > 0.0, den, 1.0)
                psum = psum + p
                probs.append(p.astype(BF16))
            cmp_probs.append(jnp.concatenate(probs, axis=1))
            ps_scr[g, 0:PS_PAD, :] = jnp.zeros((PS_PAD, T), F32)
            ps_scr[g, PS_PAD:PS_PAD + N_CHUNK, :] = psum
        for g in range(G):
            emit(g, gate(0, g) * _dot(vct_ref[g], cmp_probs[g]), first=True)
        reset_state()
        sweep(kw_ref, vwt_ref, tiles, biases, scores=ahead)

    @pl.when(qi >= n_back)
    def _():
        compressed_and_window([qi - i for i in range(n_back + 1)],
                              [causal_bias] + [None] * (n_back - 1) + [window_edge_bias])

    @pl.when(qi < n_back)
    def _():
        compressed_and_window([qi], [causal_bias])
        sweep_range(kw_ref, vwt_ref, qi, None)
    emit_state(2)

    all_causal_fit = (qi * T + T - 1) // SLC_BLOCK + 1 <= SLC_TOPK

    @pl.when(all_causal_fit)
    def _():
        for g in range(G):
            sel_scr[g] = jnp.zeros((N_SLC, T), F32)

    @pl.when(jnp.logical_not(all_causal_fit))
    def _():
        ROWS = 8
        jrow = lax.broadcasted_iota(jnp.int32, (N_SLC, T), 0)
        cur = tpos // SLC_BLOCK
        causal_blk = jrow <= cur
        forced = (jrow == 0) | (causal_blk & ((cur - jrow) < N_LOCAL_BLOCKS))
        for g in range(G):
            tap = lambda k: ps_scr[g, pl.ds(PS_PAD + k, N_SLC, stride=4), :]
            imp = 0.5 * tap(-1) + tap(0) + tap(1) + tap(2) + 0.5 * tap(3)
            score = jnp.where(forced, FORCE, jnp.where(causal_blk, imp, NEG))
            parts = [score[r:r + ROWS] for r in range(0, N_SLC, ROWS)]
            ranks = [jnp.zeros((ROWS, T), F32) for _ in parts]
            for j2 in range(N_SLC):
                other = score[j2:j2 + 1, :]
                for i, part in enumerate(parts):
                    r0 = i * ROWS
                    if r0 + ROWS - 1 < j2:
                        beats = other > part
                    elif r0 > j2:
                        beats = other >= part
                    else:
                        beats = (other > part) | ((other == part) & (jrow[r0:r0 + ROWS] > j2))
                    ranks[i] = ranks[i] + jnp.where(beats, 1.0, 0.0)
            rank = jnp.concatenate(ranks, axis=0)
            sel_scr[g] = jnp.where(rank < SLC_TOPK, 0.0, -jnp.inf)

    def slc_bias(g, kt):
        half = K_TILE // 2
        top = jnp.broadcast_to(sel_scr[g, pl.ds(2 * kt, 1), :], (half, T))
        bot = jnp.broadcast_to(sel_scr[g, pl.ds(2 * kt + 1, 1), :], (half, T))
        causal = jnp.where((kt * K_TILE + krow) <= tpos, 0.0, -jnp.inf)
        return (jnp.concatenate([top, bot], axis=0) + causal).astype(BF16)

    reset_state()
    sweep_range(ks_ref, vst_ref, qi + 1, slc_bias)
    emit_state(1)

    o_ref[...] = ot_scr[...].T.astype(BF16)


def _nsa_attention(q, gates_t, k_cmp, v_cmp, ks, vst, kw, vwt):
    B, _, S, _ = ks.shape
    G, T = N_KV_GROUPS, Q_TILE
    NQ = N_HEADS * HEAD_DIM
    NT = S // K_TILE
    kc = k_cmp.reshape(B, G, N_CHUNK, HEAD_DIM).astype(BF16)
    vct = v_cmp.reshape(B, G, N_CHUNK, HEAD_DIM).transpose(0, 1, 3, 2).astype(BF16)
    per_b = lambda *shape: pl.BlockSpec((None,) + shape, lambda b, i: (b,) + (0,) * len(shape))
    return pl.pallas_call(
        _attn_kernel,
        grid=(B, S // T),
        in_specs=[pl.BlockSpec((None, G, None, HEAD_DIM, HEADS_PER_GROUP * T),
                               lambda b, i: (b, 0, i, 0, 0)),
                  pl.BlockSpec((None, 3 * N_HEADS, T), lambda b, i: (b, 0, i)),
                  per_b(G, N_CHUNK, HEAD_DIM), per_b(G, HEAD_DIM, N_CHUNK),
                  per_b(G, S, HEAD_DIM), per_b(G, NT, VT_ROWS, K_TILE),
                  per_b(G, S, HEAD_DIM), per_b(G, NT, VT_ROWS, K_TILE)],
        out_specs=pl.BlockSpec((None, T, NQ), lambda b, i: (b, i, 0)),
        out_shape=jax.ShapeDtypeStruct((B, S, NQ), BF16),
        scratch_shapes=[pltpu.VMEM((G, N_SLC, T), F32), pltpu.VMEM((G, PS_PAD + N_CHUNK, T), F32),
                        pltpu.VMEM((G, 1, HEADS_PER_GROUP * T), F32),
                        pltpu.VMEM((G, VT_ROWS, HEADS_PER_GROUP * T), F32),
                        pltpu.VMEM((NQ, T), F32)],
        compiler_params=_cparams(2),
        name="nsa_attention",
    )(q, gates_t, kc, vct, ks, vst, kw, vwt)


def kernel(x, a_norm, a_w_in, a_conv_w, a_conv_b, a_gate_w, a_gate_b, a_lambda, a_w_out,
           kv_norm, kv_w, k_norm, cmp_pos, cmp_w1, cmp_b1, cmp_w2, cmp_b2,
           b_norm, b_w_in, b_gate_b, q_norm, b_w_out, f_norm, f_w_in, f_w_out):
    B, S, D = x.shape
    assert D == D_MODEL and S == N_SLC * SLC_BLOCK and S == N_CHUNK * CMP_STRIDE
    n_a = a_norm.shape[0]
    n_b = b_norm.shape[0]
    h = x
    for i in range(n_a):
        h = _recurrent_block(h, a_norm[i], a_w_in[i], a_conv_w[i], a_conv_b[i], a_gate_w[i],
                             a_gate_b[i], a_lambda[i], a_w_out[i])
        h = _swiglu(h, f_norm[i], f_w_in[i], f_w_out[i])

    cos_t, sin_t = _rope_tables(jnp.arange(S))
    cos128, sin128 = jnp.tile(cos_t, (1, 2)), jnp.tile(sin_t, (1, 2))
    kvc, ks, vs, kw, vw = _kv_proj(h, kv_norm, kv_w, k_norm, cos128, sin128)
    cmp = _compress(kvc, cmp_pos, cmp_w1, cmp_b1, cmp_w2, cmp_b2, k_norm[0])
    for j in range(n_b):
        q, gates = _q_proj(h, b_norm[j], b_w_in[j], b_gate_b[j], q_norm[j], cos128, sin128)
        o = _nsa_attention(q, gates, cmp[0], cmp[1], ks, vs, kw, vw)
        layer = n_a + j
        h = _swiglu(h, f_norm[layer], f_w_in[layer], f_w_out[layer], attn=o, w_o=b_w_out[j])
    return h
```

```python
import functools

import jax
import jax.numpy as jnp
from jax import lax
from jax.experimental import pallas as pl
from jax.experimental.pallas import tpu as pltpu

F32 = jnp.float32
BF16 = jnp.bfloat16

D_MODEL = 1024
LRU_WIDTH = D_MODEL
LRU_HEADS = 8
LRU_BLOCK = LRU_WIDTH // LRU_HEADS
CONV_WIDTH = 4
LRU_C = 8.0
HEAD_DIM = 64
N_HEADS = D_MODEL // HEAD_DIM
N_KV_GROUPS = 4
HEADS_PER_GROUP = N_HEADS // N_KV_GROUPS
CMP_BLOCK = 32
CMP_STRIDE = 16
CMP_HIDDEN = 256
SLC_BLOCK = 64
SLC_TOPK = 16
N_LOCAL_BLOCKS = 2
WINDOW = 512
ROPE_THETA = 10000.0
FFN_HIDDEN = 2816
EPS = 1e-6
NEG = -1e30
FORCE = 1e30

LANES = 128
KV_WIDTH = N_KV_GROUPS * HEAD_DIM
Q_TILE = 128
K_TILE = 128
VT_ROWS = HEAD_DIM + 16
PROJ_ROWS = 128
VMEM_LIMIT = 56 * 1024 * 1024


def _cparams(n_axes):
    return pltpu.CompilerParams(dimension_semantics=("arbitrary",) * n_axes,
                                vmem_limit_bytes=VMEM_LIMIT)


def _const_spec(shape):
    nd = len(shape)
    return pl.BlockSpec(shape, lambda *_: (0,) * nd, pipeline_mode=pl.Buffered(1))


def _rms(x, g):
    ms = jnp.mean(x * x, axis=-1, keepdims=True)
    return x * lax.rsqrt(ms + EPS) * g


def _sigmoid(x):
    return 1.0 / (1.0 + jnp.exp(-x))


def _gelu_tanh(x):
    c = 0.7978845608028654
    return x * (0.5 * (1.0 + jnp.tanh(c * (x + 0.044715 * (x * x * x)))))


def _dot(a, b):
    return jnp.dot(a, b, preferred_element_type=F32)


def _head_mean_sq(x, ones_bd):
    sq = x * x
    hi = sq.astype(BF16)
    lo = (sq - hi.astype(F32)).astype(BF16)
    return (_dot(hi, ones_bd) + _dot(lo, ones_bd)) * (1.0 / HEAD_DIM)


def _rope_flat(x, cos_t, sin_t):
    width = x.shape[-1]
    lane = lax.broadcasted_iota(jnp.int32, x.shape, 1)
    upper = (lane & (HEAD_DIM // 2)) != 0
    partner = jnp.where(upper, pltpu.roll(x, HEAD_DIM // 2, 1),
                        pltpu.roll(x, width - HEAD_DIM // 2, 1))
    return x * cos_t + partner * sin_t


def _tile_lanes(t, width):
    reps = width // t.shape[-1]
    return t if reps == 1 else jnp.concatenate([t] * reps, axis=1)


SUBLANES = 8


def _segment_perm(ts):
    seg_len = ts // SUBLANES
    dst = jnp.arange(ts)
    src = (dst % SUBLANES) * seg_len + dst // SUBLANES
    return (src[:, None] == jnp.arange(ts)[None, :]).astype(BF16)


def _scan_segments(a, b, h_in):
    n_steps = a.shape[0] // SUBLANES
    vreg = lambda x, j: x[j * SUBLANES:(j + 1) * SUBLANES]
    h_loc, a_cum = [vreg(b, 0)], [vreg(a, 0)]
    for j in range(1, n_steps):
        h_loc.append(vreg(a, j) * h_loc[-1] + vreg(b, j))
        a_cum.append(vreg(a, j) * a_cum[-1])
    seg_a, seg_h = a_cum[-1], h_loc[-1]
    carry = [h_in]
    for s in range(SUBLANES):
        carry.append(seg_a[s:s + 1] * carry[-1] + seg_h[s:s + 1])
    enter = jnp.concatenate(carry[:SUBLANES], axis=0)
    h = jnp.concatenate([h_loc[j] + a_cum[j] * enter for j in range(n_steps)], axis=0)
    return h, carry[SUBLANES]


def _rec_kernel(x_ref, xnext_ref, g_ref, perm_ref, perm_t_ref, win_ref, cw_ref, cb_ref, wg_ref,
                gb_ref, lam_ref, wout_ref, o_ref, tail, hcar, z_first):
    R = LRU_WIDTH
    sub_rows = perm_ref.shape[0]
    n_sub = x_ref.shape[0] // sub_rows
    n_steps = sub_rows // SUBLANES
    taps = CONV_WIDTH - 1

    def project(rows_f32):
        ut = _dot(perm_ref[...], _rms(rows_f32, g_ref[...]).astype(BF16)).astype(BF16)
        return _dot(ut, win_ref[...])

    @pl.when(pl.program_id(1) == 0)
    def _():
        tail[...] = jnp.zeros_like(tail)
        hcar[...] = jnp.zeros_like(hcar)

    @pl.when((pl.program_id(0) == 0) & (pl.program_id(1) == 0))
    def _():
        z_first[...] = project(x_ref[0:sub_rows, :])

    x = x_ref[...]
    in_proj = lambda t: project(x[t * sub_rows:(t + 1) * sub_rows])

    sub = lax.broadcasted_iota(jnp.int32, (SUBLANES, LRU_BLOCK), 0)
    softplus_neg = jnp.maximum(-lam_ref[...], 0.0) + jnp.log1p(jnp.exp(-jnp.abs(lam_ref[...])))
    log_a_scale = -LRU_C * softplus_neg

    def conv_and_gates(z, prev_tail):
        conv_out, gate_pre = [], []
        for hh in range(LRU_HEADS):
            cs = slice(hh * LRU_BLOCK, (hh + 1) * LRU_BLOCK)
            xc = z[:, R + hh * LRU_BLOCK:R + (hh + 1) * LRU_BLOCK]
            wrapped = []
            for k in range(taps):
                cur = xc[(n_steps - taps + k) * SUBLANES:(n_steps - taps + k + 1) * SUBLANES]
                prev = prev_tail[k * SUBLANES:(k + 1) * SUBLANES, cs]
                wrapped.append(pltpu.roll(jnp.where(sub == SUBLANES - 1, prev, cur), 1, 0))
            back = lambda d: jnp.concatenate(
                wrapped[taps - d:] + [xc[0:(n_steps - d) * SUBLANES]], axis=0)
            cw = cw_ref[:, cs]
            xr = cb_ref[:, cs] + back(3) * cw[0:1]
            xr = xr + back(2) * cw[1:2]
            xr = xr + back(1) * cw[2:3]
            xr = xr + xc * cw[3:4]
            conv_out.append(xr)
            gate_pre.append(_dot(xr.astype(BF16), wg_ref[hh]))
        return conv_out, gate_pre

    def recur_and_gate(z, conv_out, gate_pre):
        gated = []
        for hh in range(LRU_HEADS):
            cs = slice(hh * LRU_BLOCK, (hh + 1) * LRU_BLOCK)
            xr, gates = conv_out[hh], gate_pre[hh]
            gb = gb_ref[:, cs]
            r = _sigmoid(gates[:, :LRU_BLOCK] + gb[0:1])
            i = _sigmoid(gates[:, LRU_BLOCK:] + gb[1:2])
            log_a = log_a_scale[:, cs] * r
            a = jnp.exp(log_a)
            bterm = jnp.sqrt(1.0 - a * a) * (i * xr)
            hs, hcar[:, cs] = _scan_segments(a, bterm, hcar[:, cs])
            gated.append((_gelu_tanh(z[:, cs]) * hs).astype(BF16))
        return jnp.concatenate(gated, axis=1)

    last_rows = slice((n_steps - taps) * SUBLANES, None)
    zs = {0: z_first[...]}
    staged = conv_and_gates(zs[0], tail[...])
    zs[1] = in_proj(1)
    for t in range(n_sub):
        gated = recur_and_gate(zs[t], *staged)
        if t + 1 < n_sub:
            staged = conv_and_gates(zs[t + 1], zs[t][last_rows, R:])
        yh = _dot(perm_t_ref[...], gated).astype(BF16)
        rows = slice(t * sub_rows, (t + 1) * sub_rows)
        o_ref[rows, :] = x[rows] + _dot(yh, wout_ref[...])
        if t + 2 < n_sub:
            zs[t + 2] = in_proj(t + 2)
        elif t + 2 == n_sub:
            z_first[...] = project(xnext_ref[...])
    tail[...] = zs[n_sub - 1][last_rows, R:]


def _recurrent_block(h, norm_g, w_in, conv_w, conv_b, gate_w, gate_b, lam, w_out, ts=1024,
                     sub_rows=256):
    B, S, D = h.shape
    R = LRU_WIDTH
    wg = jnp.concatenate([gate_w[0], gate_w[1]], axis=-1).astype(BF16)
    perm = _segment_perm(sub_rows)
    n_t, n_sub = S // ts, ts // sub_rows
    assert n_sub >= 2

    def next_first_rows(b, s):
        nxt = jnp.minimum(b * n_t + s + 1, B * n_t - 1)
        return nxt // n_t, (nxt % n_t) * n_sub, 0

    return pl.pallas_call(
        _rec_kernel,
        grid=(B, n_t),
        in_specs=[
            pl.BlockSpec((None, ts, D), lambda b, s: (b, s, 0)),
            pl.BlockSpec((None, sub_rows, D), next_first_rows),
            _const_spec((1, D)),
            _const_spec((sub_rows, sub_rows)),
            _const_spec((sub_rows, sub_rows)),
            _const_spec((D, 2 * R)),
            _const_spec((CONV_WIDTH, R)),
            _const_spec((1, R)),
            _const_spec((LRU_HEADS, LRU_BLOCK, 2 * LRU_BLOCK)),
            _const_spec((2, R)),
            _const_spec((1, R)),
            _const_spec((R, D)),
        ],
        out_specs=pl.BlockSpec((None, ts, D), lambda b, s: (b, s, 0)),
        out_shape=jax.ShapeDtypeStruct((B, S, D), F32),
        scratch_shapes=[pltpu.VMEM(((CONV_WIDTH - 1) * SUBLANES, R), F32), pltpu.VMEM((1, R), F32),
                        pltpu.VMEM((sub_rows, 2 * R), F32)],
        compiler_params=_cparams(2),
        name="rglru_block",
    )(h, h, norm_g.reshape(1, D), perm, perm.T, w_in.astype(BF16), conv_w, conv_b.reshape(1, R), wg,
      gate_b, lam.reshape(1, R), w_out.astype(BF16))


FFN_CHUNK = FFN_HIDDEN // 2


def _ffn_body(x, g_ref, win_ref, wout_ref, o_ref):
    u = _rms(x, g_ref[...]).astype(BF16)

    acc = x
    for c in range(FFN_HIDDEN // FFN_CHUNK):
        lo = c * FFN_CHUNK
        gate = _dot(u, win_ref[:, lo:lo + FFN_CHUNK])
        up = _dot(u, win_ref[:, FFN_HIDDEN + lo:FFN_HIDDEN + lo + FFN_CHUNK])
        act = ((gate * _sigmoid(gate)) * up).astype(BF16)
        acc = acc + _dot(act, wout_ref[lo:lo + FFN_CHUNK, :])
    o_ref[...] = acc


def _ffn_kernel(x_ref, g_ref, win_ref, wout_ref, o_ref):
    _ffn_body(x_ref[...], g_ref, win_ref, wout_ref, o_ref)


def _proj_ffn_kernel(x_ref, a_ref, wo_ref, g_ref, win_ref, wout_ref, o_ref):
    _ffn_body(x_ref[...] + _dot(a_ref[...], wo_ref[...]), g_ref, win_ref, wout_ref, o_ref)


def _swiglu(h, norm_g, w_in, w_out, attn=None, w_o=None, tm=512):
    B, S, D = h.shape
    M = B * S
    row_spec = pl.BlockSpec((tm, D), lambda i: (i, 0))
    w_specs = [_const_spec((1, D)), _const_spec((D, 2 * FFN_HIDDEN)), _const_spec((FFN_HIDDEN, D))]
    w_args = (norm_g.reshape(1, D), w_in.astype(BF16), w_out.astype(BF16))
    if attn is None:
        kern, specs, args = _ffn_kernel, [row_spec] + w_specs, (h.reshape(M, D),) + w_args
    else:
        kern = _proj_ffn_kernel
        specs = [row_spec, row_spec, _const_spec((D, D))] + w_specs
        args = (h.reshape(M, D), attn.reshape(M, D), w_o.astype(BF16)) + w_args
    out = pl.pallas_call(
        kern,
        grid=(M // tm,),
        in_specs=specs,
        out_specs=row_spec,
        out_shape=jax.ShapeDtypeStruct((M, D), F32),
        compiler_params=_cparams(1),
        name="swiglu_ffn",
    )(*args)
    return out.reshape(B, S, D)


def _kv_kernel(x_ref, g_ref, w_ref, kn_ref, cos_ref, sin_ref, ones_ref,
               kvc_ref, ks_ref, vs_ref, kw_ref, vw_ref):
    W = KV_WIDTH
    T = PROJ_ROWS
    n_sub = x_ref.shape[0] // T
    u = _rms(x_ref[...], g_ref[...]).astype(BF16)
    ones_bd = ones_ref[...]
    project = lambda t: _dot(u[t * T:(t + 1) * T], w_ref[...])
    kv_next = project(0)
    for t in range(n_sub):
        kv = kv_next
        rows = slice(t * T, (t + 1) * T)
        part = lambda j: kv[:, j * W:(j + 1) * W]
        mean_sq = {j: _head_mean_sq(part(j), ones_bd) for j in (2, 4)}
        if t + 1 < n_sub:
            kv_next = project(t + 1)
        cos_t = _tile_lanes(cos_ref[rows, :], W)
        sin_t = _tile_lanes(sin_ref[rows, :], W)
        kvc_ref[rows, :] = kv[:, 0:2 * W]
        for j, gain, k_ref in ((2, kn_ref[1:2, :], ks_ref), (4, kn_ref[2:3, :], kw_ref)):
            k = part(j) * lax.rsqrt(mean_sq[j] + EPS) * gain
            k = _rope_flat(k, cos_t, sin_t).astype(BF16)
            for g in range(N_KV_GROUPS):
                k_ref[g, rows, :] = k[:, g * HEAD_DIM:(g + 1) * HEAD_DIM]
        for j, vt_ref in ((3, vs_ref), (5, vw_ref)):
            v_t = part(j).T
            for g in range(N_KV_GROUPS):
                for i in range(T // K_TILE):
                    tile = t * (T // K_TILE) + i
                    vt_ref[g, tile, 0:HEAD_DIM, :] = v_t[g * HEAD_DIM:(g + 1) * HEAD_DIM,
                                                         i * K_TILE:(i + 1) * K_TILE].astype(BF16)
                    vt_ref[g, tile, HEAD_DIM:VT_ROWS, :] = jnp.ones((VT_ROWS - HEAD_DIM, K_TILE), BF16)


def _rope_tables(pos):
    half = HEAD_DIM // 2
    freqs = jnp.power(ROPE_THETA, -jnp.arange(half, dtype=F32) / half)
    ang = pos.astype(F32)[:, None] * freqs[None, :]
    cos, sin = jnp.cos(ang), jnp.sin(ang)
    cos_t = jnp.concatenate([cos, cos], axis=-1)
    sin_t = jnp.concatenate([-sin, sin], axis=-1)
    return cos_t, sin_t


def _block_diag_ones(width):
    seg = jnp.arange(width) // HEAD_DIM
    return (seg[:, None] == seg[None, :]).astype(BF16)


def _kv_proj(h, kv_norm, kv_w, k_norm, cos128, sin128, ts=512):
    B, S, D = h.shape
    W = KV_WIDTH
    kn = jnp.tile(k_norm, (1, N_KV_GROUPS))
    G = N_KV_GROUPS
    row = lambda width: pl.BlockSpec((None, ts, width), lambda b, s: (b, s, 0))
    tab = pl.BlockSpec((ts, LANES), lambda b, s: (s, 0))
    key_spec = pl.BlockSpec((None, G, ts, HEAD_DIM), lambda b, s: (b, 0, s, 0))
    val_spec = pl.BlockSpec((None, G, ts // K_TILE, VT_ROWS, K_TILE), lambda b, s: (b, 0, s, 0, 0))
    flat = jax.ShapeDtypeStruct((B, S, 2 * W), F32)
    keys = jax.ShapeDtypeStruct((B, G, S, HEAD_DIM), BF16)
    vals = jax.ShapeDtypeStruct((B, G, S // K_TILE, VT_ROWS, K_TILE), BF16)
    return pl.pallas_call(
        _kv_kernel,
        grid=(B, S // ts),
        in_specs=[row(D), _const_spec((1, D)), _const_spec((D, 6 * W)), _const_spec((3, W)),
                  tab, tab, _const_spec((W, W))],
        out_specs=[row(2 * W), key_spec, val_spec, key_spec, val_spec],
        out_shape=[flat, keys, vals, keys, vals],
        compiler_params=_cparams(2),
        name="shared_kv_proj",
    )(h, kv_norm.reshape(1, D), kv_w.astype(BF16), kn, cos128, sin128, _block_diag_ones(W))


N_CHUNK = 128
CHUNK_W = CMP_STRIDE * HEAD_DIM


def _cmp_kernel(xa_ref, xb_ref, pos_ref, w1_ref, b1_ref, w2_ref, b2_ref, kn_ref, cos_ref, sin_ref, o_ref):
    groups = []
    for x_ref in (xa_ref, xb_ref):
        nth = [x_ref[pl.ds(r, N_CHUNK, stride=CMP_STRIDE), :] for r in range(CMP_STRIDE)]
        for j in range(LANES // HEAD_DIM):
            groups.append(jnp.concatenate([t[:, j * HEAD_DIM:(j + 1) * HEAD_DIM] for t in nth], axis=1))
    x = jnp.concatenate(groups, axis=0)
    rows = x.shape[0]
    ya = _dot((x + pos_ref[0:1, :]).astype(BF16), w1_ref[0:CHUNK_W, :])
    yb = _dot((x + pos_ref[1:2, :]).astype(BF16), w1_ref[CHUNK_W:2 * CHUNK_W, :])
    hid = _gelu_tanh(ya + pltpu.roll(yb, rows - 1, 0) + b1_ref[...])
    out = _dot(hid.astype(BF16), w2_ref[...]) + b2_ref[...]

    @pl.when(pl.program_id(0) == 0)
    def _():
        k = _rms(out, kn_ref[...])
        half = HEAD_DIM // 2
        partner = jnp.concatenate([k[:, half:], k[:, :half]], axis=1)
        o_ref[...] = k * cos_ref[...] + partner * sin_ref[...]

    @pl.when(pl.program_id(0) == 1)
    def _():
        o_ref[...] = out


def _compress(kvc, cmp_pos, cmp_w1, cmp_b1, cmp_w2, cmp_b2, k_norm0):
    B, S, _ = kvc.shape
    G = N_KV_GROUPS
    rows = G * N_CHUNK
    pos = cmp_pos.reshape(2, 2, CHUNK_W)
    cmp_last = jnp.arange(N_CHUNK) * CMP_STRIDE + CMP_BLOCK - 1
    cos_t, sin_t = _rope_tables(cmp_last)
    cos_t, sin_t = jnp.tile(cos_t, (G, 1)), jnp.tile(sin_t, (G, 1))
    per_kv = lambda *shape: pl.BlockSpec((None,) + shape, lambda k, b: (k,) + (0,) * len(shape))
    return pl.pallas_call(
        _cmp_kernel,
        grid=(2, B),
        in_specs=[pl.BlockSpec((None, S, LANES), lambda k, b: (b, 0, 2 * k)),
                  pl.BlockSpec((None, S, LANES), lambda k, b: (b, 0, 2 * k + 1)),
                  per_kv(2, CHUNK_W), per_kv(2 * CHUNK_W, CMP_HIDDEN), per_kv(1, CMP_HIDDEN),
                  per_kv(CMP_HIDDEN, HEAD_DIM), per_kv(1, HEAD_DIM),
                  _const_spec((1, HEAD_DIM)), _const_spec((rows, HEAD_DIM)),
                  _const_spec((rows, HEAD_DIM))],
        out_specs=pl.BlockSpec((None, None, rows, HEAD_DIM), lambda k, b: (k, b, 0, 0)),
        out_shape=jax.ShapeDtypeStruct((2, B, rows, HEAD_DIM), F32),
        compiler_params=_cparams(2),
        name="kv_compress",
    )(kvc, kvc, pos, cmp_w1.astype(BF16), cmp_b1.reshape(2, 1, CMP_HIDDEN), cmp_w2.astype(BF16),
      cmp_b2.reshape(2, 1, HEAD_DIM), k_norm0.reshape(1, HEAD_DIM), cos_t, sin_t)


GATE_PAD = LANES
LOG2_E = 1.4426950408889634
Q_SCALE = HEAD_DIM ** -0.5 * LOG2_E


def _q_kernel(x_ref, g_ref, w_ref, gb_ref, qn_ref, cos_ref, sin_ref, ones_ref, q_ref, gate_ref):
    NQ = N_HEADS * HEAD_DIM
    T = PROJ_ROWS
    n_sub = x_ref.shape[0] // T
    u = _rms(x_ref[...], g_ref[...]).astype(BF16)
    ones_bd = ones_ref[...]
    W = ones_bd.shape[0]
    project = lambda t: _dot(u[t * T:(t + 1) * T], w_ref[...])
    z_next = project(0)
    for t in range(n_sub):
        z = z_next
        rows = slice(t * T, (t + 1) * T)
        chunks = [z[:, c * W:(c + 1) * W] for c in range(NQ // W)]
        mean_sq = [_head_mean_sq(q, ones_bd) for q in chunks]
        if t + 1 < n_sub:
            z_next = project(t + 1)
        gate_ref[:, rows] = _sigmoid(z[:, NQ:] + gb_ref[...]).T[0:3 * N_HEADS, :]
        cos_t = _tile_lanes(cos_ref[rows, :], W)
        sin_t = _tile_lanes(sin_ref[rows, :], W)
        for c, q in enumerate(chunks):
            q = q * lax.rsqrt(mean_sq[c] + EPS) * qn_ref[...]
            q_tr = (_rope_flat(q, cos_t, sin_t) * Q_SCALE).T
            for i in range(T // Q_TILE):
                q_ref[c, t * (T // Q_TILE) + i] = jnp.concatenate(
                    [q_tr[j * HEAD_DIM:(j + 1) * HEAD_DIM, i * Q_TILE:(i + 1) * Q_TILE]
                     for j in range(W // HEAD_DIM)], axis=1).astype(BF16)


def _q_proj(h, norm_g, w_in, gate_b, q_norm_g, cos128, sin128, ts=512):
    B, S, D = h.shape
    NQ = N_HEADS * HEAD_DIM
    n_gate = 3 * N_HEADS
    w = jnp.pad(w_in, ((0, 0), (0, GATE_PAD - n_gate))).astype(BF16)
    gb = jnp.pad(gate_b, (0, GATE_PAD - n_gate)).reshape(1, GATE_PAD)
    W = KV_WIDTH
    qn = jnp.tile(q_norm_g, W // HEAD_DIM).reshape(1, W)
    row = lambda width: pl.BlockSpec((None, ts, width), lambda b, s: (b, s, 0))
    tab = pl.BlockSpec((ts, LANES), lambda b, s: (s, 0))
    return pl.pallas_call(
        _q_kernel,
        grid=(B, S // ts),
        in_specs=[row(D), _const_spec((1, D)), _const_spec((D, NQ + GATE_PAD)),
                  _const_spec((1, GATE_PAD)), _const_spec((1, W)), tab, tab, _const_spec((W, W))],
        out_specs=[pl.BlockSpec((None, N_KV_GROUPS, ts // Q_TILE, HEAD_DIM, HEADS_PER_GROUP * Q_TILE),
                                lambda b, s: (b, 0, s, 0, 0)),
                   pl.BlockSpec((None, n_gate, ts), lambda b, s: (b, 0, s))],
        out_shape=[jax.ShapeDtypeStruct((B, N_KV_GROUPS, S // Q_TILE, HEAD_DIM,
                                         HEADS_PER_GROUP * Q_TILE), BF16),
                   jax.ShapeDtypeStruct((B, n_gate, S), F32)],
        compiler_params=_cparams(2),
        name="nsa_q_proj",
    )(h, norm_g.reshape(1, D), w, gb, qn, cos128, sin128, _block_diag_ones(W))


N_SLC = 32
PS_PAD = 8
SCORE_PAIRS_AHEAD = 2


def _flash_step(slot, parts, v_t, m_scr, acc_scr):
    T = Q_TILE
    m_old = m_scr[slot]
    m_new, p_all = [], []
    for h in range(HEADS_PER_GROUP):
        hs = slice(h * T, (h + 1) * T)
        sm = [s[:, hs].astype(BF16) if bias is None else s[:, hs].astype(BF16) + bias
              for s, bias in parts]
        m_h = m_old[:, hs]
        for x in sm:
            m_h = jnp.maximum(m_h, jnp.max(x, axis=0, keepdims=True).astype(F32))
        m_new.append(m_h)
        p_all.append(jnp.concatenate([jnp.exp2(x - m_h.astype(BF16)) for x in sm], axis=0))
    m_new = jnp.concatenate(m_new, axis=1)
    alpha = jnp.exp2(m_old - m_new)
    m_scr[slot] = m_new
    acc_scr[slot] = alpha * acc_scr[slot] + _dot(v_t, jnp.concatenate(p_all, axis=1))


def _attn_kernel(q_ref, gt_ref, kc_ref, vct_ref, ks_ref, vst_ref, kw_ref, vwt_ref, o_ref,
                 sel_scr, ps_scr, m_scr, acc_scr, ot_scr):
    T = Q_TILE
    G = N_KV_GROUPS
    HG = HEADS_PER_GROUP
    NL = HG * T
    qi = pl.program_id(1)
    tpos = qi * T + lax.broadcasted_iota(jnp.int32, (1, T), 1)
    krow = lax.broadcasted_iota(jnp.int32, (K_TILE, T), 0)
    q_t = lambda g: q_ref[g]
    key_tile = lambda k_ref, g, kt: k_ref[g, pl.ds(pl.multiple_of(kt * K_TILE, K_TILE), K_TILE), :]

    def gate(branch, g):
        r0 = branch * N_HEADS + HG * g
        return jnp.concatenate([gt_ref[r0 + h:r0 + h + 1, :] for h in range(HG)], axis=1)

    def emit(g, o_t, first):
        for h in range(HG):
            rows = slice((HG * g + h) * HEAD_DIM, (HG * g + h + 1) * HEAD_DIM)
            piece = o_t[:, h * T:(h + 1) * T]
            ot_scr[rows, :] = piece if first else ot_scr[rows, :] + piece

    def reset_state():
        for g in range(G):
            m_scr[g] = jnp.full((1, NL), NEG, F32)
            acc_scr[g] = jnp.zeros((VT_ROWS, NL), F32)

    def emit_state(branch):
        for g in range(G):
            denom = acc_scr[g, HEAD_DIM:HEAD_DIM + 1, :]
            emit(g, acc_scr[g, 0:HEAD_DIM, :] * (gate(branch, g) / denom), first=False)

    def sweep(k_ref, vt_ref, tiles, biases, scores=None):
        pairs = [range(lo, min(lo + 2, len(tiles))) for lo in range(0, len(tiles), 2)]
        scores = dict(scores or {})

        def score_matmuls(pair, g):
            for i in pair:
                if (i, g) not in scores:
                    scores[i, g] = _dot(key_tile(k_ref, g, tiles[i]), q_t(g))

        for pair in pairs[:SCORE_PAIRS_AHEAD]:
            for g in range(G):
                score_matmuls(pair, g)
        for n, pair in enumerate(pairs):
            for g in range(G):
                parts = [(scores[i, g], None if biases[i] is None else biases[i](g)) for i in pair]
                v_t = jnp.concatenate([vt_ref[g, tiles[i]] for i in pair], axis=1)
                _flash_step(g, parts, v_t, m_scr, acc_scr)
                if n + SCORE_PAIRS_AHEAD < len(pairs):
                    score_matmuls(pairs[n + SCORE_PAIRS_AHEAD], g)

    def sweep_range(k_ref, vt_ref, n_tiles, bias_of):
        def run(tiles):
            sweep(k_ref, vt_ref, tiles,
                  [None if bias_of is None else functools.partial(bias_of, kt=kt) for kt in tiles])

        def quad(j, c):
            run([4 * j + i for i in range(4)])
            return c

        lax.fori_loop(0, n_tiles // 4, quad, 0)
        rem = n_tiles % 4
        base = n_tiles - rem

        @pl.when(rem >= 2)
        def _():
            run([base, base + 1])

        @pl.when(rem % 2 == 1)
        def _():
            run([n_tiles - 1])

    n_back = WINDOW // K_TILE

    def causal_bias(g):
        return jnp.where((qi * K_TILE + krow) <= tpos, 0.0, -jnp.inf).astype(BF16)

    def window_edge_bias(g):
        return jnp.where(((qi - n_back) * K_TILE + krow) > (tpos - WINDOW), 0.0,
                         -jnp.inf).astype(BF16)

    def compressed_and_window(tiles, biases):
        cmp_scores = [_dot(kc_ref[g], q_t(g)) for g in range(G)]
        ahead = {(i, g): _dot(key_tile(kw_ref, g, tiles[i]), q_t(g))
                 for i in range(min(len(tiles), 2 * SCORE_PAIRS_AHEAD)) for g in range(G)}
        cvalid = ((krow * CMP_STRIDE + (CMP_BLOCK - 1)) <= tpos) & (krow < N_CHUNK - 1)
        cmp_bias = jnp.where(cvalid, 0.0, -jnp.inf)
        cmp_probs = []
        for g in range(G):
            psum = jnp.zeros((N_CHUNK, T), F32)
            probs = []
            for h in range(HG):
                sm = cmp_scores[g][:, h * T:(h + 1) * T] + cmp_bias
                e = jnp.exp2(sm - jnp.maximum(jnp.max(sm, axis=0, keepdims=True), NEG))
                den = jnp.sum(e, axis=0, keepdims=True)
                p = e / jnp.where(den > 0.0, den, 1.0)
                psum = psum + p
                probs.append(p.astype(BF16))
            cmp_probs.append(jnp.concatenate(probs, axis=1))
            ps_scr[g, 0:PS_PAD, :] = jnp.zeros((PS_PAD, T), F32)
            ps_scr[g, PS_PAD:PS_PAD + N_CHUNK, :] = psum
        for g in range(G):
            emit(g, gate(0, g) * _dot(vct_ref[g], cmp_probs[g]), first=True)
        reset_state()
        sweep(kw_ref, vwt_ref, tiles, biases, scores=ahead)

    @pl.when(qi >= n_back)
    def _():
        compressed_and_window([qi - i for i in range(n_back + 1)],
                              [causal_bias] + [None] * (n_back - 1) + [window_edge_bias])

    @pl.when(qi < n_back)
    def _():
        compressed_and_window([qi], [causal_bias])
        sweep_range(kw_ref, vwt_ref, qi, None)
    emit_state(2)

    all_causal_fit = (qi * T + T - 1) // SLC_BLOCK + 1 <= SLC_TOPK

    @pl.when(all_causal_fit)
    def _():
        for g in range(G):
            sel_scr[g] = jnp.zeros((N_SLC, T), F32)

    @pl.when(jnp.logical_not(all_causal_fit))
    def _():
        ROWS = 8
        jrow = lax.broadcasted_iota(jnp.int32, (N_SLC, T), 0)
        cur = tpos // SLC_BLOCK
        causal_blk = jrow <= cur
        forced = (jrow == 0) | (causal_blk & ((cur - jrow) < N_LOCAL_BLOCKS))
        for g in range(G):
            tap = lambda k: ps_scr[g, pl.ds(PS_PAD + k, N_SLC, stride=4), :]
            imp = 0.5 * tap(-1) + tap(0) + tap(1) + tap(2) + 0.5 * tap(3)
            score = jnp.where(forced, FORCE, jnp.where(causal_blk, imp, NEG))
            parts = [score[r:r + ROWS] for r in range(0, N_SLC, ROWS)]
            ranks = [jnp.zeros((ROWS, T), F32) for _ in parts]
            for j2 in range(N_SLC):
                other = score[j2:j2 + 1, :]
                for i, part in enumerate(parts):
                    r0 = i * ROWS
                    if r0 + ROWS - 1 < j2:
                        beats = other > part
                    elif r0 > j2:
                        beats = other >= part
                    else:
                        beats = (other > part) | ((other == part) & (jrow[r0:r0 + ROWS] > j2))
                    ranks[i] = ranks[i] + jnp.where(beats, 1.0, 0.0)
            rank = jnp.concatenate(ranks, axis=0)
            sel_scr[g] = jnp.where(rank < SLC_TOPK, 0.0, -jnp.inf)

    def slc_bias(g, kt):
        half = K_TILE // 2
        top = jnp.broadcast_to(sel_scr[g, pl.ds(2 * kt, 1), :], (half, T))
        bot = jnp.broadcast_to(sel_scr[g, pl.ds(2 * kt + 1, 1), :], (half, T))
        causal = jnp.where((kt * K_TILE + krow) <= tpos, 0.0, -jnp.inf)
        return (jnp.concatenate([top, bot], axis=0) + causal).astype(BF16)

    reset_state()
    sweep_range(ks_ref, vst_ref, qi + 1, slc_bias)
    emit_state(1)

    o_ref[...] = ot_scr[...].T.astype(BF16)


def _nsa_attention(q, gates_t, k_cmp, v_cmp, ks, vst, kw, vwt):
    B, _, S, _ = ks.shape
    G, T = N_KV_GROUPS, Q_TILE
    NQ = N_HEADS * HEAD_DIM
    NT = S // K_TILE
    kc = k_cmp.reshape(B, G, N_CHUNK, HEAD_DIM).astype(BF16)
    vct = v_cmp.reshape(B, G, N_CHUNK, HEAD_DIM).transpose(0, 1, 3, 2).astype(BF16)
    per_b = lambda *shape: pl.BlockSpec((None,) + shape, lambda b, i: (b,) + (0,) * len(shape))
    return pl.pallas_call(
        _attn_kernel,
        grid=(B, S // T),
        in_specs=[pl.BlockSpec((None, G, None, HEAD_DIM, HEADS_PER_GROUP * T),
                               lambda b, i: (b, 0, i, 0, 0)),
                  pl.BlockSpec((None, 3 * N_HEADS, T), lambda b, i: (b, 0, i)),
                  per_b(G, N_CHUNK, HEAD_DIM), per_b(G, HEAD_DIM, N_CHUNK),
                  per_b(G, S, HEAD_DIM), per_b(G, NT, VT_ROWS, K_TILE),
                  per_b(G, S, HEAD_DIM), per_b(G, NT, VT_ROWS, K_TILE)],
        out_specs=pl.BlockSpec((None, T, NQ), lambda b, i: (b, i, 0)),
        out_shape=jax.ShapeDtypeStruct((B, S, NQ), BF16),
        scratch_shapes=[pltpu.VMEM((G, N_SLC, T), F32), pltpu.VMEM((G, PS_PAD + N_CHUNK, T), F32),
                        pltpu.VMEM((G, 1, HEADS_PER_GROUP * T), F32),
                        pltpu.VMEM((G, VT_ROWS, HEADS_PER_GROUP * T), F32),
                        pltpu.VMEM((NQ, T), F32)],
        compiler_params=_cparams(2),
        name="nsa_attention",
    )(q, gates_t, kc, vct, ks, vst, kw, vwt)


def kernel(x, a_norm, a_w_in, a_conv_w, a_conv_b, a_gate_w, a_gate_b, a_lambda, a_w_out,
           kv_norm, kv_w, k_norm, cmp_pos, cmp_w1, cmp_b1, cmp_w2, cmp_b2,
           b_norm, b_w_in, b_gate_b, q_norm, b_w_out, f_norm, f_w_in, f_w_out):
    B, S, D = x.shape
    assert D == D_MODEL and S == N_SLC * SLC_BLOCK and S == N_CHUNK * CMP_STRIDE
    n_a = a_norm.shape[0]
    n_b = b_norm.shape[0]
    h = x
    for i in range(n_a):
        h = _recurrent_block(h, a_norm[i], a_w_in[i], a_conv_w[i], a_conv_b[i], a_gate_w[i],
                             a_gate_b[i], a_lambda[i], a_w_out[i])
        h = _swiglu(h, f_norm[i], f_w_in[i], f_w_out[i])

    cos_t, sin_t = _rope_tables(jnp.arange(S))
    cos128, sin128 = jnp.tile(cos_t, (1, 2)), jnp.tile(sin_t, (1, 2))
    kvc, ks, vs, kw, vw = _kv_proj(h, kv_norm, kv_w, k_norm, cos128, sin128)
    cmp = _compress(kvc, cmp_pos, cmp_w1, cmp_b1, cmp_w2, cmp_b2, k_norm[0])
    for j in range(n_b):
        q, gates = _q_proj(h, b_norm[j], b_w_in[j], b_gate_b[j], q_norm[j], cos128, sin128)
        o = _nsa_attention(q, gates, cmp[0], cmp[1], ks, vs, kw, vw)
        layer = n_a + j
        h = _swiglu(h, f_norm[layer], f_w_in[layer], f_w_out[layer], attn=o, w_o=b_w_out[j])
    return h
```

```python
import functools

import jax
import jax.numpy as jnp
from jax import lax
from jax.experimental import pallas as pl
from jax.experimental.pallas import tpu as pltpu

F32 = jnp.float32
BF16 = jnp.bfloat16

D_MODEL = 1024
LRU_WIDTH = D_MODEL
LRU_HEADS = 8
LRU_BLOCK = LRU_WIDTH // LRU_HEADS
CONV_WIDTH = 4
LRU_C = 8.0
HEAD_DIM = 64
N_HEADS = D_MODEL // HEAD_DIM
N_KV_GROUPS = 4
HEADS_PER_GROUP = N_HEADS // N_KV_GROUPS
CMP_BLOCK = 32
CMP_STRIDE = 16
CMP_HIDDEN = 256
SLC_BLOCK = 64
SLC_TOPK = 16
N_LOCAL_BLOCKS = 2
WINDOW = 512
ROPE_THETA = 10000.0
FFN_HIDDEN = 2816
EPS = 1e-6
NEG = -1e30
FORCE = 1e30

LANES = 128
KV_WIDTH = N_KV_GROUPS * HEAD_DIM
Q_TILE = 128
K_TILE = 128
VT_ROWS = HEAD_DIM + 16
PROJ_ROWS = 128
VMEM_LIMIT = 56 * 1024 * 1024


def _cparams(n_axes):
    return pltpu.CompilerParams(dimension_semantics=("arbitrary",) * n_axes,
                                vmem_limit_bytes=VMEM_LIMIT)


def _const_spec(shape):
    nd = len(shape)
    return pl.BlockSpec(shape, lambda *_: (0,) * nd, pipeline_mode=pl.Buffered(1))


def _rms(x, g):
    ms = jnp.mean(x * x, axis=-1, keepdims=True)
    return x * lax.rsqrt(ms + EPS) * g


def _sigmoid(x):
    return 1.0 / (1.0 + jnp.exp(-x))


def _gelu_tanh(x):
    c = 0.7978845608028654
    return x * (0.5 * (1.0 + jnp.tanh(c * (x + 0.044715 * (x * x * x)))))


def _dot(a, b):
    return jnp.dot(a, b, preferred_element_type=F32)


def _head_mean_sq(x, ones_bd):
    sq = x * x
    hi = sq.astype(BF16)
    lo = (sq - hi.astype(F32)).astype(BF16)
    return (_dot(hi, ones_bd) + _dot(lo, ones_bd)) * (1.0 / HEAD_DIM)


def _rope_flat(x, cos_t, sin_t):
    width = x.shape[-1]
    lane = lax.broadcasted_iota(jnp.int32, x.shape, 1)
    upper = (lane & (HEAD_DIM // 2)) != 0
    partner = jnp.where(upper, pltpu.roll(x, HEAD_DIM // 2, 1),
                        pltpu.roll(x, width - HEAD_DIM // 2, 1))
    return x * cos_t + partner * sin_t


def _tile_lanes(t, width):
    reps = width // t.shape[-1]
    return t if reps == 1 else jnp.concatenate([t] * reps, axis=1)


SUBLANES = 8


def _segment_perm(ts):
    seg_len = ts // SUBLANES
    dst = jnp.arange(ts)
    src = (dst % SUBLANES) * seg_len + dst // SUBLANES
    return (src[:, None] == jnp.arange(ts)[None, :]).astype(BF16)


def _scan_segments(a, b, h_in):
    n_steps = a.shape[0] // SUBLANES
    vreg = lambda x, j: x[j * SUBLANES:(j + 1) * SUBLANES]
    h_loc, a_cum = [vreg(b, 0)], [vreg(a, 0)]
    for j in range(1, n_steps):
        h_loc.append(vreg(a, j) * h_loc[-1] + vreg(b, j))
        a_cum.append(vreg(a, j) * a_cum[-1])
    seg_a, seg_h = a_cum[-1], h_loc[-1]
    carry = [h_in]
    for s in range(SUBLANES):
        carry.append(seg_a[s:s + 1] * carry[-1] + seg_h[s:s + 1])
    enter = jnp.concatenate(carry[:SUBLANES], axis=0)
    h = jnp.concatenate([h_loc[j] + a_cum[j] * enter for j in range(n_steps)], axis=0)
    return h, carry[SUBLANES]


def _rec_kernel(x_ref, xnext_ref, g_ref, perm_ref, perm_t_ref, win_ref, cw_ref, cb_ref, wg_ref,
                gb_ref, lam_ref, wout_ref, o_ref, tail, hcar, z_first):
    R = LRU_WIDTH
    sub_rows = perm_ref.shape[0]
    n_sub = x_ref.shape[0] // sub_rows
    n_steps = sub_rows // SUBLANES
    taps = CONV_WIDTH - 1

    def project(rows_f32):
        ut = _dot(perm_ref[...], _rms(rows_f32, g_ref[...]).astype(BF16)).astype(BF16)
        return _dot(ut, win_ref[...])

    @pl.when(pl.program_id(1) == 0)
    def _():
        tail[...] = jnp.zeros_like(tail)
        hcar[...] = jnp.zeros_like(hcar)

    @pl.when((pl.program_id(0) == 0) & (pl.program_id(1) == 0))
    def _():
        z_first[...] = project(x_ref[0:sub_rows, :])

    x = x_ref[...]
    in_proj = lambda t: project(x[t * sub_rows:(t + 1) * sub_rows])

    sub = lax.broadcasted_iota(jnp.int32, (SUBLANES, LRU_BLOCK), 0)
    softplus_neg = jnp.maximum(-lam_ref[...], 0.0) + jnp.log1p(jnp.exp(-jnp.abs(lam_ref[...])))
    log_a_scale = -LRU_C * softplus_neg

    def conv_and_gates(z, prev_tail):
        conv_out, gate_pre = [], []
        for hh in range(LRU_HEADS):
            cs = slice(hh * LRU_BLOCK, (hh + 1) * LRU_BLOCK)
            xc = z[:, R + hh * LRU_BLOCK:R + (hh + 1) * LRU_BLOCK]
            wrapped = []
            for k in range(taps):
                cur = xc[(n_steps - taps + k) * SUBLANES:(n_steps - taps + k + 1) * SUBLANES]
                prev = prev_tail[k * SUBLANES:(k + 1) * SUBLANES, cs]
                wrapped.append(pltpu.roll(jnp.where(sub == SUBLANES - 1, prev, cur), 1, 0))
            back = lambda d: jnp.concatenate(
                wrapped[taps - d:] + [xc[0:(n_steps - d) * SUBLANES]], axis=0)
            cw = cw_ref[:, cs]
            xr = cb_ref[:, cs] + back(3) * cw[0:1]
            xr = xr + back(2) * cw[1:2]
            xr = xr + back(1) * cw[2:3]
            xr = xr + xc * cw[3:4]
            conv_out.append(xr)
            gate_pre.append(_dot(xr.astype(BF16), wg_ref[hh]))
        return conv_out, gate_pre

    def recur_and_gate(z, conv_out, gate_pre):
        gated = []
        for hh in range(LRU_HEADS):
            cs = slice(hh * LRU_BLOCK, (hh + 1) * LRU_BLOCK)
            xr, gates = conv_out[hh], gate_pre[hh]
            gb = gb_ref[:, cs]
            r = _sigmoid(gates[:, :LRU_BLOCK] + gb[0:1])
            i = _sigmoid(gates[:, LRU_BLOCK:] + gb[1:2])
            log_a = log_a_scale[:, cs] * r
            a = jnp.exp(log_a)
            bterm = jnp.sqrt(1.0 - a * a) * (i * xr)
            hs, hcar[:, cs] = _scan_segments(a, bterm, hcar[:, cs])
            gated.append((_gelu_tanh(z[:, cs]) * hs).astype(BF16))
        return jnp.concatenate(gated, axis=1)

    last_rows = slice((n_steps - taps) * SUBLANES, None)
    zs = {0: z_first[...]}
    staged = conv_and_gates(zs[0], tail[...])
    zs[1] = in_proj(1)
    for t in range(n_sub):
        gated = recur_and_gate(zs[t], *staged)
        if t + 1 < n_sub:
            staged = conv_and_gates(zs[t + 1], zs[t][last_rows, R:])
        yh = _dot(perm_t_ref[...], gated).astype(BF16)
        rows = slice(t * sub_rows, (t + 1) * sub_rows)
        o_ref[rows, :] = x[rows] + _dot(yh, wout_ref[...])
        if t + 2 < n_sub:
            zs[t + 2] = in_proj(t + 2)
        elif t + 2 == n_sub:
            z_first[...] = project(xnext_ref[...])
    tail[...] = zs[n_sub - 1][last_rows, R:]


def _recurrent_block(h, norm_g, w_in, conv_w, conv_b, gate_w, gate_b, lam, w_out, ts=1024,
                     sub_rows=256):
    B, S, D = h.shape
    R = LRU_WIDTH
    wg = jnp.concatenate([gate_w[0], gate_w[1]], axis=-1).astype(BF16)
    perm = _segment_perm(sub_rows)
    n_t, n_sub = S // ts, ts // sub_rows
    assert n_sub >= 2

    def next_first_rows(b, s):
        nxt = jnp.minimum(b * n_t + s + 1, B * n_t - 1)
        return nxt // n_t, (nxt % n_t) * n_sub, 0

    return pl.pallas_call(
        _rec_kernel,
        grid=(B, n_t),
        in_specs=[
            pl.BlockSpec((None, ts, D), lambda b, s: (b, s, 0)),
            pl.BlockSpec((None, sub_rows, D), next_first_rows),
            _const_spec((1, D)),
            _const_spec((sub_rows, sub_rows)),
            _const_spec((sub_rows, sub_rows)),
            _const_spec((D, 2 * R)),
            _const_spec((CONV_WIDTH, R)),
            _const_spec((1, R)),
            _const_spec((LRU_HEADS, LRU_BLOCK, 2 * LRU_BLOCK)),
            _const_spec((2, R)),
            _const_spec((1, R)),
            _const_spec((R, D)),
        ],
        out_specs=pl.BlockSpec((None, ts, D), lambda b, s: (b, s, 0)),
        out_shape=jax.ShapeDtypeStruct((B, S, D), F32),
        scratch_shapes=[pltpu.VMEM(((CONV_WIDTH - 1) * SUBLANES, R), F32), pltpu.VMEM((1, R), F32),
                        pltpu.VMEM((sub_rows, 2 * R), F32)],
        compiler_params=_cparams(2),
        name="rglru_block",
    )(h, h, norm_g.reshape(1, D), perm, perm.T, w_in.astype(BF16), conv_w, conv_b.reshape(1, R), wg,
      gate_b, lam.reshape(1, R), w_out.astype(BF16))


FFN_CHUNK = FFN_HIDDEN // 2


def _ffn_body(x, g_ref, win_ref, wout_ref, o_ref):
    u = _rms(x, g_ref[...]).astype(BF16)

    acc = x
    for c in range(FFN_HIDDEN // FFN_CHUNK):
        lo = c * FFN_CHUNK
        gate = _dot(u, win_ref[:, lo:lo + FFN_CHUNK])
        up = _dot(u, win_ref[:, FFN_HIDDEN + lo:FFN_HIDDEN + lo + FFN_CHUNK])
        act = ((gate * _sigmoid(gate)) * up).astype(BF16)
        acc = acc + _dot(act, wout_ref[lo:lo + FFN_CHUNK, :])
    o_ref[...] = acc


def _ffn_kernel(x_ref, g_ref, win_ref, wout_ref, o_ref):
    _ffn_body(x_ref[...], g_ref, win_ref, wout_ref, o_ref)


def _proj_ffn_kernel(x_ref, a_ref, wo_ref, g_ref, win_ref, wout_ref, o_ref):
    _ffn_body(x_ref[...] + _dot(a_ref[...], wo_ref[...]), g_ref, win_ref, wout_ref, o_ref)


def _swiglu(h, norm_g, w_in, w_out, attn=None, w_o=None, tm=512):
    B, S, D = h.shape
    M = B * S
    row_spec = pl.BlockSpec((tm, D), lambda i: (i, 0))
    w_specs = [_const_spec((1, D)), _const_spec((D, 2 * FFN_HIDDEN)), _const_spec((FFN_HIDDEN, D))]
    w_args = (norm_g.reshape(1, D), w_in.astype(BF16), w_out.astype(BF16))
    if attn is None:
        kern, specs, args = _ffn_kernel, [row_spec] + w_specs, (h.reshape(M, D),) + w_args
    else:
        kern = _proj_ffn_kernel
        specs = [row_spec, row_spec, _const_spec((D, D))] + w_specs
        args = (h.reshape(M, D), attn.reshape(M, D), w_o.astype(BF16)) + w_args
    out = pl.pallas_call(
        kern,
        grid=(M // tm,),
        in_specs=specs,
        out_specs=row_spec,
        out_shape=jax.ShapeDtypeStruct((M, D), F32),
        compiler_params=_cparams(1),
        name="swiglu_ffn",
    )(*args)
    return out.reshape(B, S, D)


def _kv_kernel(x_ref, g_ref, w_ref, kn_ref, cos_ref, sin_ref, ones_ref,
               kvc_ref, ks_ref, vs_ref, kw_ref, vw_ref):
    W = KV_WIDTH
    T = PROJ_ROWS
    n_sub = x_ref.shape[0] // T
    u = _rms(x_ref[...], g_ref[...]).astype(BF16)
    ones_bd = ones_ref[...]
    project = lambda t: _dot(u[t * T:(t + 1) * T], w_ref[...])
    kv_next = project(0)
    for t in range(n_sub):
        kv = kv_next
        rows = slice(t * T, (t + 1) * T)
        part = lambda j: kv[:, j * W:(j + 1) * W]
        mean_sq = {j: _head_mean_sq(part(j), ones_bd) for j in (2, 4)}
        if t + 1 < n_sub:
            kv_next = project(t + 1)
        cos_t = _tile_lanes(cos_ref[rows, :], W)
        sin_t = _tile_lanes(sin_ref[rows, :], W)
        kvc_ref[rows, :] = kv[:, 0:2 * W]
        for j, gain, k_ref in ((2, kn_ref[1:2, :], ks_ref), (4, kn_ref[2:3, :], kw_ref)):
            k = part(j) * lax.rsqrt(mean_sq[j] + EPS) * gain
            k = _rope_flat(k, cos_t, sin_t).astype(BF16)
            for g in range(N_KV_GROUPS):
                k_ref[g, rows, :] = k[:, g * HEAD_DIM:(g + 1) * HEAD_DIM]
        for j, vt_ref in ((3, vs_ref), (5, vw_ref)):
            v_t = part(j).T
            for g in range(N_KV_GROUPS):
                for i in range(T // K_TILE):
                    tile = t * (T // K_TILE) + i
                    vt_ref[g, tile, 0:HEAD_DIM, :] = v_t[g * HEAD_DIM:(g + 1) * HEAD_DIM,
                                                         i * K_TILE:(i + 1) * K_TILE].astype(BF16)
                    vt_ref[g, tile, HEAD_DIM:VT_ROWS, :] = jnp.ones((VT_ROWS - HEAD_DIM, K_TILE), BF16)


def _rope_tables(pos):
    half = HEAD_DIM // 2
    freqs = jnp.power(ROPE_THETA, -jnp.arange(half, dtype=F32) / half)
    ang = pos.astype(F32)[:, None] * freqs[None, :]
    cos, sin = jnp.cos(ang), jnp.sin(ang)
    cos_t = jnp.concatenate([cos, cos], axis=-1)
    sin_t = jnp.concatenate([-sin, sin], axis=-1)
    return cos_t, sin_t


def _block_diag_ones(width):
    seg = jnp.arange(width) // HEAD_DIM
    return (seg[:, None] == seg[None, :]).astype(BF16)


def _kv_proj(h, kv_norm, kv_w, k_norm, cos128, sin128, ts=512):
    B, S, D = h.shape
    W = KV_WIDTH
    kn = jnp.tile(k_norm, (1, N_KV_GROUPS))
    G = N_KV_GROUPS
    row = lambda width: pl.BlockSpec((None, ts, width), lambda b, s: (b, s, 0))
    tab = pl.BlockSpec((ts, LANES), lambda b, s: (s, 0))
    key_spec = pl.BlockSpec((None, G, ts, HEAD_DIM), lambda b, s: (b, 0, s, 0))
    val_spec = pl.BlockSpec((None, G, ts // K_TILE, VT_ROWS, K_TILE), lambda b, s: (b, 0, s, 0, 0))
    flat = jax.ShapeDtypeStruct((B, S, 2 * W), F32)
    keys = jax.ShapeDtypeStruct((B, G, S, HEAD_DIM), BF16)
    vals = jax.ShapeDtypeStruct((B, G, S // K_TILE, VT_ROWS, K_TILE), BF16)
    return pl.pallas_call(
        _kv_kernel,
        grid=(B, S // ts),
        in_specs=[row(D), _const_spec((1, D)), _const_spec((D, 6 * W)), _const_spec((3, W)),
                  tab, tab, _const_spec((W, W))],
        out_specs=[row(2 * W), key_spec, val_spec, key_spec, val_spec],
        out_shape=[flat, keys, vals, keys, vals],
        compiler_params=_cparams(2),
        name="shared_kv_proj",
    )(h, kv_norm.reshape(1, D), kv_w.astype(BF16), kn, cos128, sin128, _block_diag_ones(W))


N_CHUNK = 128
CHUNK_W = CMP_STRIDE * HEAD_DIM


def _cmp_kernel(xa_ref, xb_ref, pos_ref, w1_ref, b1_ref, w2_ref, b2_ref, kn_ref, cos_ref, sin_ref, o_ref):
    groups = []
    for x_ref in (xa_ref, xb_ref):
        nth = [x_ref[pl.ds(r, N_CHUNK, stride=CMP_STRIDE), :] for r in range(CMP_STRIDE)]
        for j in range(LANES // HEAD_DIM):
            groups.append(jnp.concatenate([t[:, j * HEAD_DIM:(j + 1) * HEAD_DIM] for t in nth], axis=1))
    x = jnp.concatenate(groups, axis=0)
    rows = x.shape[0]
    ya = _dot((x + pos_ref[0:1, :]).astype(BF16), w1_ref[0:CHUNK_W, :])
    yb = _dot((x + pos_ref[1:2, :]).astype(BF16), w1_ref[CHUNK_W:2 * CHUNK_W, :])
    hid = _gelu_tanh(ya + pltpu.roll(yb, rows - 1, 0) + b1_ref[...])
    out = _dot(hid.astype(BF16), w2_ref[...]) + b2_ref[...]

    @pl.when(pl.program_id(0) == 0)
    def _():
        k = _rms(out, kn_ref[...])
        half = HEAD_DIM // 2
        partner = jnp.concatenate([k[:, half:], k[:, :half]], axis=1)
        o_ref[...] = k * cos_ref[...] + partner * sin_ref[...]

    @pl.when(pl.program_id(0) == 1)
    def _():
        o_ref[...] = out


def _compress(kvc, cmp_pos, cmp_w1, cmp_b1, cmp_w2, cmp_b2, k_norm0):
    B, S, _ = kvc.shape
    G = N_KV_GROUPS
    rows = G * N_CHUNK
    pos = cmp_pos.reshape(2, 2, CHUNK_W)
    cmp_last = jnp.arange(N_CHUNK) * CMP_STRIDE + CMP_BLOCK - 1
    cos_t, sin_t = _rope_tables(cmp_last)
    cos_t, sin_t = jnp.tile(cos_t, (G, 1)), jnp.tile(sin_t, (G, 1))
    per_kv = lambda *shape: pl.BlockSpec((None,) + shape, lambda k, b: (k,) + (0,) * len(shape))
    return pl.pallas_call(
        _cmp_kernel,
        grid=(2, B),
        in_specs=[pl.BlockSpec((None, S, LANES), lambda k, b: (b, 0, 2 * k)),
                  pl.BlockSpec((None, S, LANES), lambda k, b: (b, 0, 2 * k + 1)),
                  per_kv(2, CHUNK_W), per_kv(2 * CHUNK_W, CMP_HIDDEN), per_kv(1, CMP_HIDDEN),
                  per_kv(CMP_HIDDEN, HEAD_DIM), per_kv(1, HEAD_DIM),
                  _const_spec((1, HEAD_DIM)), _const_spec((rows, HEAD_DIM)),
                  _const_spec((rows, HEAD_DIM))],
        out_specs=pl.BlockSpec((None, None, rows, HEAD_DIM), lambda k, b: (k, b, 0, 0)),
        out_shape=jax.ShapeDtypeStruct((2, B, rows, HEAD_DIM), F32),
        compiler_params=_cparams(2),
        name="kv_compress",
    )(kvc, kvc, pos, cmp_w1.astype(BF16), cmp_b1.reshape(2, 1, CMP_HIDDEN), cmp_w2.astype(BF16),
      cmp_b2.reshape(2, 1, HEAD_DIM), k_norm0.reshape(1, HEAD_DIM), cos_t, sin_t)


GATE_PAD = LANES
LOG2_E = 1.4426950408889634
Q_SCALE = HEAD_DIM ** -0.5 * LOG2_E


def _q_kernel(x_ref, xnext_ref, g_ref, w_ref, gb_ref, qn_ref, cos_ref, sin_ref, ones_ref,
              q_ref, gate_ref, z_first):
    NQ = N_HEADS * HEAD_DIM
    T = PROJ_ROWS
    n_sub = x_ref.shape[0] // T
    ones_bd = ones_ref[...]
    W = ones_bd.shape[0]
    project = lambda rows_f32: _dot(_rms(rows_f32, g_ref[...]).astype(BF16), w_ref[...])

    @pl.when((pl.program_id(0) == 0) & (pl.program_id(1) == 0))
    def _():
        z_first[...] = project(x_ref[0:T, :])

    z_next = z_first[...]
    for t in range(n_sub):
        z = z_next
        rows = slice(t * T, (t + 1) * T)
        chunks = [z[:, c * W:(c + 1) * W] for c in range(NQ // W)]
        mean_sq = [_head_mean_sq(q, ones_bd) for q in chunks]
        if t + 1 < n_sub:
            z_next = project(x_ref[(t + 1) * T:(t + 2) * T, :])
        else:
            z_first[...] = project(xnext_ref[...])
        gate_ref[:, rows] = _sigmoid(z[:, NQ:] + gb_ref[...]).T[0:3 * N_HEADS, :]
        cos_t = _tile_lanes(cos_ref[rows, :], W)
        sin_t = _tile_lanes(sin_ref[rows, :], W)
        for c, q in enumerate(chunks):
            q = q * lax.rsqrt(mean_sq[c] + EPS) * qn_ref[...]
            q_tr = (_rope_flat(q, cos_t, sin_t) * Q_SCALE).T
            for i in range(T // Q_TILE):
                q_ref[c, t * (T // Q_TILE) + i] = jnp.concatenate(
                    [q_tr[j * HEAD_DIM:(j + 1) * HEAD_DIM, i * Q_TILE:(i + 1) * Q_TILE]
                     for j in range(W // HEAD_DIM)], axis=1).astype(BF16)


def _q_proj(h, norm_g, w_in, gate_b, q_norm_g, cos128, sin128, ts=512):
    B, S, D = h.shape
    NQ = N_HEADS * HEAD_DIM
    n_gate = 3 * N_HEADS
    w = jnp.pad(w_in, ((0, 0), (0, GATE_PAD - n_gate))).astype(BF16)
    gb = jnp.pad(gate_b, (0, GATE_PAD - n_gate)).reshape(1, GATE_PAD)
    W = KV_WIDTH
    qn = jnp.tile(q_norm_g, W // HEAD_DIM).reshape(1, W)
    row = lambda width: pl.BlockSpec((None, ts, width), lambda b, s: (b, s, 0))
    tab = pl.BlockSpec((ts, LANES), lambda b, s: (s, 0))
    n_t, n_sub = S // ts, ts // PROJ_ROWS

    def next_first_rows(b, s):
        nxt = jnp.minimum(b * n_t + s + 1, B * n_t - 1)
        return nxt // n_t, (nxt % n_t) * n_sub, 0

    return pl.pallas_call(
        _q_kernel,
        grid=(B, n_t),
        in_specs=[row(D), pl.BlockSpec((None, PROJ_ROWS, D), next_first_rows),
                  _const_spec((1, D)), _const_spec((D, NQ + GATE_PAD)),
                  _const_spec((1, GATE_PAD)), _const_spec((1, W)), tab, tab, _const_spec((W, W))],
        out_specs=[pl.BlockSpec((None, N_KV_GROUPS, ts // Q_TILE, HEAD_DIM, HEADS_PER_GROUP * Q_TILE),
                                lambda b, s: (b, 0, s, 0, 0)),
                   pl.BlockSpec((None, n_gate, ts), lambda b, s: (b, 0, s))],
        out_shape=[jax.ShapeDtypeStruct((B, N_KV_GROUPS, S // Q_TILE, HEAD_DIM,
                                         HEADS_PER_GROUP * Q_TILE), BF16),
                   jax.ShapeDtypeStruct((B, n_gate, S), F32)],
        scratch_shapes=[pltpu.VMEM((PROJ_ROWS, NQ + GATE_PAD), F32)],
        compiler_params=_cparams(2),
        name="nsa_q_proj",
    )(h, h, norm_g.reshape(1, D), w, gb, qn, cos128, sin128, _block_diag_ones(W))


N_SLC = 32
PS_PAD = 8
SCORE_PAIRS_AHEAD = 2


def _flash_step(slot, parts, v_t, m_scr, acc_scr):
    T = Q_TILE
    m_old = m_scr[slot]
    m_new, p_all = [], []
    for h in range(HEADS_PER_GROUP):
        hs = slice(h * T, (h + 1) * T)
        sm = [s[:, hs].astype(BF16) if bias is None else s[:, hs].astype(BF16) + bias
              for s, bias in parts]
        m_h = m_old[:, hs]
        for x in sm:
            m_h = jnp.maximum(m_h, jnp.max(x, axis=0, keepdims=True).astype(F32))
        m_new.append(m_h)
        p_all.append(jnp.concatenate([jnp.exp2(x - m_h.astype(BF16)) for x in sm], axis=0))
    m_new = jnp.concatenate(m_new, axis=1)
    alpha = jnp.exp2(m_old - m_new)
    m_scr[slot] = m_new
    acc_scr[slot] = alpha * acc_scr[slot] + _dot(v_t, jnp.concatenate(p_all, axis=1))


def _attn_kernel(q_ref, gt_ref, kc_ref, vct_ref, ks_ref, vst_ref, kw_ref, vwt_ref, o_ref,
                 sel_scr, ps_scr, m_scr, acc_scr, ot_scr):
    T = Q_TILE
    G = N_KV_GROUPS
    HG = HEADS_PER_GROUP
    NL = HG * T
    qi = pl.program_id(1)
    tpos = qi * T + lax.broadcasted_iota(jnp.int32, (1, T), 1)
    krow = lax.broadcasted_iota(jnp.int32, (K_TILE, T), 0)
    q_t = lambda g: q_ref[g]
    key_tile = lambda k_ref, g, kt: k_ref[g, pl.ds(pl.multiple_of(kt * K_TILE, K_TILE), K_TILE), :]

    def gate(branch, g):
        r0 = branch * N_HEADS + HG * g
        return jnp.concatenate([gt_ref[r0 + h:r0 + h + 1, :] for h in range(HG)], axis=1)

    def emit(g, o_t, first):
        for h in range(HG):
            rows = slice((HG * g + h) * HEAD_DIM, (HG * g + h + 1) * HEAD_DIM)
            piece = o_t[:, h * T:(h + 1) * T]
            ot_scr[rows, :] = piece if first else ot_scr[rows, :] + piece

    def reset_state():
        for g in range(G):
            m_scr[g] = jnp.full((1, NL), NEG, F32)
            acc_scr[g] = jnp.zeros((VT_ROWS, NL), F32)

    def emit_state(branch):
        for g in range(G):
            denom = acc_scr[g, HEAD_DIM:HEAD_DIM + 1, :]
            emit(g, acc_scr[g, 0:HEAD_DIM, :] * (gate(branch, g) / denom), first=False)

    def sweep(k_ref, vt_ref, tiles, biases, scores=None):
        pairs = [range(lo, min(lo + 2, len(tiles))) for lo in range(0, len(tiles), 2)]
        scores = dict(scores or {})

        def score_matmuls(pair, g):
            for i in pair:
                if (i, g) not in scores:
                    scores[i, g] = _dot(key_tile(k_ref, g, tiles[i]), q_t(g))

        for pair in pairs[:SCORE_PAIRS_AHEAD]:
            for g in range(G):
                score_matmuls(pair, g)
        for n, pair in enumerate(pairs):
            for g in range(G):
                parts = [(scores[i, g], None if biases[i] is None else biases[i](g)) for i in pair]
                v_t = jnp.concatenate([vt_ref[g, tiles[i]] for i in pair], axis=1)
                _flash_step(g, parts, v_t, m_scr, acc_scr)
                if n + SCORE_PAIRS_AHEAD < len(pairs):
                    score_matmuls(pairs[n + SCORE_PAIRS_AHEAD], g)

    def sweep_range(k_ref, vt_ref, n_tiles, bias_of):
        def run(tiles):
            sweep(k_ref, vt_ref, tiles,
                  [None if bias_of is None else functools.partial(bias_of, kt=kt) for kt in tiles])

        def quad(j, c):
            run([4 * j + i for i in range(4)])
            return c

        lax.fori_loop(0, n_tiles // 4, quad, 0)
        rem = n_tiles % 4
        base = n_tiles - rem

        @pl.when(rem >= 2)
        def _():
            run([base, base + 1])

        @pl.when(rem % 2 == 1)
        def _():
            run([n_tiles - 1])

    n_back = WINDOW // K_TILE

    def causal_bias(g):
        return jnp.where((qi * K_TILE + krow) <= tpos, 0.0, -jnp.inf).astype(BF16)

    def window_edge_bias(g):
        return jnp.where(((qi - n_back) * K_TILE + krow) > (tpos - WINDOW), 0.0,
                         -jnp.inf).astype(BF16)

    def compressed_and_window(tiles, biases):
        cmp_scores = [_dot(kc_ref[g], q_t(g)) for g in range(G)]
        ahead = {(i, g): _dot(key_tile(kw_ref, g, tiles[i]), q_t(g))
                 for i in range(min(len(tiles), 2 * SCORE_PAIRS_AHEAD)) for g in range(G)}
        cvalid = ((krow * CMP_STRIDE + (CMP_BLOCK - 1)) <= tpos) & (krow < N_CHUNK - 1)
        cmp_bias = jnp.where(cvalid, 0.0, -jnp.inf)
        cmp_probs = []
        for g in range(G):
            psum = jnp.zeros((N_CHUNK, T), F32)
            probs = []
            for h in range(HG):
                sm = cmp_scores[g][:, h * T:(h + 1) * T] + cmp_bias
                e = jnp.exp2(sm - jnp.maximum(jnp.max(sm, axis=0, keepdims=True), NEG))
                den = jnp.sum(e, axis=0, keepdims=True)
                p = e / jnp.where(den > 0.0, den, 1.0)
                psum = psum + p
                probs.append(p.astype(BF16))
            cmp_probs.append(jnp.concatenate(probs, axis=1))
            ps_scr[g, 0:PS_PAD, :] = jnp.zeros((PS_PAD, T), F32)
            ps_scr[g, PS_PAD:PS_PAD + N_CHUNK, :] = psum
        for g in range(G):
            emit(g, gate(0, g) * _dot(vct_ref[g], cmp_probs[g]), first=True)
        reset_state()
        sweep(kw_ref, vwt_ref, tiles, biases, scores=ahead)

    @pl.when(qi >= n_back)
    def _():
        compressed_and_window([qi - i for i in range(n_back + 1)],
                              [causal_bias] + [None] * (n_back - 1) + [window_edge_bias])

    @pl.when(qi < n_back)
    def _():
        compressed_and_window([qi], [causal_bias])
        sweep_range(kw_ref, vwt_ref, qi, None)
    emit_state(2)

    all_causal_fit = (qi * T + T - 1) // SLC_BLOCK + 1 <= SLC_TOPK

    @pl.when(all_causal_fit)
    def _():
        for g in range(G):
            sel_scr[g] = jnp.zeros((N_SLC, T), F32)

    @pl.when(jnp.logical_not(all_causal_fit))
    def _():
        ROWS = 8
        jrow = lax.broadcasted_iota(jnp.int32, (N_SLC, T), 0)
        cur = tpos // SLC_BLOCK
        causal_blk = jrow <= cur
        forced = (jrow == 0) | (causal_blk & ((cur - jrow) < N_LOCAL_BLOCKS))
        for g in range(G):
            tap = lambda k: ps_scr[g, pl.ds(PS_PAD + k, N_SLC, stride=4), :]
            imp = 0.5 * tap(-1) + tap(0) + tap(1) + tap(2) + 0.5 * tap(3)
            score = jnp.where(forced, FORCE, jnp.where(causal_blk, imp, NEG))
            parts = [score[r:r + ROWS] for r in range(0, N_SLC, ROWS)]
            ranks = [jnp.zeros((ROWS, T), F32) for _ in parts]
            for j2 in range(N_SLC):
                other = score[j2:j2 + 1, :]
                for i, part in enumerate(parts):
                    r0 = i * ROWS
                    if r0 + ROWS - 1 < j2:
                        beats = other > part
                    elif r0 > j2:
                        beats = other >= part
                    else:
                        beats = (other > part) | ((other == part) & (jrow[r0:r0 + ROWS] > j2))
                    ranks[i] = ranks[i] + jnp.where(beats, 1.0, 0.0)
            rank = jnp.concatenate(ranks, axis=0)
            sel_scr[g] = jnp.where(rank < SLC_TOPK, 0.0, -jnp.inf)

    def slc_bias(g, kt):
        half = K_TILE // 2
        top = jnp.broadcast_to(sel_scr[g, pl.ds(2 * kt, 1), :], (half, T))
        bot = jnp.broadcast_to(sel_scr[g, pl.ds(2 * kt + 1, 1), :], (half, T))
        causal = jnp.where((kt * K_TILE + krow) <= tpos, 0.0, -jnp.inf)
        return (jnp.concatenate([top, bot], axis=0) + causal).astype(BF16)

    reset_state()
    sweep_range(ks_ref, vst_ref, qi + 1, slc_bias)
    emit_state(1)

    o_ref[...] = ot_scr[...].T.astype(BF16)


def _nsa_attention(q, gates_t, k_cmp, v_cmp, ks, vst, kw, vwt):
    B, _, S, _ = ks.shape
    G, T = N_KV_GROUPS, Q_TILE
    NQ = N_HEADS * HEAD_DIM
    NT = S // K_TILE
    kc = k_cmp.reshape(B, G, N_CHUNK, HEAD_DIM).astype(BF16)
    vct = v_cmp.reshape(B, G, N_CHUNK, HEAD_DIM).transpose(0, 1, 3, 2).astype(BF16)
    per_b = lambda *shape: pl.BlockSpec((None,) + shape, lambda b, i: (b,) + (0,) * len(shape))
    return pl.pallas_call(
        _attn_kernel,
        grid=(B, S // T),
        in_specs=[pl.BlockSpec((None, G, None, HEAD_DIM, HEADS_PER_GROUP * T),
                               lambda b, i: (b, 0, i, 0, 0)),
                  pl.BlockSpec((None, 3 * N_HEADS, T), lambda b, i: (b, 0, i)),
                  per_b(G, N_CHUNK, HEAD_DIM), per_b(G, HEAD_DIM, N_CHUNK),
                  per_b(G, S, HEAD_DIM), per_b(G, NT, VT_ROWS, K_TILE),
                  per_b(G, S, HEAD_DIM), per_b(G, NT, VT_ROWS, K_TILE)],
        out_specs=pl.BlockSpec((None, T, NQ), lambda b, i: (b, i, 0)),
        out_shape=jax.ShapeDtypeStruct((B, S, NQ), BF16),
        scratch_shapes=[pltpu.VMEM((G, N_SLC, T), F32), pltpu.VMEM((G, PS_PAD + N_CHUNK, T), F32),
                        pltpu.VMEM((G, 1, HEADS_PER_GROUP * T), F32),
                        pltpu.VMEM((G, VT_ROWS, HEADS_PER_GROUP * T), F32),
                        pltpu.VMEM((NQ, T), F32)],
        compiler_params=_cparams(2),
        name="nsa_attention",
    )(q, gates_t, kc, vct, ks, vst, kw, vwt)


def kernel(x, a_norm, a_w_in, a_conv_w, a_conv_b, a_gate_w, a_gate_b, a_lambda, a_w_out,
           kv_norm, kv_w, k_norm, cmp_pos, cmp_w1, cmp_b1, cmp_w2, cmp_b2,
           b_norm, b_w_in, b_gate_b, q_norm, b_w_out, f_norm, f_w_in, f_w_out):
    B, S, D = x.shape
    assert D == D_MODEL and S == N_SLC * SLC_BLOCK and S == N_CHUNK * CMP_STRIDE
    n_a = a_norm.shape[0]
    n_b = b_norm.shape[0]
    h = x
    for i in range(n_a):
        h = _recurrent_block(h, a_norm[i], a_w_in[i], a_conv_w[i], a_conv_b[i], a_gate_w[i],
                             a_gate_b[i], a_lambda[i], a_w_out[i])
        h = _swiglu(h, f_norm[i], f_w_in[i], f_w_out[i])

    cos_t, sin_t = _rope_tables(jnp.arange(S))
    cos128, sin128 = jnp.tile(cos_t, (1, 2)), jnp.tile(sin_t, (1, 2))
    kvc, ks, vs, kw, vw = _kv_proj(h, kv_norm, kv_w, k_norm, cos128, sin128)
    cmp = _compress(kvc, cmp_pos, cmp_w1, cmp_b1, cmp_w2, cmp_b2, k_norm[0])
    for j in range(n_b):
        q, gates = _q_proj(h, b_norm[j], b_w_in[j], b_gate_b[j], q_norm[j], cos128, sin128)
        o = _nsa_attention(q, gates, cmp[0], cmp[1], ks, vs, kw, vw)
        layer = n_a + j
        h = _swiglu(h, f_norm[layer], f_w_in[layer], f_w_out[layer], attn=o, w_o=b_w_out[j])
    return h
```

```python
import functools

import jax
import jax.numpy as jnp
from jax import lax
from jax.experimental import pallas as pl
from jax.experimental.pallas import tpu as pltpu

F32 = jnp.float32
BF16 = jnp.bfloat16

D_MODEL = 1024
LRU_WIDTH = D_MODEL
LRU_HEADS = 8
LRU_BLOCK = LRU_WIDTH // LRU_HEADS
CONV_WIDTH = 4
LRU_C = 8.0
HEAD_DIM = 64
N_HEADS = D_MODEL // HEAD_DIM
N_KV_GROUPS = 4
HEADS_PER_GROUP = N_HEADS // N_KV_GROUPS
CMP_BLOCK = 32
CMP_STRIDE = 16
CMP_HIDDEN = 256
SLC_BLOCK = 64
SLC_TOPK = 16
N_LOCAL_BLOCKS = 2
WINDOW = 512
ROPE_THETA = 10000.0
FFN_HIDDEN = 2816
EPS = 1e-6
NEG = -1e30
FORCE = 1e30

LANES = 128
KV_WIDTH = N_KV_GROUPS * HEAD_DIM
Q_TILE = 128
K_TILE = 128
VT_ROWS = HEAD_DIM + 16
PROJ_ROWS = 128
VMEM_LIMIT = 56 * 1024 * 1024


def _cparams(n_axes):
    return pltpu.CompilerParams(dimension_semantics=("arbitrary",) * n_axes,
                                vmem_limit_bytes=VMEM_LIMIT)


def _const_spec(shape):
    nd = len(shape)
    return pl.BlockSpec(shape, lambda *_: (0,) * nd, pipeline_mode=pl.Buffered(1))


def _rms(x, g):
    ms = jnp.mean(x * x, axis=-1, keepdims=True)
    return x * lax.rsqrt(ms + EPS) * g


def _sigmoid(x):
    return 1.0 / (1.0 + jnp.exp(-x))


def _gelu_tanh(x):
    c = 0.7978845608028654
    return x * (0.5 * (1.0 + jnp.tanh(c * (x + 0.044715 * (x * x * x)))))


def _dot(a, b):
    return jnp.dot(a, b, preferred_element_type=F32)


def _head_mean_sq(x, ones_bd):
    sq = x * x
    hi = sq.astype(BF16)
    lo = (sq - hi.astype(F32)).astype(BF16)
    return (_dot(hi, ones_bd) + _dot(lo, ones_bd)) * (1.0 / HEAD_DIM)


def _rope_flat(x, cos_t, sin_t):
    width = x.shape[-1]
    lane = lax.broadcasted_iota(jnp.int32, x.shape, 1)
    upper = (lane & (HEAD_DIM // 2)) != 0
    partner = jnp.where(upper, pltpu.roll(x, HEAD_DIM // 2, 1),
                        pltpu.roll(x, width - HEAD_DIM // 2, 1))
    return x * cos_t + partner * sin_t


def _tile_lanes(t, width):
    reps = width // t.shape[-1]
    return t if reps == 1 else jnp.concatenate([t] * reps, axis=1)


SUBLANES = 8


def _segment_perm(ts):
    seg_len = ts // SUBLANES
    dst = jnp.arange(ts)
    src = (dst % SUBLANES) * seg_len + dst // SUBLANES
    return (src[:, None] == jnp.arange(ts)[None, :]).astype(BF16)


def _scan_segments(a, b, h_in):
    n_steps = a.shape[0] // SUBLANES
    vreg = lambda x, j: x[j * SUBLANES:(j + 1) * SUBLANES]
    h_loc, a_cum = [vreg(b, 0)], [vreg(a, 0)]
    for j in range(1, n_steps):
        h_loc.append(vreg(a, j) * h_loc[-1] + vreg(b, j))
        a_cum.append(vreg(a, j) * a_cum[-1])
    seg_a, seg_h = a_cum[-1], h_loc[-1]
    carry = [h_in]
    for s in range(SUBLANES):
        carry.append(seg_a[s:s + 1] * carry[-1] + seg_h[s:s + 1])
    enter = jnp.concatenate(carry[:SUBLANES], axis=0)
    h = jnp.concatenate([h_loc[j] + a_cum[j] * enter for j in range(n_steps)], axis=0)
    return h, carry[SUBLANES]


def _rec_kernel(x_ref, xnext_ref, g_ref, perm_ref, perm_t_ref, win_ref, cw_ref, cb_ref, wg_ref,
                gb_ref, lam_ref, wout_ref, o_ref, tail, hcar, z_first):
    R = LRU_WIDTH
    sub_rows = perm_ref.shape[0]
    n_sub = x_ref.shape[0] // sub_rows
    n_steps = sub_rows // SUBLANES
    taps = CONV_WIDTH - 1

    def project(rows_f32):
        ut = _dot(perm_ref[...], _rms(rows_f32, g_ref[...]).astype(BF16)).astype(BF16)
        return _dot(ut, win_ref[...])

    @pl.when(pl.program_id(1) == 0)
    def _():
        tail[...] = jnp.zeros_like(tail)
        hcar[...] = jnp.zeros_like(hcar)

    @pl.when((pl.program_id(0) == 0) & (pl.program_id(1) == 0))
    def _():
        z_first[...] = project(x_ref[0:sub_rows, :])

    x = x_ref[...]
    in_proj = lambda t: project(x[t * sub_rows:(t + 1) * sub_rows])

    sub = lax.broadcasted_iota(jnp.int32, (SUBLANES, LRU_BLOCK), 0)
    softplus_neg = jnp.maximum(-lam_ref[...], 0.0) + jnp.log1p(jnp.exp(-jnp.abs(lam_ref[...])))
    log_a_scale = -LRU_C * softplus_neg

    def conv_and_gates(z, prev_tail):
        conv_out, gate_pre = [], []
        for hh in range(LRU_HEADS):
            cs = slice(hh * LRU_BLOCK, (hh + 1) * LRU_BLOCK)
            xc = z[:, R + hh * LRU_BLOCK:R + (hh + 1) * LRU_BLOCK]
            wrapped = []
            for k in range(taps):
                cur = xc[(n_steps - taps + k) * SUBLANES:(n_steps - taps + k + 1) * SUBLANES]
                prev = prev_tail[k * SUBLANES:(k + 1) * SUBLANES, cs]
                wrapped.append(pltpu.roll(jnp.where(sub == SUBLANES - 1, prev, cur), 1, 0))
            back = lambda d: jnp.concatenate(
                wrapped[taps - d:] + [xc[0:(n_steps - d) * SUBLANES]], axis=0)
            cw = cw_ref[:, cs]
            xr = cb_ref[:, cs] + back(3) * cw[0:1]
            xr = xr + back(2) * cw[1:2]
            xr = xr + back(1) * cw[2:3]
            xr = xr + xc * cw[3:4]
            conv_out.append(xr)
            gate_pre.append(_dot(xr.astype(BF16), wg_ref[hh]))
        return conv_out, gate_pre

    def recur_and_gate(z, conv_out, gate_pre):
        gated = []
        for hh in range(LRU_HEADS):
            cs = slice(hh * LRU_BLOCK, (hh + 1) * LRU_BLOCK)
            xr, gates = conv_out[hh], gate_pre[hh]
            gb = gb_ref[:, cs]
            r = _sigmoid(gates[:, :LRU_BLOCK] + gb[0:1])
            i = _sigmoid(gates[:, LRU_BLOCK:] + gb[1:2])
            log_a = log_a_scale[:, cs] * r
            a = jnp.exp(log_a)
            bterm = jnp.sqrt(1.0 - a * a) * (i * xr)
            hs, hcar[:, cs] = _scan_segments(a, bterm, hcar[:, cs])
            gated.append((_gelu_tanh(z[:, cs]) * hs).astype(BF16))
        return jnp.concatenate(gated, axis=1)

    last_rows = slice((n_steps - taps) * SUBLANES, None)
    zs = {0: z_first[...]}
    staged = conv_and_gates(zs[0], tail[...])
    zs[1] = in_proj(1)
    for t in range(n_sub):
        gated = recur_and_gate(zs[t], *staged)
        if t + 1 < n_sub:
            staged = conv_and_gates(zs[t + 1], zs[t][last_rows, R:])
        yh = _dot(perm_t_ref[...], gated).astype(BF16)
        rows = slice(t * sub_rows, (t + 1) * sub_rows)
        o_ref[rows, :] = x[rows] + _dot(yh, wout_ref[...])
        if t + 2 < n_sub:
            zs[t + 2] = in_proj(t + 2)
        elif t + 2 == n_sub:
            z_first[...] = project(xnext_ref[...])
    tail[...] = zs[n_sub - 1][last_rows, R:]


def _recurrent_block(h, norm_g, w_in, conv_w, conv_b, gate_w, gate_b, lam, w_out, ts=1024,
                     sub_rows=256):
    B, S, D = h.shape
    R = LRU_WIDTH
    wg = jnp.concatenate([gate_w[0], gate_w[1]], axis=-1).astype(BF16)
    perm = _segment_perm(sub_rows)
    n_t, n_sub = S // ts, ts // sub_rows
    assert n_sub >= 2

    def next_first_rows(b, s):
        nxt = jnp.minimum(b * n_t + s + 1, B * n_t - 1)
        return nxt // n_t, (nxt % n_t) * n_sub, 0

    return pl.pallas_call(
        _rec_kernel,
        grid=(B, n_t),
        in_specs=[
            pl.BlockSpec((None, ts, D), lambda b, s: (b, s, 0)),
            pl.BlockSpec((None, sub_rows, D), next_first_rows),
            _const_spec((1, D)),
            _const_spec((sub_rows, sub_rows)),
            _const_spec((sub_rows, sub_rows)),
            _const_spec((D, 2 * R)),
            _const_spec((CONV_WIDTH, R)),
            _const_spec((1, R)),
            _const_spec((LRU_HEADS, LRU_BLOCK, 2 * LRU_BLOCK)),
            _const_spec((2, R)),
            _const_spec((1, R)),
            _const_spec((R, D)),
        ],
        out_specs=pl.BlockSpec((None, ts, D), lambda b, s: (b, s, 0)),
        out_shape=jax.ShapeDtypeStruct((B, S, D), F32),
        scratch_shapes=[pltpu.VMEM(((CONV_WIDTH - 1) * SUBLANES, R), F32), pltpu.VMEM((1, R), F32),
                        pltpu.VMEM((sub_rows, 2 * R), F32)],
        compiler_params=_cparams(2),
        name="rglru_block",
    )(h, h, norm_g.reshape(1, D), perm, perm.T, w_in.astype(BF16), conv_w, conv_b.reshape(1, R), wg,
      gate_b, lam.reshape(1, R), w_out.astype(BF16))


FFN_CHUNK = FFN_HIDDEN // 2


def _ffn_body(x, g_ref, win_ref, wout_ref, o_ref):
    u = _rms(x, g_ref[...]).astype(BF16)

    acc = x
    for c in range(FFN_HIDDEN // FFN_CHUNK):
        lo = c * FFN_CHUNK
        gate = _dot(u, win_ref[:, lo:lo + FFN_CHUNK])
        up = _dot(u, win_ref[:, FFN_HIDDEN + lo:FFN_HIDDEN + lo + FFN_CHUNK])
        act = ((gate * _sigmoid(gate)) * up).astype(BF16)
        acc = acc + _dot(act, wout_ref[lo:lo + FFN_CHUNK, :])
    o_ref[...] = acc


def _ffn_kernel(x_ref, g_ref, win_ref, wout_ref, o_ref):
    _ffn_body(x_ref[...], g_ref, win_ref, wout_ref, o_ref)


def _proj_ffn_kernel(x_ref, a_ref, wo_ref, g_ref, win_ref, wout_ref, o_ref):
    _ffn_body(x_ref[...] + _dot(a_ref[...], wo_ref[...]), g_ref, win_ref, wout_ref, o_ref)


def _swiglu(h, norm_g, w_in, w_out, attn=None, w_o=None, tm=512):
    B, S, D = h.shape
    M = B * S
    row_spec = pl.BlockSpec((tm, D), lambda i: (i, 0))
    w_specs = [_const_spec((1, D)), _const_spec((D, 2 * FFN_HIDDEN)), _const_spec((FFN_HIDDEN, D))]
    w_args = (norm_g.reshape(1, D), w_in.astype(BF16), w_out.astype(BF16))
    if attn is None:
        kern, specs, args = _ffn_kernel, [row_spec] + w_specs, (h.reshape(M, D),) + w_args
    else:
        kern = _proj_ffn_kernel
        specs = [row_spec, row_spec, _const_spec((D, D))] + w_specs
        args = (h.reshape(M, D), attn.reshape(M, D), w_o.astype(BF16)) + w_args
    out = pl.pallas_call(
        kern,
        grid=(M // tm,),
        in_specs=specs,
        out_specs=row_spec,
        out_shape=jax.ShapeDtypeStruct((M, D), F32),
        compiler_params=_cparams(1),
        name="swiglu_ffn",
    )(*args)
    return out.reshape(B, S, D)


def _kv_kernel(x_ref, xnext_ref, g_ref, w_ref, kn_ref, cos_ref, sin_ref, ones_ref,
               kvc_ref, ks_ref, vs_ref, kw_ref, vw_ref, kv_first):
    W = KV_WIDTH
    T = PROJ_ROWS
    n_sub = x_ref.shape[0] // T
    ones_bd = ones_ref[...]
    project = lambda rows_f32: _dot(_rms(rows_f32, g_ref[...]).astype(BF16), w_ref[...])

    @pl.when((pl.program_id(0) == 0) & (pl.program_id(1) == 0))
    def _():
        kv_first[...] = project(x_ref[0:T, :])

    kv_next = kv_first[...]
    for t in range(n_sub):
        kv = kv_next
        rows = slice(t * T, (t + 1) * T)
        part = lambda j: kv[:, j * W:(j + 1) * W]
        mean_sq = {j: _head_mean_sq(part(j), ones_bd) for j in (2, 4)}
        if t + 1 < n_sub:
            kv_next = project(x_ref[(t + 1) * T:(t + 2) * T, :])
        else:
            kv_first[...] = project(xnext_ref[...])
        cos_t = _tile_lanes(cos_ref[rows, :], W)
        sin_t = _tile_lanes(sin_ref[rows, :], W)
        kvc_ref[rows, :] = kv[:, 0:2 * W]
        for j, gain, k_ref in ((2, kn_ref[1:2, :], ks_ref), (4, kn_ref[2:3, :], kw_ref)):
            k = part(j) * lax.rsqrt(mean_sq[j] + EPS) * gain
            k = _rope_flat(k, cos_t, sin_t).astype(BF16)
            for g in range(N_KV_GROUPS):
                k_ref[g, rows, :] = k[:, g * HEAD_DIM:(g + 1) * HEAD_DIM]
        for j, vt_ref in ((3, vs_ref), (5, vw_ref)):
            v_t = part(j).T
            for g in range(N_KV_GROUPS):
                for i in range(T // K_TILE):
                    tile = t * (T // K_TILE) + i
                    vt_ref[g, tile, 0:HEAD_DIM, :] = v_t[g * HEAD_DIM:(g + 1) * HEAD_DIM,
                                                         i * K_TILE:(i + 1) * K_TILE].astype(BF16)
                    vt_ref[g, tile, HEAD_DIM:VT_ROWS, :] = jnp.ones((VT_ROWS - HEAD_DIM, K_TILE), BF16)


def _rope_tables(pos):
    half = HEAD_DIM // 2
    freqs = jnp.power(ROPE_THETA, -jnp.arange(half, dtype=F32) / half)
    ang = pos.astype(F32)[:, None] * freqs[None, :]
    cos, sin = jnp.cos(ang), jnp.sin(ang)
    cos_t = jnp.concatenate([cos, cos], axis=-1)
    sin_t = jnp.concatenate([-sin, sin], axis=-1)
    return cos_t, sin_t


def _block_diag_ones(width):
    seg = jnp.arange(width) // HEAD_DIM
    return (seg[:, None] == seg[None, :]).astype(BF16)


def _kv_proj(h, kv_norm, kv_w, k_norm, cos128, sin128, ts=512):
    B, S, D = h.shape
    W = KV_WIDTH
    kn = jnp.tile(k_norm, (1, N_KV_GROUPS))
    G = N_KV_GROUPS
    row = lambda width: pl.BlockSpec((None, ts, width), lambda b, s: (b, s, 0))
    tab = pl.BlockSpec((ts, LANES), lambda b, s: (s, 0))
    key_spec = pl.BlockSpec((None, G, ts, HEAD_DIM), lambda b, s: (b, 0, s, 0))
    val_spec = pl.BlockSpec((None, G, ts // K_TILE, VT_ROWS, K_TILE), lambda b, s: (b, 0, s, 0, 0))
    flat = jax.ShapeDtypeStruct((B, S, 2 * W), F32)
    keys = jax.ShapeDtypeStruct((B, G, S, HEAD_DIM), BF16)
    vals = jax.ShapeDtypeStruct((B, G, S // K_TILE, VT_ROWS, K_TILE), BF16)
    n_t, n_sub = S // ts, ts // PROJ_ROWS

    def next_first_rows(b, s):
        nxt = jnp.minimum(b * n_t + s + 1, B * n_t - 1)
        return nxt // n_t, (nxt % n_t) * n_sub, 0

    return pl.pallas_call(
        _kv_kernel,
        grid=(B, n_t),
        in_specs=[row(D), pl.BlockSpec((None, PROJ_ROWS, D), next_first_rows),
                  _const_spec((1, D)), _const_spec((D, 6 * W)), _const_spec((3, W)),
                  tab, tab, _const_spec((W, W))],
        out_specs=[row(2 * W), key_spec, val_spec, key_spec, val_spec],
        out_shape=[flat, keys, vals, keys, vals],
        scratch_shapes=[pltpu.VMEM((PROJ_ROWS, 6 * W), F32)],
        compiler_params=_cparams(2),
        name="shared_kv_proj",
    )(h, h, kv_norm.reshape(1, D), kv_w.astype(BF16), kn, cos128, sin128, _block_diag_ones(W))


N_CHUNK = 128
CHUNK_W = CMP_STRIDE * HEAD_DIM


def _cmp_kernel(xa_ref, xb_ref, pos_ref, w1_ref, b1_ref, w2_ref, b2_ref, kn_ref, cos_ref, sin_ref, o_ref):
    groups = []
    for x_ref in (xa_ref, xb_ref):
        nth = [x_ref[pl.ds(r, N_CHUNK, stride=CMP_STRIDE), :] for r in range(CMP_STRIDE)]
        for j in range(LANES // HEAD_DIM):
            groups.append(jnp.concatenate([t[:, j * HEAD_DIM:(j + 1) * HEAD_DIM] for t in nth], axis=1))
    x = jnp.concatenate(groups, axis=0)
    rows = x.shape[0]
    ya = _dot((x + pos_ref[0:1, :]).astype(BF16), w1_ref[0:CHUNK_W, :])
    yb = _dot((x + pos_ref[1:2, :]).astype(BF16), w1_ref[CHUNK_W:2 * CHUNK_W, :])
    hid = _gelu_tanh(ya + pltpu.roll(yb, rows - 1, 0) + b1_ref[...])
    out = _dot(hid.astype(BF16), w2_ref[...]) + b2_ref[...]

    @pl.when(pl.program_id(0) == 0)
    def _():
        k = _rms(out, kn_ref[...])
        half = HEAD_DIM // 2
        partner = jnp.concatenate([k[:, half:], k[:, :half]], axis=1)
        o_ref[...] = k * cos_ref[...] + partner * sin_ref[...]

    @pl.when(pl.program_id(0) == 1)
    def _():
        o_ref[...] = out


def _compress(kvc, cmp_pos, cmp_w1, cmp_b1, cmp_w2, cmp_b2, k_norm0):
    B, S, _ = kvc.shape
    G = N_KV_GROUPS
    rows = G * N_CHUNK
    pos = cmp_pos.reshape(2, 2, CHUNK_W)
    cmp_last = jnp.arange(N_CHUNK) * CMP_STRIDE + CMP_BLOCK - 1
    cos_t, sin_t = _rope_tables(cmp_last)
    cos_t, sin_t = jnp.tile(cos_t, (G, 1)), jnp.tile(sin_t, (G, 1))
    per_kv = lambda *shape: pl.BlockSpec((None,) + shape, lambda k, b: (k,) + (0,) * len(shape))
    return pl.pallas_call(
        _cmp_kernel,
        grid=(2, B),
        in_specs=[pl.BlockSpec((None, S, LANES), lambda k, b: (b, 0, 2 * k)),
                  pl.BlockSpec((None, S, LANES), lambda k, b: (b, 0, 2 * k + 1)),
                  per_kv(2, CHUNK_W), per_kv(2 * CHUNK_W, CMP_HIDDEN), per_kv(1, CMP_HIDDEN),
                  per_kv(CMP_HIDDEN, HEAD_DIM), per_kv(1, HEAD_DIM),
                  _const_spec((1, HEAD_DIM)), _const_spec((rows, HEAD_DIM)),
                  _const_spec((rows, HEAD_DIM))],
        out_specs=pl.BlockSpec((None, None, rows, HEAD_DIM), lambda k, b: (k, b, 0, 0)),
        out_shape=jax.ShapeDtypeStruct((2, B, rows, HEAD_DIM), F32),
        compiler_params=_cparams(2),
        name="kv_compress",
    )(kvc, kvc, pos, cmp_w1.astype(BF16), cmp_b1.reshape(2, 1, CMP_HIDDEN), cmp_w2.astype(BF16),
      cmp_b2.reshape(2, 1, HEAD_DIM), k_norm0.reshape(1, HEAD_DIM), cos_t, sin_t)


GATE_PAD = LANES
LOG2_E = 1.4426950408889634
Q_SCALE = HEAD_DIM ** -0.5 * LOG2_E


def _q_kernel(x_ref, xnext_ref, g_ref, w_ref, gb_ref, qn_ref, cos_ref, sin_ref, ones_ref,
              q_ref, gate_ref, z_first):
    NQ = N_HEADS * HEAD_DIM
    T = PROJ_ROWS
    n_sub = x_ref.shape[0] // T
    ones_bd = ones_ref[...]
    W = ones_bd.shape[0]
    project = lambda rows_f32: _dot(_rms(rows_f32, g_ref[...]).astype(BF16), w_ref[...])

    @pl.when((pl.program_id(0) == 0) & (pl.program_id(1) == 0))
    def _():
        z_first[...] = project(x_ref[0:T, :])

    z_next = z_first[...]
    for t in range(n_sub):
        z = z_next
        rows = slice(t * T, (t + 1) * T)
        chunks = [z[:, c * W:(c + 1) * W] for c in range(NQ // W)]
        mean_sq = [_head_mean_sq(q, ones_bd) for q in chunks]
        if t + 1 < n_sub:
            z_next = project(x_ref[(t + 1) * T:(t + 2) * T, :])
        else:
            z_first[...] = project(xnext_ref[...])
        gate_ref[:, rows] = _sigmoid(z[:, NQ:] + gb_ref[...]).T[0:3 * N_HEADS, :]
        cos_t = _tile_lanes(cos_ref[rows, :], W)
        sin_t = _tile_lanes(sin_ref[rows, :], W)
        for c, q in enumerate(chunks):
            q = q * lax.rsqrt(mean_sq[c] + EPS) * qn_ref[...]
            q_tr = (_rope_flat(q, cos_t, sin_t) * Q_SCALE).T
            for i in range(T // Q_TILE):
                q_ref[c, t * (T // Q_TILE) + i] = jnp.concatenate(
                    [q_tr[j * HEAD_DIM:(j + 1) * HEAD_DIM, i * Q_TILE:(i + 1) * Q_TILE]
                     for j in range(W // HEAD_DIM)], axis=1).astype(BF16)


def _q_proj(h, norm_g, w_in, gate_b, q_norm_g, cos128, sin128, ts=512):
    B, S, D = h.shape
    NQ = N_HEADS * HEAD_DIM
    n_gate = 3 * N_HEADS
    w = jnp.pad(w_in, ((0, 0), (0, GATE_PAD - n_gate))).astype(BF16)
    gb = jnp.pad(gate_b, (0, GATE_PAD - n_gate)).reshape(1, GATE_PAD)
    W = KV_WIDTH
    qn = jnp.tile(q_norm_g, W // HEAD_DIM).reshape(1, W)
    row = lambda width: pl.BlockSpec((None, ts, width), lambda b, s: (b, s, 0))
    tab = pl.BlockSpec((ts, LANES), lambda b, s: (s, 0))
    n_t, n_sub = S // ts, ts // PROJ_ROWS

    def next_first_rows(b, s):
        nxt = jnp.minimum(b * n_t + s + 1, B * n_t - 1)
        return nxt // n_t, (nxt % n_t) * n_sub, 0

    return pl.pallas_call(
        _q_kernel,
        grid=(B, n_t),
        in_specs=[row(D), pl.BlockSpec((None, PROJ_ROWS, D), next_first_rows),
                  _const_spec((1, D)), _const_spec((D, NQ + GATE_PAD)),
                  _const_spec((1, GATE_PAD)), _const_spec((1, W)), tab, tab, _const_spec((W, W))],
        out_specs=[pl.BlockSpec((None, N_KV_GROUPS, ts // Q_TILE, HEAD_DIM, HEADS_PER_GROUP * Q_TILE),
                                lambda b, s: (b, 0, s, 0, 0)),
                   pl.BlockSpec((None, n_gate, ts), lambda b, s: (b, 0, s))],
        out_shape=[jax.ShapeDtypeStruct((B, N_KV_GROUPS, S // Q_TILE, HEAD_DIM,
                                         HEADS_PER_GROUP * Q_TILE), BF16),
                   jax.ShapeDtypeStruct((B, n_gate, S), F32)],
        scratch_shapes=[pltpu.VMEM((PROJ_ROWS, NQ + GATE_PAD), F32)],
        compiler_params=_cparams(2),
        name="nsa_q_proj",
    )(h, h, norm_g.reshape(1, D), w, gb, qn, cos128, sin128, _block_diag_ones(W))


N_SLC = 32
PS_PAD = 8
SCORE_PAIRS_AHEAD = 2


def _flash_step(slot, parts, v_t, m_scr, acc_scr):
    T = Q_TILE
    m_old = m_scr[slot]
    m_new, p_all = [], []
    for h in range(HEADS_PER_GROUP):
        hs = slice(h * T, (h + 1) * T)
        sm = [s[:, hs].astype(BF16) if bias is None else s[:, hs].astype(BF16) + bias
              for s, bias in parts]
        m_h = m_old[:, hs]
        for x in sm:
            m_h = jnp.maximum(m_h, jnp.max(x, axis=0, keepdims=True).astype(F32))
        m_new.append(m_h)
        p_all.append(jnp.concatenate([jnp.exp2(x - m_h.astype(BF16)) for x in sm], axis=0))
    m_new = jnp.concatenate(m_new, axis=1)
    alpha = jnp.exp2(m_old - m_new)
    m_scr[slot] = m_new
    acc_scr[slot] = alpha * acc_scr[slot] + _dot(v_t, jnp.concatenate(p_all, axis=1))


def _attn_kernel(q_ref, gt_ref, kc_ref, vct_ref, ks_ref, vst_ref, kw_ref, vwt_ref, o_ref,
                 sel_scr, ps_scr, m_scr, acc_scr, ot_scr):
    T = Q_TILE
    G = N_KV_GROUPS
    HG = HEADS_PER_GROUP
    NL = HG * T
    qi = pl.program_id(1)
    tpos = qi * T + lax.broadcasted_iota(jnp.int32, (1, T), 1)
    krow = lax.broadcasted_iota(jnp.int32, (K_TILE, T), 0)
    q_t = lambda g: q_ref[g]
    key_tile = lambda k_ref, g, kt: k_ref[g, pl.ds(pl.multiple_of(kt * K_TILE, K_TILE), K_TILE), :]

    def gate(branch, g):
        r0 = branch * N_HEADS + HG * g
        return jnp.concatenate([gt_ref[r0 + h:r0 + h + 1, :] for h in range(HG)], axis=1)

    def emit(g, o_t, first):
        for h in range(HG):
            rows = slice((HG * g + h) * HEAD_DIM, (HG * g + h + 1) * HEAD_DIM)
            piece = o_t[:, h * T:(h + 1) * T]
            ot_scr[rows, :] = piece if first else ot_scr[rows, :] + piece

    def reset_state():
        for g in range(G):
            m_scr[g] = jnp.full((1, NL), NEG, F32)
            acc_scr[g] = jnp.zeros((VT_ROWS, NL), F32)

    def emit_state(branch):
        for g in range(G):
            denom = acc_scr[g, HEAD_DIM:HEAD_DIM + 1, :]
            emit(g, acc_scr[g, 0:HEAD_DIM, :] * (gate(branch, g) / denom), first=False)

    def sweep(k_ref, vt_ref, tiles, biases, scores=None):
        pairs = [range(lo, min(lo + 2, len(tiles))) for lo in range(0, len(tiles), 2)]
        scores = dict(scores or {})

        def score_matmuls(pair, g):
            for i in pair:
                if (i, g) not in scores:
                    scores[i, g] = _dot(key_tile(k_ref, g, tiles[i]), q_t(g))

        for pair in pairs[:SCORE_PAIRS_AHEAD]:
            for g in range(G):
                score_matmuls(pair, g)
        for n, pair in enumerate(pairs):
            for g in range(G):
                parts = [(scores[i, g], None if biases[i] is None else biases[i](g)) for i in pair]
                v_t = jnp.concatenate([vt_ref[g, tiles[i]] for i in pair], axis=1)
                _flash_step(g, parts, v_t, m_scr, acc_scr)
                if n + SCORE_PAIRS_AHEAD < len(pairs):
                    score_matmuls(pairs[n + SCORE_PAIRS_AHEAD], g)

    def sweep_range(k_ref, vt_ref, n_tiles, bias_of):
        def run(tiles):
            sweep(k_ref, vt_ref, tiles,
                  [None if bias_of is None else functools.partial(bias_of, kt=kt) for kt in tiles])

        def quad(j, c):
            run([4 * j + i for i in range(4)])
            return c

        lax.fori_loop(0, n_tiles // 4, quad, 0)
        rem = n_tiles % 4
        base = n_tiles - rem

        @pl.when(rem >= 2)
        def _():
            run([base, base + 1])

        @pl.when(rem % 2 == 1)
        def _():
            run([n_tiles - 1])

    n_back = WINDOW // K_TILE

    def causal_bias(g):
        return jnp.where((qi * K_TILE + krow) <= tpos, 0.0, -jnp.inf).astype(BF16)

    def window_edge_bias(g):
        return jnp.where(((qi - n_back) * K_TILE + krow) > (tpos - WINDOW), 0.0,
                         -jnp.inf).astype(BF16)

    def compressed_and_window(tiles, biases):
        cmp_scores = [_dot(kc_ref[g], q_t(g)) for g in range(G)]
        ahead = {(i, g): _dot(key_tile(kw_ref, g, tiles[i]), q_t(g))
                 for i in range(min(len(tiles), 2 * SCORE_PAIRS_AHEAD)) for g in range(G)}
        cvalid = ((krow * CMP_STRIDE + (CMP_BLOCK - 1)) <= tpos) & (krow < N_CHUNK - 1)
        cmp_bias = jnp.where(cvalid, 0.0, -jnp.inf)
        cmp_probs = []
        for g in range(G):
            psum = jnp.zeros((N_CHUNK, T), F32)
            probs = []
            for h in range(HG):
                sm = cmp_scores[g][:, h * T:(h + 1) * T] + cmp_bias
                e = jnp.exp2(sm - jnp.maximum(jnp.max(sm, axis=0, keepdims=True), NEG))
                den = jnp.sum(e, axis=0, keepdims=True)
                p = e / jnp.where(den > 0.0, den, 1.0)
                psum = psum + p
                probs.append(p.astype(BF16))
            cmp_probs.append(jnp.concatenate(probs, axis=1))
            ps_scr[g, 0:PS_PAD, :] = jnp.zeros((PS_PAD, T), F32)
            ps_scr[g, PS_PAD:PS_PAD + N_CHUNK, :] = psum
        for g in range(G):
            emit(g, gate(0, g) * _dot(vct_ref[g], cmp_probs[g]), first=True)
        reset_state()
        sweep(kw_ref, vwt_ref, tiles, biases, scores=ahead)

    @pl.when(qi >= n_back)
    def _():
        compressed_and_window([qi - i for i in range(n_back + 1)],
                              [causal_bias] + [None] * (n_back - 1) + [window_edge_bias])

    @pl.when(qi < n_back)
    def _():
        compressed_and_window([qi], [causal_bias])
        sweep_range(kw_ref, vwt_ref, qi, None)
    emit_state(2)

    all_causal_fit = (qi * T + T - 1) // SLC_BLOCK + 1 <= SLC_TOPK

    @pl.when(all_causal_fit)
    def _():
        for g in range(G):
            sel_scr[g] = jnp.zeros((N_SLC, T), F32)

    @pl.when(jnp.logical_not(all_causal_fit))
    def _():
        ROWS = 8
        jrow = lax.broadcasted_iota(jnp.int32, (N_SLC, T), 0)
        cur = tpos // SLC_BLOCK
        causal_blk = jrow <= cur
        forced = (jrow == 0) | (causal_blk & ((cur - jrow) < N_LOCAL_BLOCKS))
        for g in range(G):
            tap = lambda k: ps_scr[g, pl.ds(PS_PAD + k, N_SLC, stride=4), :]
            imp = 0.5 * tap(-1) + tap(0) + tap(1) + tap(2) + 0.5 * tap(3)
            score = jnp.where(forced, FORCE, jnp.where(causal_blk, imp, NEG))
            parts = [score[r:r + ROWS] for r in range(0, N_SLC, ROWS)]
            ranks = [jnp.zeros((ROWS, T), F32) for _ in parts]
            for j2 in range(N_SLC):
                other = score[j2:j2 + 1, :]
                for i, part in enumerate(parts):
                    r0 = i * ROWS
                    if r0 + ROWS - 1 < j2:
                        beats = other > part
                    elif r0 > j2:
                        beats = other >= part
                    else:
                        beats = (other > part) | ((other == part) & (jrow[r0:r0 + ROWS] > j2))
                    ranks[i] = ranks[i] + jnp.where(beats, 1.0, 0.0)
            rank = jnp.concatenate(ranks, axis=0)
            sel_scr[g] = jnp.where(rank < SLC_TOPK, 0.0, -jnp.inf)

    def slc_bias(g, kt):
        half = K_TILE // 2
        top = jnp.broadcast_to(sel_scr[g, pl.ds(2 * kt, 1), :], (half, T))
        bot = jnp.broadcast_to(sel_scr[g, pl.ds(2 * kt + 1, 1), :], (half, T))
        causal = jnp.where((kt * K_TILE + krow) <= tpos, 0.0, -jnp.inf)
        return (jnp.concatenate([top, bot], axis=0) + causal).astype(BF16)

    reset_state()
    sweep_range(ks_ref, vst_ref, qi + 1, slc_bias)
    emit_state(1)

    o_ref[...] = ot_scr[...].T.astype(BF16)


def _nsa_attention(q, gates_t, k_cmp, v_cmp, ks, vst, kw, vwt):
    B, _, S, _ = ks.shape
    G, T = N_KV_GROUPS, Q_TILE
    NQ = N_HEADS * HEAD_DIM
    NT = S // K_TILE
    kc = k_cmp.reshape(B, G, N_CHUNK, HEAD_DIM).astype(BF16)
    vct = v_cmp.reshape(B, G, N_CHUNK, HEAD_DIM).transpose(0, 1, 3, 2).astype(BF16)
    per_b = lambda *shape: pl.BlockSpec((None,) + shape, lambda b, i: (b,) + (0,) * len(shape))
    return pl.pallas_call(
        _attn_kernel,
        grid=(B, S // T),
        in_specs=[pl.BlockSpec((None, G, None, HEAD_DIM, HEADS_PER_GROUP * T),
                               lambda b, i: (b, 0, i, 0, 0)),
                  pl.BlockSpec((None, 3 * N_HEADS, T), lambda b, i: (b, 0, i)),
                  per_b(G, N_CHUNK, HEAD_DIM), per_b(G, HEAD_DIM, N_CHUNK),
                  per_b(G, S, HEAD_DIM), per_b(G, NT, VT_ROWS, K_TILE),
                  per_b(G, S, HEAD_DIM), per_b(G, NT, VT_ROWS, K_TILE)],
        out_specs=pl.BlockSpec((None, T, NQ), lambda b, i: (b, i, 0)),
        out_shape=jax.ShapeDtypeStruct((B, S, NQ), BF16),
        scratch_shapes=[pltpu.VMEM((G, N_SLC, T), F32), pltpu.VMEM((G, PS_PAD + N_CHUNK, T), F32),
                        pltpu.VMEM((G, 1, HEADS_PER_GROUP * T), F32),
                        pltpu.VMEM((G, VT_ROWS, HEADS_PER_GROUP * T), F32),
                        pltpu.VMEM((NQ, T), F32)],
        compiler_params=_cparams(2),
        name="nsa_attention",
    )(q, gates_t, kc, vct, ks, vst, kw, vwt)


def kernel(x, a_norm, a_w_in, a_conv_w, a_conv_b, a_gate_w, a_gate_b, a_lambda, a_w_out,
           kv_norm, kv_w, k_norm, cmp_pos, cmp_w1, cmp_b1, cmp_w2, cmp_b2,
           b_norm, b_w_in, b_gate_b, q_norm, b_w_out, f_norm, f_w_in, f_w_out):
    B, S, D = x.shape
    assert D == D_MODEL and S == N_SLC * SLC_BLOCK and S == N_CHUNK * CMP_STRIDE
    n_a = a_norm.shape[0]
    n_b = b_norm.shape[0]
    h = x
    for i in range(n_a):
        h = _recurrent_block(h, a_norm[i], a_w_in[i], a_conv_w[i], a_conv_b[i], a_gate_w[i],
                             a_gate_b[i], a_lambda[i], a_w_out[i])
        h = _swiglu(h, f_norm[i], f_w_in[i], f_w_out[i])

    cos_t, sin_t = _rope_tables(jnp.arange(S))
    cos128, sin128 = jnp.tile(cos_t, (1, 2)), jnp.tile(sin_t, (1, 2))
    kvc, ks, vs, kw, vw = _kv_proj(h, kv_norm, kv_w, k_norm, cos128, sin128)
    cmp = _compress(kvc, cmp_pos, cmp_w1, cmp_b1, cmp_w2, cmp_b2, k_norm[0])
    for j in range(n_b):
        q, gates = _q_proj(h, b_norm[j], b_w_in[j], b_gate_b[j], q_norm[j], cos128, sin128)
        o = _nsa_attention(q, gates, cmp[0], cmp[1], ks, vs, kw, vw)
        layer = n_a + j
        h = _swiglu(h, f_norm[layer], f_w_in[layer], f_w_out[layer], attn=o, w_o=b_w_out[j])
    return h
```
